```python
import math
import jax, jax.numpy as jnp
from jax import lax
import numpy as np

D_MODEL = 2048
BATCH = 4
SEQ = 4096
DEPTH = 1

GRID_W = 64
CTX_LEN = 256

D_SSD = D_MODEL
SSD_HEAD_DIM = 64
SSD_HEADS = D_SSD // SSD_HEAD_DIM
SSD_GROUPS = 8
SSD_REP = SSD_HEADS // SSD_GROUPS
SSD_STATE = 128
SSD_CONV = 3
SSD_CHUNK = 128

D_POOL = D_MODEL
POOL_WINDOWS = (2, 4, 8, 16)
POOL_GROUPS = len(POOL_WINDOWS)
POOL_GW = D_POOL // POOL_GROUPS

D_FF = 5632
FFN_CONV = 3

EPS = 1e-6

COL_DT = 2 * SSD_HEADS
COL_BC = SSD_GROUPS * SSD_STATE
N_XBC = D_SSD + 2 * COL_BC
N_STATE_COLS = COL_DT + D_SSD + COL_BC
IN_SIZES = (COL_DT, N_XBC, D_SSD, D_POOL, D_MODEL, D_MODEL)
IN_COLS = sum(IN_SIZES)

kernel_name = "hybrid_ssd_pool_convglu_prefix_block"


def rms_norm(x, w):
    x32 = x.astype(jnp.float32)
    y = x32 * lax.rsqrt(jnp.mean(x32 * x32, axis=-1, keepdims=True) + EPS)
    return (y * w.astype(jnp.float32)).astype(x.dtype)


def modulate(x, w, shift, scale):
    return rms_norm(x, w) * (1 + scale) + shift


def dwconv_axis1(x, w, b):
    k = w.shape[0]
    n = x.shape[1]
    p = k // 2
    xp = jnp.pad(x, [(0, 0), (p, p)] + [(0, 0)] * (x.ndim - 2))
    out = xp[:, 0:n] * w[0] + b
    for j in range(1, k):
        out = out + xp[:, j:j + n] * w[j]
    return out


def centred_pool_residual(v, window):
    n = v.shape[-2]
    v32 = v.astype(jnp.float32)
    csum = jnp.cumsum(v32, axis=-2)
    prefix = jnp.pad(csum, [(0, 0)] * (v.ndim - 2) + [(1, 0), (0, 0)])
    t = jnp.arange(n)
    start = jnp.clip(t - window // 2, 0, n)
    end = jnp.clip(t + window - window // 2, 0, n)
    total = jnp.take(prefix, end, axis=-2) - jnp.take(prefix, start, axis=-2)
    count = (end - start).astype(jnp.float32)[:, None]
    return (total / count - v32).astype(v.dtype)


def pool_mixer(v, pool_w, pool_scale, grid):
    b, l, _ = v.shape
    if grid:
        v = v.reshape(b, l // GRID_W, GRID_W, D_POOL)
    groups = jnp.split(v, POOL_GROUPS, axis=-1)
    p = jnp.stack([centred_pool_residual(g, w) for g, w in zip(groups, POOL_WINDOWS)], axis=-2)
    p = jnp.einsum('...gi,gio->...go', p, pool_w).reshape(b, l, D_POOL)
    return p * pool_scale


def _ssd_chunks(xs, dt, A, B):
    b, l = xs.shape[:2]
    nc = l // SSD_CHUNK
    xd = (xs.astype(jnp.float32) * dt[..., None]).reshape(b, nc, SSD_CHUNK, SSD_GROUPS, SSD_REP, SSD_HEAD_DIM)
    a_cs = jnp.cumsum((dt * A).reshape(b, nc, SSD_CHUNK, SSD_GROUPS, SSD_REP), axis=2)
    bc = B.astype(jnp.float32).reshape(b, nc, SSD_CHUNK, SSD_GROUPS, SSD_STATE)
    return xd, a_cs, bc


def _ssd_carry(xd, a_cs, bc, h0):
    decay_to_end = jnp.exp(a_cs[:, :, -1:] - a_cs)
    states = jnp.einsum('bcsgn,bcsgr,bcsgrp->bcgrpn', bc, decay_to_end, xd)
    chunk_decay = jnp.exp(a_cs[:, :, -1])

    def step(h, inp):
        s, d = inp
        return h * d[..., None, None] + s, h

    h_final, h_in = lax.scan(step, h0, (jnp.moveaxis(states, 1, 0), jnp.moveaxis(chunk_decay, 1, 0)))
    return jnp.moveaxis(h_in, 0, 1), h_final


def ssd_scan(xs, dt, A, B, C, d_skip, h0):
    xd, a_cs, bc = _ssd_chunks(xs, dt, A, B)
    h_in, h_final = _ssd_carry(xd, a_cs, bc, h0)
    b, nc = xd.shape[:2]
    cc = C.astype(jnp.float32).reshape(b, nc, SSD_CHUNK, SSD_GROUPS, SSD_STATE)
    seg = a_cs[:, :, :, None] - a_cs[:, :, None]
    order = jnp.tril(jnp.ones((SSD_CHUNK, SSD_CHUNK), dtype=bool))[:, :, None, None]
    decay = jnp.exp(jnp.where(order, seg, -jnp.inf))
    cb = jnp.einsum('bclgn,bcsgn->bclsg', cc, bc)
    y_diag = jnp.einsum('bclsg,bclsgr,bcsgrp->bclgrp', cb, decay, xd)
    y_off = jnp.einsum('bclgn,bcgrpn,bclgr->bclgrp', cc, h_in, jnp.exp(a_cs))
    y = (y_diag + y_off).reshape(xs.shape) + d_skip[..., None] * xs.astype(jnp.float32)
    return y, h_final


def _flip(t):
    return jnp.flip(t, axis=1)


def _ssd_prepare(dt_raw, xbc, conv_w, conv_b, dt_bias, a_log):
    b, l, _ = xbc.shape
    xbc = jax.nn.silu(dwconv_axis1(xbc, conv_w, conv_b))
    xs = xbc[..., :D_SSD].reshape(b, l, SSD_GROUPS, SSD_REP, SSD_HEAD_DIM)
    B = xbc[..., D_SSD:D_SSD + COL_BC].reshape(b, l, SSD_GROUPS, SSD_STATE)
    rest = xbc[..., D_SSD + COL_BC:]
    dt = jax.nn.softplus(dt_raw.astype(jnp.float32).reshape(b, l, 2, SSD_GROUPS, SSD_REP)
                         + dt_bias.astype(jnp.float32).reshape(2, SSD_GROUPS, SSD_REP))
    A = -jnp.exp(a_log.astype(jnp.float32)).reshape(2, SSD_GROUPS, SSD_REP)
    return xs, B, rest, dt, A


def _zero_state(b):
    return jnp.zeros((b, SSD_GROUPS, SSD_REP, SSD_HEAD_DIM, SSD_STATE), jnp.float32)


def context_ssd_states(h, lp):
    b = h.shape[0]
    proj = h @ lp['w_in'][:, :N_STATE_COLS]
    xs, B, _, dt, A = _ssd_prepare(proj[..., :COL_DT], proj[..., COL_DT:],
                                   lp['ssd_conv_w'][:, :D_SSD + COL_BC], lp['ssd_conv_b'][:D_SSD + COL_BC],
                                   lp['dt_bias'], lp['a_log'])
    xd, a_cs, bc = _ssd_chunks(xs, dt[:, :, 0], A[0], B)
    h_f = _ssd_carry(xd, a_cs, bc, _zero_state(b))[1]
    xd, a_cs, bc = _ssd_chunks(_flip(xs), _flip(dt[:, :, 1]), A[1], _flip(B))
    h_b = _ssd_carry(xd, a_cs, bc, _zero_state(b))[1]
    return h_f, h_b


def token_mixer(h, lp, h0_f, h0_b, grid):
    b, l, _ = h.shape
    proj = h @ lp['w_in']
    idx = np.cumsum(IN_SIZES)[:-1].tolist()
    dt_raw, xbc, z, v, g_ssd, g_pool = jnp.split(proj, idx, axis=-1)

    xs, B, c_flat, dt, A = _ssd_prepare(dt_raw, xbc, lp['ssd_conv_w'], lp['ssd_conv_b'], lp['dt_bias'], lp['a_log'])
    C = c_flat.reshape(b, l, SSD_GROUPS, SSD_STATE)
    d_skip = lp['d_skip'].astype(jnp.float32).reshape(2, SSD_GROUPS, SSD_REP)
    y_f, h_f = ssd_scan(xs, dt[:, :, 0], A[0], B, C, d_skip[0], h0_f)
    y_b, h_b = ssd_scan(_flip(xs), _flip(dt[:, :, 1]), A[1], _flip(B), _flip(C), d_skip[1], h0_b)
    y = (y_f + _flip(y_b)).reshape(b, l, D_SSD)
    gated = (y * jax.nn.silu(z.astype(jnp.float32))).reshape(b, l, SSD_GROUPS, D_SSD // SSD_GROUPS)
    y_ssd = rms_norm(gated, lp['ssd_norm_w'].reshape(SSD_GROUPS, -1)).reshape(b, l, D_SSD).astype(h.dtype)

    y_pool = pool_mixer(v, lp['pool_w'], lp['pool_scale'], grid)

    merged = (jax.nn.sigmoid(g_ssd) * (y_ssd @ lp['w_ssd_out'])
              + jax.nn.sigmoid(g_pool) * (y_pool @ lp['w_pool_out']))
    return merged @ lp['w_o'], h_f, h_b


def conv_ffn(h, lp, grid):
    b, l, _ = h.shape
    a, g = jnp.split(h @ lp['w_up'], 2, axis=-1)
    if grid:
        g = g.reshape(b, l // GRID_W, GRID_W, D_FF)
    g = dwconv_axis1(g, lp['ffn_conv_w'], lp['ffn_conv_b']).reshape(b, l, D_FF)
    return (jax.nn.gelu(g, approximate=False) * a) @ lp['w_down']


def setup_inputs(seed: int = 0) -> dict:
    key = jax.random.key(seed)
    ks = jax.random.split(key, 28)
    f32 = jnp.float32

    def nrm(k, shape, scale):
        return jax.random.normal(k, shape, f32) * scale

    dt0 = jnp.exp(jax.random.uniform(ks[10], (DEPTH, 2, SSD_HEADS), f32,
                                     minval=math.log(1e-3), maxval=math.log(1e-1)))
    return {
        "x": nrm(ks[0], (BATCH, SEQ, D_MODEL), 1.0),
        "c": nrm(ks[1], (BATCH, D_MODEL), 1.0),
        "ctx": nrm(ks[2], (BATCH, CTX_LEN, D_MODEL), 1.0),
        "c_ctx": nrm(ks[3], (D_MODEL,), 1.0),
        "w_ada": nrm(ks[4], (DEPTH, D_MODEL, 6 * D_MODEL), 0.5 * D_MODEL ** -0.5),
        "b_ada": nrm(ks[5], (DEPTH, 6 * D_MODEL), 0.01),
        "norm1_w": 1.0 + nrm(ks[6], (DEPTH, D_MODEL), 0.02),
        "w_in": nrm(ks[7], (DEPTH, D_MODEL, IN_COLS), D_MODEL ** -0.5),
        "ssd_conv_w": nrm(ks[8], (DEPTH, SSD_CONV, N_XBC), SSD_CONV ** -0.5),
        "ssd_conv_b": nrm(ks[9], (DEPTH, N_XBC), 0.01),
        "dt_bias": dt0 + jnp.log(-jnp.expm1(-dt0)),
        "a_log": jnp.log(jax.random.uniform(ks[11], (DEPTH, 2, SSD_HEADS), f32, minval=1.0, maxval=16.0)),
        "d_skip": 1.0 + nrm(ks[12], (DEPTH, 2, SSD_HEADS), 0.1),
        "ssd_norm_w": 1.0 + nrm(ks[13], (DEPTH, D_SSD), 0.02),
        "w_ssd_out": nrm(ks[14], (DEPTH, D_SSD, D_MODEL), D_SSD ** -0.5),
        "pool_w": nrm(ks[15], (DEPTH, POOL_GROUPS, POOL_GW, POOL_GW), POOL_GW ** -0.5),
        "pool_scale": 1.0 + nrm(ks[16], (DEPTH, D_POOL), 0.02),
        "w_pool_out": nrm(ks[17], (DEPTH, D_POOL, D_MODEL), D_POOL ** -0.5),
        "w_o": nrm(ks[18], (DEPTH, D_MODEL, D_MODEL), D_MODEL ** -0.5),
        "norm2_w": 1.0 + nrm(ks[19], (DEPTH, D_MODEL), 0.02),
        "w_up": nrm(ks[20], (DEPTH, D_MODEL, 2 * D_FF), D_MODEL ** -0.5),
        "ffn_conv_w": nrm(ks[21], (DEPTH, FFN_CONV, D_FF), FFN_CONV ** -0.5),
        "ffn_conv_b": nrm(ks[22], (DEPTH, D_FF), 0.01),
        "w_down": nrm(ks[23], (DEPTH, D_FF, D_MODEL), D_FF ** -0.5),
        "final_norm_w": 1.0 + nrm(ks[24], (D_MODEL,), 0.02),
    }


def reference(x, c, ctx, c_ctx, w_ada, b_ada, norm1_w, w_in, ssd_conv_w, ssd_conv_b, dt_bias, a_log,
              d_skip, ssd_norm_w, w_ssd_out, pool_w, pool_scale, w_pool_out, w_o, norm2_w, w_up,
              ffn_conv_w, ffn_conv_b, w_down, final_norm_w):
    x_lat = x
    x_ctx = ctx
    for i in range(DEPTH):
        lp = {
            'w_in': w_in[i], 'ssd_conv_w': ssd_conv_w[i], 'ssd_conv_b': ssd_conv_b[i],
            'dt_bias': dt_bias[i], 'a_log': a_log[i], 'd_skip': d_skip[i], 'ssd_norm_w': ssd_norm_w[i],
            'w_ssd_out': w_ssd_out[i], 'pool_w': pool_w[i], 'pool_scale': pool_scale[i],
            'w_pool_out': w_pool_out[i], 'w_o': w_o[i], 'w_up': w_up[i], 'ffn_conv_w': ffn_conv_w[i],
            'ffn_conv_b': ffn_conv_b[i], 'w_down': w_down[i],
        }
        mod_lat = (jax.nn.silu(c) @ w_ada[i] + b_ada[i])[:, None, :]
        mod_ctx = (jax.nn.silu(c_ctx) @ w_ada[i] + b_ada[i])[None, None, :]
        sh1, sc1, g1, sh2, sc2, g2 = jnp.split(mod_lat, 6, axis=-1)
        csh1, csc1, cg1, csh2, csc2, cg2 = jnp.split(mod_ctx, 6, axis=-1)

        h_ctx = modulate(x_ctx, norm1_w[i], csh1, csc1)
        if i == DEPTH - 1:
            h0_f, h0_b = context_ssd_states(h_ctx, lp)
        else:
            zero = _zero_state(x_ctx.shape[0])
            out_ctx, h0_f, h0_b = token_mixer(h_ctx, lp, zero, zero, grid=False)
            x_ctx = x_ctx + cg1 * out_ctx
            x_ctx = x_ctx + cg2 * conv_ffn(modulate(x_ctx, norm2_w[i], csh2, csc2), lp, grid=False)

        h_lat = modulate(x_lat, norm1_w[i], sh1, sc1)
        out_lat, _, _ = token_mixer(h_lat, lp, h0_f, h0_b, grid=True)
        x_lat = x_lat + g1 * out_lat
        x_lat = x_lat + g2 * conv_ffn(modulate(x_lat, norm2_w[i], sh2, sc2), lp, grid=True)
    return rms_norm(x_lat, final_norm_w)
```

```python
import functools

import numpy as np
import jax
import jax.numpy as jnp
from jax import lax
from jax.experimental import pallas as pl
from jax.experimental.pallas import tpu as pltpu

F32 = jnp.float32
BF16 = jnp.bfloat16

D_MODEL = 2048
BATCH = 4
SEQ = 4096
TOKENS = BATCH * SEQ
GRID_W = 64
CTX_LEN = 256

SSD_HEAD_DIM = 64
SSD_HEADS = 32
SSD_GROUPS = 8
SSD_REP = 4
SSD_STATE = 128
CHUNK = 128
GROUP_W = SSD_REP * SSD_HEAD_DIM
N_CHUNKS = SEQ // CHUNK
N_CTX_CHUNKS = CTX_LEN // CHUNK

POOL_WINDOWS = (2, 4, 8, 16)
POOL_GW = 512
D_FF = 5632
EPS = 1e-6

COL_X = 0
COL_B = 2048
COL_C = 3072
COL_Z = 4096
COL_V = 6144
COL_GS = 8192
COL_GP = 10240
N_PROJ = 12288
N_CTX_PROJ = 3072

LANES = 128
VMEM_LIMIT = 56 * 1024 * 1024


def _dot(a, b):
    return jnp.dot(a, b, preferred_element_type=F32)


def _silu(x):
    return x * jax.nn.sigmoid(x)


def _softplus(x):
    return jnp.maximum(x, 0.0) + jnp.log1p(jnp.exp(-jnp.abs(x)))


def _params(sem):
    return pltpu.CompilerParams(dimension_semantics=sem, vmem_limit_bytes=VMEM_LIMIT)


def _ada_kernel(c_ref, w_ref, b_ref, o_ref):
    s = _silu(c_ref[...])
    o_ref[...] = jnp.dot(s, w_ref[...], precision=lax.Precision.HIGHEST,
                         preferred_element_type=F32) + b_ref[...]


def _ada(cs, w, b):
    tn = 1024
    n = w.shape[1]
    return pl.pallas_call(
        _ada_kernel,
        grid=(n // tn,),
        in_specs=[pl.BlockSpec((8, D_MODEL), lambda j: (0, 0)),
                  pl.BlockSpec((D_MODEL, tn), lambda j: (0, j)),
                  pl.BlockSpec((1, tn), lambda j: (0, j))],
        out_specs=pl.BlockSpec((8, tn), lambda j: (0, j)),
        out_shape=jax.ShapeDtypeStruct((8, n), F32),
        compiler_params=_params(("arbitrary",)),
        name="ada",
    )(cs, w, b)


def _modulate_rows(x_ref, nw, sc, sh, out_ref, rows, rc=256):
    def body(k, carry):
        r = pl.multiple_of(k * rc, rc)
        x = x_ref[pl.ds(r, rc), :]
        ms = jnp.mean(x * x, axis=-1, keepdims=True)
        y = x * lax.rsqrt(ms + EPS) * nw
        out_ref[pl.ds(r, rc), :] = (y * (1.0 + sc) + sh).astype(out_ref.dtype)
        return carry
    lax.fori_loop(0, rows // rc, body, 0)


def _inproj_kernel(x_ref, sh_ref, sc_ref, nw_ref, w_ref, wdt_ref, o_ref, dt_ref, h_scr):
    @pl.when(pl.program_id(1) == 0)
    def _():
        _modulate_rows(x_ref, nw_ref[...], sc_ref[0], sh_ref[0], h_scr, h_scr.shape[0])
        dt_ref[...] = _dot(h_scr[...], wdt_ref[...])

    o_ref[...] = _dot(h_scr[...], w_ref[...]).astype(o_ref.dtype)


def _inproj(x2d, sh, sc, nw, w_cat, w_dt, *, n_cols, rows_per_mod, tm=1024, tn=1024):
    m = x2d.shape[0]
    tpb = rows_per_mod // tm
    return pl.pallas_call(
        _inproj_kernel,
        grid=(m // tm, n_cols // tn),
        in_specs=[pl.BlockSpec((tm, D_MODEL), lambda i, j: (i, 0)),
                  pl.BlockSpec((1, 1, D_MODEL), lambda i, j: (i // tpb, 0, 0)),
                  pl.BlockSpec((1, 1, D_MODEL), lambda i, j: (i // tpb, 0, 0)),
                  pl.BlockSpec((1, D_MODEL), lambda i, j: (0, 0)),
                  pl.BlockSpec((D_MODEL, tn), lambda i, j: (0, j)),
                  pl.BlockSpec((D_MODEL, LANES), lambda i, j: (0, 0))],
        out_specs=[pl.BlockSpec((tm, tn), lambda i, j: (i, j)),
                   pl.BlockSpec((tm, LANES), lambda i, j: (i, 0))],
        out_shape=[jax.ShapeDtypeStruct((m, n_cols), BF16),
                   jax.ShapeDtypeStruct((m, LANES), F32)],
        scratch_shapes=[pltpu.VMEM((tm, D_MODEL), BF16)],
        compiler_params=_params(("arbitrary", "arbitrary")),
        name="inproj",
    )(x2d, sh, sc, nw, w_cat, w_dt)


def _conv_silu_tile(ref, c, n_chunks, w, bias):
    length = n_chunks * CHUNK
    if isinstance(c, int):
        r0, p0, n0 = c * CHUNK, max(c * CHUNK - 16, 0), min(c * CHUNK + CHUNK, length - 16)
    else:
        r0 = pl.multiple_of(c * CHUNK, CHUNK)
        p0 = pl.multiple_of(jnp.maximum(c * CHUNK - 16, 0), 16)
        n0 = pl.multiple_of(jnp.minimum(c * CHUNK + CHUNK, length - 16), 16)
    x = ref[0, pl.ds(r0, CHUNK), :].astype(F32)
    prev_row = ref[0, pl.ds(p0, 16), :].astype(F32)[15:16, :]
    prev_row = jnp.where(c > 0, prev_row, 0.0)
    next_row = ref[0, pl.ds(n0, 16), :].astype(F32)[0:1, :]
    next_row = jnp.where(c < n_chunks - 1, next_row, 0.0)
    rows = lax.broadcasted_iota(jnp.int32, x.shape, 0)
    xm = jnp.where(rows == 0, prev_row, pltpu.roll(x, 1, axis=0))
    xp = jnp.where(rows == CHUNK - 1, next_row, pltpu.roll(x, CHUNK - 1, axis=0))
    y = xm * w[0:1, :] + x * w[1:2, :] + xp * w[2:3, :] + bias
    return _silu(y)


def _dir_quantities(dt_raw, dt_bias, a_neg, tri, causal):
    dt = _softplus(dt_raw + dt_bias)
    a = dt * a_neg
    hi = a.astype(BF16)
    lo = (a - hi.astype(F32)).astype(BF16)
    r = _dot(jnp.concatenate([hi, lo], axis=0), tri)
    acs = r[0:8] + r[8:16]
    tot = acs[:, CHUNK - 1:CHUNK] if causal else acs[:, 0:1]
    e = jnp.exp(acs)
    w = jnp.exp(tot - acs) * dt
    dec = jnp.broadcast_to(jnp.exp(tot), acs.shape)
    rows = jnp.concatenate([acs, e, w, dec, jnp.zeros((CHUNK - 32, CHUNK), F32)], axis=0)
    return acs, dt, rows.T


def _expand_heads(rt, base):
    lane = lax.broadcasted_iota(jnp.int32, (CHUNK, LANES), 1)
    t0 = jnp.where(lane < SSD_HEAD_DIM, rt[:, base:base + 1], rt[:, base + 1:base + 2])
    t1 = jnp.where(lane < SSD_HEAD_DIM, rt[:, base + 2:base + 3], rt[:, base + 3:base + 4])
    return jnp.concatenate([t0, t1], axis=1)


def _state_update(xs, bt, rt, off, h_ref):
    h = h_ref[...]
    xw = (xs * _expand_heads(rt, 16 + off)).astype(BF16)
    h_ref[...] = h * _expand_heads(rt, 24 + off) + _dot(bt, xw)
    return h


def _chunk_dir(xs, bt, cc, cb, acs, dt, rt, off, h_ref, causal):
    l_io = lax.broadcasted_iota(jnp.int32, (CHUNK, CHUNK), 0)
    s_io = lax.broadcasted_iota(jnp.int32, (CHUNK, CHUNK), 1)
    mask = (l_io >= s_io) if causal else (l_io <= s_io)
    lane = lax.broadcasted_iota(jnp.int32, (CHUNK, LANES), 1)
    ys = []
    for pair in range(2):
        ms = []
        for r in (2 * pair, 2 * pair + 1):
            k = off + r
            seg = rt[:, k:k + 1] - acs[k:k + 1, :]
            decay = jnp.exp(jnp.where(mask, seg, -jnp.inf))
            ms.append((cb * decay * dt[k:k + 1, :]).astype(BF16))
        xt = xs[:, pair * LANES:(pair + 1) * LANES]
        top = jnp.where(lane < SSD_HEAD_DIM, xt, 0.0).astype(BF16)
        bot = jnp.where(lane >= SSD_HEAD_DIM, xt, 0.0).astype(BF16)
        ys.append(_dot(jnp.concatenate(ms, axis=1), jnp.concatenate([top, bot], axis=0)))
    y_diag = jnp.concatenate(ys, axis=1)
    h = _state_update(xs, bt, rt, off, h_ref)
    y_off = _dot(cc, h.astype(BF16)) * _expand_heads(rt, 8 + off)
    return y_diag + y_off


def _ssd_kernel(xs_ref, b_ref, c_ref, z_ref, dtr_ref, cxs_ref, cbm_ref, cdtr_ref,
                cwx_ref, cwb_ref, cwc_ref, bx_ref, bb_ref, bc_ref,
                dtb_ref, alog_ref, dsk_ref, nw_ref, tri_ref,
                y_ref,
                xs_scr, bt_scr, cc_scr, cb_scr, yf_scr, hf_scr, hb_scr):
    dt_bias = dtb_ref[0]
    a_neg = -jnp.exp(alog_ref[0])
    cwx, cwb, cwc = cwx_ref[...], cwb_ref[...], cwc_ref[...]
    bx, bb, bc = bx_ref[...], bb_ref[...], bc_ref[...]

    hf_scr[...] = jnp.zeros(hf_scr.shape, F32)
    hb_scr[...] = jnp.zeros(hb_scr.shape, F32)
    ctx_tiles = []
    for c in range(N_CTX_CHUNKS):
        xs = _conv_silu_tile(cxs_ref, c, N_CTX_CHUNKS, cwx, bx)
        bt = _conv_silu_tile(cbm_ref, c, N_CTX_CHUNKS, cwb, bb).T.astype(BF16)
        ctx_tiles.append((xs, bt))
    for c in range(N_CTX_CHUNKS):
        xs, bt = ctx_tiles[c]
        _, _, rt = _dir_quantities(cdtr_ref[0, 0, c], dt_bias, a_neg, tri_ref[0], True)
        _state_update(xs, bt, rt, 0, hf_scr)
    for c in reversed(range(N_CTX_CHUNKS)):
        xs, bt = ctx_tiles[c]
        _, _, rt = _dir_quantities(cdtr_ref[0, 0, c], dt_bias, a_neg, tri_ref[1], False)
        _state_update(xs, bt, rt, 4, hb_scr)

    def fwd_body(c, carry):
        r0 = pl.multiple_of(c * CHUNK, CHUNK)
        xs = _conv_silu_tile(xs_ref, c, N_CHUNKS, cwx, bx)
        bt = _conv_silu_tile(b_ref, c, N_CHUNKS, cwb, bb).T.astype(BF16)
        cc = _conv_silu_tile(c_ref, c, N_CHUNKS, cwc, bc).astype(BF16)
        cb = _dot(cc, bt)
        xs_scr[pl.ds(r0, CHUNK), :] = xs
        bt_scr[c] = bt
        cc_scr[pl.ds(r0, CHUNK), :] = cc
        cb_scr[c] = cb
        acs, dt, rt = _dir_quantities(dtr_ref[0, 0, c], dt_bias, a_neg, tri_ref[0], True)
        yf_scr[pl.ds(r0, CHUNK), :] = _chunk_dir(xs, bt, cc, cb, acs, dt, rt, 0, hf_scr, True)
        return carry

    lax.fori_loop(0, N_CHUNKS, fwd_body, 0)

    d_sum = dsk_ref[0, 0:1, :] + dsk_ref[0, 1:2, :]
    nw = nw_ref[...]

    def bwd_body(i, carry):
        c = N_CHUNKS - 1 - i
        r0 = pl.multiple_of(c * CHUNK, CHUNK)
        xs = xs_scr[pl.ds(r0, CHUNK), :]
        acs, dt, rt = _dir_quantities(dtr_ref[0, 0, c], dt_bias, a_neg, tri_ref[1], False)
        yb = _chunk_dir(xs, bt_scr[c], cc_scr[pl.ds(r0, CHUNK), :], cb_scr[c],
                        acs, dt, rt, 4, hb_scr, False)
        y = yf_scr[pl.ds(r0, CHUNK), :] + yb + xs * d_sum
        gated = y * _silu(z_ref[0, pl.ds(r0, CHUNK), :].astype(F32))
        ms = jnp.mean(gated * gated, axis=-1, keepdims=True)
        y_ref[0, pl.ds(r0, CHUNK), :] = (gated * lax.rsqrt(ms + EPS) * nw).astype(y_ref.dtype)
        return carry

    lax.fori_loop(0, N_CHUNKS, bwd_body, 0)


def _ssd(proj, dtr, cproj, cdtr, conv_w, conv_b, dtb, alog, dsk, norm_w, tri):
    gx = COL_X // GROUP_W
    gz = COL_Z // GROUP_W
    gb = COL_B // SSD_STATE
    gc = COL_C // SSD_STATE
    in_specs = [
        pl.BlockSpec((1, SEQ, GROUP_W), lambda b, g: (b, 0, gx + g)),
        pl.BlockSpec((1, SEQ, SSD_STATE), lambda b, g: (b, 0, gb + g)),
        pl.BlockSpec((1, SEQ, SSD_STATE), lambda b, g: (b, 0, gc + g)),
        pl.BlockSpec((1, SEQ, GROUP_W), lambda b, g: (b, 0, gz + g)),
        pl.BlockSpec((1, 1, N_CHUNKS, 8, CHUNK), lambda b, g: (g, b, 0, 0, 0)),
        pl.BlockSpec((1, CTX_LEN, GROUP_W), lambda b, g: (b, 0, gx + g)),
        pl.BlockSpec((1, CTX_LEN, SSD_STATE), lambda b, g: (b, 0, gb + g)),
        pl.BlockSpec((1, 1, N_CTX_CHUNKS, 8, CHUNK), lambda b, g: (g, b, 0, 0, 0)),
        pl.BlockSpec((3, GROUP_W), lambda b, g: (0, gx + g)),
        pl.BlockSpec((3, SSD_STATE), lambda b, g: (0, gb + g)),
        pl.BlockSpec((3, SSD_STATE), lambda b, g: (0, gc + g)),
        pl.BlockSpec((1, GROUP_W), lambda b, g: (0, gx + g)),
        pl.BlockSpec((1, SSD_STATE), lambda b, g: (0, gb + g)),
        pl.BlockSpec((1, SSD_STATE), lambda b, g: (0, gc + g)),
        pl.BlockSpec((1, 8, 1), lambda b, g: (g, 0, 0)),
        pl.BlockSpec((1, 8, 1), lambda b, g: (g, 0, 0)),
        pl.BlockSpec((1, 2, GROUP_W), lambda b, g: (g, 0, 0)),
        pl.BlockSpec((1, GROUP_W), lambda b, g: (0, g)),
        pl.BlockSpec((2, CHUNK, CHUNK), lambda b, g: (0, 0, 0)),
    ]
    return pl.pallas_call(
        _ssd_kernel,
        grid=(BATCH, SSD_GROUPS),
        in_specs=in_specs,
        out_specs=pl.BlockSpec((1, SEQ, GROUP_W), lambda b, g: (b, 0, g)),
        out_shape=jax.ShapeDtypeStruct((BATCH, SEQ, D_MODEL), BF16),
        scratch_shapes=[pltpu.VMEM((SEQ, GROUP_W), F32),
                        pltpu.VMEM((N_CHUNKS, CHUNK, CHUNK), BF16),
                        pltpu.VMEM((SEQ, SSD_STATE), BF16),
                        pltpu.VMEM((N_CHUNKS, CHUNK, CHUNK), F32),
                        pltpu.VMEM((SEQ, GROUP_W), F32),
                        pltpu.VMEM((SSD_STATE, GROUP_W), F32),
                        pltpu.VMEM((SSD_STATE, GROUP_W), F32)],
        compiler_params=_params(("arbitrary", "arbitrary")),
        name="ssd",
    )(proj, proj, proj, proj, dtr, cproj, cproj, cdtr,
      conv_w, conv_w, conv_w, conv_b, conv_b, conv_b, dtb, alog, dsk, norm_w, tri)


def _pool_kernel(v_ref, win_ref, inv_ref, pw_ref, ps_ref, o_ref):
    v = v_ref[...]
    sums = _dot(win_ref[0], v)
    p = sums * inv_ref[0] - v.astype(F32)
    o_ref[...] = (_dot(p.astype(BF16), pw_ref[0]) * ps_ref[...]).astype(o_ref.dtype)


def _pool_constants(tm):
    t = np.arange(tm)
    col = t % GRID_W
    wins, invs = [], []
    for w in POOL_WINDOWS:
        start = np.clip(col - w // 2, 0, GRID_W)
        end = np.clip(col + w - w // 2, 0, GRID_W)
        same_row = (t[:, None] // GRID_W) == (t[None, :] // GRID_W)
        inside = (col[None, :] >= start[:, None]) & (col[None, :] < end[:, None])
        wins.append((same_row & inside).astype(np.float32))
        invs.append((1.0 / (end - start).astype(np.float32))[:, None])
    return jnp.asarray(np.stack(wins), BF16), jnp.asarray(np.stack(invs), F32)


def _pool(proj2d, pool_w, pool_scale, tm=512):
    win, inv = _pool_constants(tm)
    gv = COL_V // POOL_GW
    return pl.pallas_call(
        _pool_kernel,
        grid=(TOKENS // tm, len(POOL_WINDOWS)),
        in_specs=[pl.BlockSpec((tm, POOL_GW), lambda i, k: (i, gv + k)),
                  pl.BlockSpec((1, tm, tm), lambda i, k: (k, 0, 0)),
                  pl.BlockSpec((1, tm, 1), lambda i, k: (k, 0, 0)),
                  pl.BlockSpec((1, POOL_GW, POOL_GW), lambda i, k: (k, 0, 0)),
                  pl.BlockSpec((1, POOL_GW), lambda i, k: (0, k))],
        out_specs=pl.BlockSpec((tm, POOL_GW), lambda i, k: (i, k)),
        out_shape=jax.ShapeDtypeStruct((TOKENS, D_MODEL), BF16),
        compiler_params=_params(("arbitrary", "arbitrary")),
        name="pool",
    )(proj2d, win, inv, pool_w, pool_scale)


def _merge_kernel(ys_ref, yp_ref, ws_ref, wp_ref, gs_ref, gp_ref, o_ref):
    a = jax.nn.sigmoid(gs_ref[...].astype(F32)) * _dot(ys_ref[...], ws_ref[...])
    b = jax.nn.sigmoid(gp_ref[...].astype(F32)) * _dot(yp_ref[...], wp_ref[...])
    o_ref[...] = (a + b).astype(o_ref.dtype)


def _merge(y_ssd, y_pool, w_s, w_p, proj2d, tm=1024, tn=512):
    ggs = COL_GS // tn
    ggp = COL_GP // tn
    return pl.pallas_call(
        _merge_kernel,
        grid=(TOKENS // tm, D_MODEL // tn),
        in_specs=[pl.BlockSpec((tm, D_MODEL), lambda i, j: (i, 0)),
                  pl.BlockSpec((tm, D_MODEL), lambda i, j: (i, 0)),
                  pl.BlockSpec((D_MODEL, tn), lambda i, j: (0, j)),
                  pl.BlockSpec((D_MODEL, tn), lambda i, j: (0, j)),
                  pl.BlockSpec((tm, tn), lambda i, j: (i, ggs + j)),
                  pl.BlockSpec((tm, tn), lambda i, j: (i, ggp + j))],
        out_specs=pl.BlockSpec((tm, tn), lambda i, j: (i, j)),
        out_shape=jax.ShapeDtypeStruct((TOKENS, D_MODEL), BF16),
        compiler_params=_params(("arbitrary", "arbitrary")),
        name="merge",
    )(y_ssd, y_pool, w_s, w_p, proj2d, proj2d)


def _outproj_kernel(m_ref, wo_ref, x_ref, g1_ref, sh_ref, sc_ref, nw_ref, x1_ref, h2_ref):
    x1_ref[...] = x_ref[...] + g1_ref[0] * _dot(m_ref[...], wo_ref[...])
    _modulate_rows(x1_ref, nw_ref[...], sc_ref[0], sh_ref[0], h2_ref, x1_ref.shape[0], rc=128)


def _outproj(merged, w_o, x2d, g1, sh2, sc2, nw2, tm=256):
    tpb = SEQ // tm
    mod_spec = pl.BlockSpec((1, 1, D_MODEL), lambda i: (i // tpb, 0, 0))
    return pl.pallas_call(
        _outproj_kernel,
        grid=(TOKENS // tm,),
        in_specs=[pl.BlockSpec((tm, D_MODEL), lambda i: (i, 0)),
                  pl.BlockSpec((D_MODEL, D_MODEL), lambda i: (0, 0)),
                  pl.BlockSpec((tm, D_MODEL), lambda i: (i, 0)),
                  mod_spec, mod_spec, mod_spec,
                  pl.BlockSpec((1, D_MODEL), lambda i: (0, 0))],
        out_specs=[pl.BlockSpec((tm, D_MODEL), lambda i: (i, 0)),
                   pl.BlockSpec((tm, D_MODEL), lambda i: (i, 0))],
        out_shape=[jax.ShapeDtypeStruct((TOKENS, D_MODEL), F32),
                   jax.ShapeDtypeStruct((TOKENS, D_MODEL), BF16)],
        compiler_params=_params(("arbitrary",)),
        name="outproj",
    )(merged, w_o, x2d, g1, sh2, sc2, nw2)


def _up_kernel(h_ref, w_ref, o_ref):
    o_ref[...] = _dot(h_ref[...], w_ref[...]).astype(o_ref.dtype)


def _up(h2, w_up, tm=1024, tn=1024):
    n = w_up.shape[1]
    return pl.pallas_call(
        _up_kernel,
        grid=(TOKENS // tm, n // tn),
        in_specs=[pl.BlockSpec((tm, D_MODEL), lambda i, j: (i, 0)),
                  pl.BlockSpec((D_MODEL, tn), lambda i, j: (0, j))],
        out_specs=pl.BlockSpec((tm, tn), lambda i, j: (i, j)),
        out_shape=jax.ShapeDtypeStruct((TOKENS, n), BF16),
        compiler_params=_params(("arbitrary", "arbitrary")),
        name="up",
    )(h2, w_up)


def _down_kernel(a_ref, g_ref, gp_ref, gn_ref, fw_ref, fb_ref, wd_ref, x1_ref, g2_ref, nw_ref,
                 o_ref, acc_ref):
    i = pl.program_id(0)
    kk = pl.program_id(1)
    tm = a_ref.shape[0]

    @pl.when(kk == 0)
    def _():
        acc_ref[...] = jnp.zeros(acc_ref.shape, F32)

    has_prev = ((i * tm) % SEQ) != 0
    has_next = (((i + 1) * tm) % SEQ) != 0
    prev = jnp.where(has_prev, gp_ref[...].astype(F32), 0.0)
    nxt = jnp.where(has_next, gn_ref[...].astype(F32), 0.0)
    ext = jnp.concatenate([prev, g_ref[...].astype(F32), nxt], axis=0)
    fw = fw_ref[...]
    cv = (ext[0:tm] * fw[0:1, :] + fb_ref[...] + ext[GRID_W:GRID_W + tm] * fw[1:2, :]
          + ext[2 * GRID_W:2 * GRID_W + tm] * fw[2:3, :])
    gelu = 0.5 * cv * (1.0 + lax.erf(cv * np.float32(np.sqrt(0.5))))
    u = gelu * a_ref[...].astype(F32)
    acc_ref[...] += _dot(u.astype(BF16), wd_ref[...])

    @pl.when(kk == pl.num_programs(1) - 1)
    def _():
        x2 = x1_ref[...] + g2_ref[0] * acc_ref[...]
        ms = jnp.mean(x2 * x2, axis=-1, keepdims=True)
        o_ref[...] = x2 * lax.rsqrt(ms + EPS) * nw_ref[...]


def _down(up, ffn_conv_w, ffn_conv_b, w_down, x1, g2, fnw, tm=512, tk=512):
    nk = D_FF // tk
    tpb = SEQ // tm
    hb = tm // GRID_W
    n_halo = TOKENS // GRID_W
    return pl.pallas_call(
        _down_kernel,
        grid=(TOKENS // tm, nk),
        in_specs=[pl.BlockSpec((tm, tk), lambda i, k: (i, k)),
                  pl.BlockSpec((tm, tk), lambda i, k: (i, nk + k)),
                  pl.BlockSpec((GRID_W, tk), lambda i, k: (jnp.maximum(i * hb - 1, 0), nk + k)),
                  pl.BlockSpec((GRID_W, tk),
                               lambda i, k: (jnp.minimum((i + 1) * hb, n_halo - 1), nk + k)),
                  pl.BlockSpec((3, tk), lambda i, k: (0, k)),
                  pl.BlockSpec((1, tk), lambda i, k: (0, k)),
                  pl.BlockSpec((tk, D_MODEL), lambda i, k: (k, 0)),
                  pl.BlockSpec((tm, D_MODEL), lambda i, k: (i, 0)),
                  pl.BlockSpec((1, 1, D_MODEL), lambda i, k: (i // tpb, 0, 0)),
                  pl.BlockSpec((1, D_MODEL), lambda i, k: (0, 0))],
        out_specs=pl.BlockSpec((tm, D_MODEL), lambda i, k: (i, 0)),
        out_shape=jax.ShapeDtypeStruct((TOKENS, D_MODEL), F32),
        scratch_shapes=[pltpu.VMEM((tm, D_MODEL), F32)],
        compiler_params=_params(("arbitrary", "arbitrary")),
        name="down",
    )(up, up, up, up, ffn_conv_w, ffn_conv_b, w_down, x1, g2, fnw)


def _dt_rows(dt_raw, batch, n_chunks):
    d = dt_raw[:, :2 * SSD_HEADS].reshape(batch, n_chunks, CHUNK, 2, SSD_GROUPS, SSD_REP)
    d = jnp.transpose(d, (4, 0, 1, 3, 5, 2))
    return d.reshape(SSD_GROUPS, batch, n_chunks, 2 * SSD_REP, CHUNK)


def _group_rows(p):
    p = p.reshape(2, SSD_GROUPS, SSD_REP)
    return jnp.transpose(p, (1, 0, 2)).reshape(SSD_GROUPS, 2 * SSD_REP, 1)


def kernel(x, c, ctx, c_ctx, w_ada, b_ada, norm1_w, w_in, ssd_conv_w, ssd_conv_b, dt_bias, a_log,
           d_skip, ssd_norm_w, w_ssd_out, pool_w, pool_scale, w_pool_out, w_o, norm2_w, w_up,
           ffn_conv_w, ffn_conv_b, w_down, final_norm_w):
    assert w_in.shape[0] == 1, "single-layer block only"
    d = D_MODEL
    x2d = x.reshape(TOKENS, d)
    ctx2d = ctx.reshape(BATCH * CTX_LEN, d)

    cs = jnp.zeros((8, d), F32).at[:BATCH].set(c).at[BATCH].set(c_ctx)
    mod = _ada(cs, w_ada[0], b_ada)
    sh1, sc1, g1, sh2, sc2, g2 = [mod[:BATCH, k * d:(k + 1) * d].reshape(BATCH, 1, d)
                                  for k in range(6)]
    csh1 = mod[BATCH:BATCH + 1, 0:d].reshape(1, 1, d)
    csc1 = mod[BATCH:BATCH + 1, d:2 * d].reshape(1, 1, d)

    w_in0 = w_in[0]
    n_dt = 2 * SSD_HEADS
    w_dt = jnp.pad(w_in0[:, :n_dt], ((0, 0), (0, LANES - n_dt))).astype(BF16)
    w_cat = w_in0[:, n_dt:].astype(BF16)
    w_s = w_ssd_out[0].astype(BF16)
    w_p = w_pool_out[0].astype(BF16)
    w_ob = w_o[0].astype(BF16)
    w_upb = w_up[0].astype(BF16)
    w_db = w_down[0].astype(BF16)
    pool_wb = pool_w[0].astype(BF16)

    proj, dt_raw = _inproj(x2d, sh1, sc1, norm1_w, w_cat, w_dt, n_cols=N_PROJ, rows_per_mod=SEQ)
    cproj, cdt_raw = _inproj(ctx2d, csh1, csc1, norm1_w, w_cat, w_dt, n_cols=N_CTX_PROJ,
                             rows_per_mod=BATCH * CTX_LEN)

    dtr = _dt_rows(dt_raw, BATCH, N_CHUNKS)
    cdtr = _dt_rows(cdt_raw, BATCH, N_CTX_CHUNKS)
    dsk = jnp.repeat(jnp.transpose(d_skip[0].reshape(2, SSD_GROUPS, SSD_REP), (1, 0, 2)),
                     SSD_HEAD_DIM, axis=-1)
    idx = np.arange(CHUNK)
    tri = jnp.asarray(np.stack([idx[:, None] <= idx[None, :], idx[:, None] >= idx[None, :]])
                      .astype(np.float32), BF16)

    y_ssd = _ssd(proj.reshape(BATCH, SEQ, N_PROJ), dtr,
                 cproj.reshape(BATCH, CTX_LEN, N_CTX_PROJ), cdtr,
                 ssd_conv_w[0], ssd_conv_b, _group_rows(dt_bias[0]), _group_rows(a_log[0]),
                 dsk, ssd_norm_w, tri)
    y_pool = _pool(proj, pool_wb, pool_scale)
    merged = _merge(y_ssd.reshape(TOKENS, d), y_pool, w_s, w_p, proj)
    x1, h2 = _outproj(merged, w_ob, x2d, g1, sh2, sc2, norm2_w)
    up = _up(h2, w_upb)
    out = _down(up, ffn_conv_w[0], ffn_conv_b, w_db, x1, g2, final_norm_w.reshape(1, d))
    return out.reshape(BATCH, SEQ, d)
```

```python
import functools

import numpy as np
import jax
import jax.numpy as jnp
from jax import lax
from jax.experimental import pallas as pl
from jax.experimental.pallas import tpu as pltpu

F32 = jnp.float32
BF16 = jnp.bfloat16

D_MODEL = 2048
BATCH = 4
SEQ = 4096
TOKENS = BATCH * SEQ
GRID_W = 64
CTX_LEN = 256

SSD_HEAD_DIM = 64
SSD_HEADS = 32
SSD_GROUPS = 8
SSD_REP = 4
SSD_STATE = 128
CHUNK = 128
GROUP_W = SSD_REP * SSD_HEAD_DIM
N_CHUNKS = SEQ // CHUNK
N_CTX_CHUNKS = CTX_LEN // CHUNK

POOL_WINDOWS = (2, 4, 8, 16)
POOL_GW = 512
D_FF = 5632
EPS = 1e-6

COL_X = 0
COL_B = 2048
COL_C = 3072
COL_Z = 4096
COL_V = 6144
COL_GS = 8192
COL_GP = 10240
N_PROJ = 12288
N_CTX_PROJ = 3072

LANES = 128
VMEM_LIMIT = 56 * 1024 * 1024


def _dot(a, b):
    return jnp.dot(a, b, preferred_element_type=F32)


def _silu(x):
    return x * jax.nn.sigmoid(x)


def _softplus(x):
    return jnp.maximum(x, 0.0) + jnp.log1p(jnp.exp(-jnp.abs(x)))


def _params(sem):
    return pltpu.CompilerParams(dimension_semantics=sem, vmem_limit_bytes=VMEM_LIMIT)


def _ada_kernel(c_ref, w_ref, b_ref, o_ref):
    s = _silu(c_ref[...])
    o_ref[...] = jnp.dot(s, w_ref[...], precision=lax.Precision.HIGHEST,
                         preferred_element_type=F32) + b_ref[...]


def _ada(cs, w, b):
    tn = 1024
    n = w.shape[1]
    return pl.pallas_call(
        _ada_kernel,
        grid=(n // tn,),
        in_specs=[pl.BlockSpec((8, D_MODEL), lambda j: (0, 0)),
                  pl.BlockSpec((D_MODEL, tn), lambda j: (0, j)),
                  pl.BlockSpec((1, tn), lambda j: (0, j))],
        out_specs=pl.BlockSpec((8, tn), lambda j: (0, j)),
        out_shape=jax.ShapeDtypeStruct((8, n), F32),
        compiler_params=_params(("arbitrary",)),
        name="ada",
    )(cs, w, b)


def _modulate_rows(x_ref, nw, sc, sh, out_ref, rows, rc=256):
    def body(k, carry):
        r = pl.multiple_of(k * rc, rc)
        x = x_ref[pl.ds(r, rc), :]
        ms = jnp.mean(x * x, axis=-1, keepdims=True)
        y = x * lax.rsqrt(ms + EPS) * nw
        out_ref[pl.ds(r, rc), :] = (y * (1.0 + sc) + sh).astype(out_ref.dtype)
        return carry
    lax.fori_loop(0, rows // rc, body, 0)


def _inproj_kernel(x_ref, sh_ref, sc_ref, nw_ref, w_ref, wdt_ref, o_ref, dt_ref, h_scr):
    @pl.when(pl.program_id(1) == 0)
    def _():
        _modulate_rows(x_ref, nw_ref[...], sc_ref[0], sh_ref[0], h_scr, h_scr.shape[0])
        dt_ref[...] = _dot(h_scr[...], wdt_ref[...])

    o_ref[...] = _dot(h_scr[...], w_ref[...]).astype(o_ref.dtype)


def _inproj(x2d, sh, sc, nw, w_cat, w_dt, *, n_cols, rows_per_mod, tm=1024, tn=1024):
    m = x2d.shape[0]
    tpb = rows_per_mod // tm
    return pl.pallas_call(
        _inproj_kernel,
        grid=(m // tm, n_cols // tn),
        in_specs=[pl.BlockSpec((tm, D_MODEL), lambda i, j: (i, 0)),
                  pl.BlockSpec((1, 1, D_MODEL), lambda i, j: (i // tpb, 0, 0)),
                  pl.BlockSpec((1, 1, D_MODEL), lambda i, j: (i // tpb, 0, 0)),
                  pl.BlockSpec((1, D_MODEL), lambda i, j: (0, 0)),
                  pl.BlockSpec((D_MODEL, tn), lambda i, j: (0, j)),
                  pl.BlockSpec((D_MODEL, LANES), lambda i, j: (0, 0))],
        out_specs=[pl.BlockSpec((tm, tn), lambda i, j: (i, j)),
                   pl.BlockSpec((tm, LANES), lambda i, j: (i, 0))],
        out_shape=[jax.ShapeDtypeStruct((m, n_cols), BF16),
                   jax.ShapeDtypeStruct((m, LANES), F32)],
        scratch_shapes=[pltpu.VMEM((tm, D_MODEL), BF16)],
        compiler_params=_params(("arbitrary", "arbitrary")),
        name="inproj",
    )(x2d, sh, sc, nw, w_cat, w_dt)


def _conv_silu_tile(ref, c, n_chunks, w, bias):
    length = n_chunks * CHUNK
    if isinstance(c, int):
        r0, p0, n0 = c * CHUNK, max(c * CHUNK - 16, 0), min(c * CHUNK + CHUNK, length - 16)
    else:
        r0 = pl.multiple_of(c * CHUNK, CHUNK)
        p0 = pl.multiple_of(jnp.maximum(c * CHUNK - 16, 0), 16)
        n0 = pl.multiple_of(jnp.minimum(c * CHUNK + CHUNK, length - 16), 16)
    x = ref[0, pl.ds(r0, CHUNK), :].astype(F32)
    prev_row = ref[0, pl.ds(p0, 16), :].astype(F32)[15:16, :]
    prev_row = jnp.where(c > 0, prev_row, 0.0)
    next_row = ref[0, pl.ds(n0, 16), :].astype(F32)[0:1, :]
    next_row = jnp.where(c < n_chunks - 1, next_row, 0.0)
    rows = lax.broadcasted_iota(jnp.int32, x.shape, 0)
    xm = jnp.where(rows == 0, prev_row, pltpu.roll(x, 1, axis=0))
    xp = jnp.where(rows == CHUNK - 1, next_row, pltpu.roll(x, CHUNK - 1, axis=0))
    y = xm * w[0:1, :] + x * w[1:2, :] + xp * w[2:3, :] + bias
    return _silu(y)


def _decay_rows(dt_raw, dt_bias, a_neg, tri, causal):
    dt = _softplus(dt_raw + dt_bias)
    a = dt * a_neg
    hi = a.astype(BF16)
    lo = (a - hi.astype(F32)).astype(BF16)
    r = _dot(jnp.concatenate([hi, lo], axis=0), tri)
    acs = r[0:8] + r[8:16]
    tot = acs[:, CHUNK - 1:CHUNK] if causal else acs[:, 0:1]
    return acs, dt, jnp.exp(tot - acs) * dt, jnp.exp(tot)


def _col_form(acs):
    return jnp.concatenate([acs, jnp.zeros((CHUNK - 8, CHUNK), F32)], axis=0).T


def _head_lanes(dec, off):
    lane = lax.broadcasted_iota(jnp.int32, (1, LANES), 1)
    t0 = jnp.where(lane < SSD_HEAD_DIM, dec[off:off + 1, :], dec[off + 1:off + 2, :])
    t1 = jnp.where(lane < SSD_HEAD_DIM, dec[off + 2:off + 3, :], dec[off + 3:off + 4, :])
    return jnp.concatenate([t0, t1], axis=1)


def _split_heads(t):
    lane = lax.broadcasted_iota(jnp.int32, t.shape, 1)
    return (jnp.where(lane < SSD_HEAD_DIM, t, 0.0).astype(BF16),
            jnp.where(lane >= SSD_HEAD_DIM, t, 0.0).astype(BF16))


def _state_update(x_split, bt, w, dec, off, h_ref):
    h = h_ref[...]
    ss = []
    for pair in range(2):
        bw = [(bt * w[off + r:off + r + 1, :]).astype(BF16) for r in (2 * pair, 2 * pair + 1)]
        ss.append(_dot(jnp.concatenate(bw, axis=1), jnp.concatenate(x_split[pair], axis=0)))
    h_ref[...] = h * _head_lanes(dec, off) + jnp.concatenate(ss, axis=1)
    return h


def _chunk_dir(xs, bt, cc, cb, dt_raw, dt_bias, a_neg, tri, off, h_ref, causal):
    acs, dt, w, dec = _decay_rows(dt_raw, dt_bias, a_neg, tri, causal)
    rt = _col_form(acs)
    l_io = lax.broadcasted_iota(jnp.int32, (CHUNK, CHUNK), 0)
    s_io = lax.broadcasted_iota(jnp.int32, (CHUNK, CHUNK), 1)
    mask = (l_io >= s_io) if causal else (l_io <= s_io)
    x_split = [_split_heads(xs[:, p * LANES:(p + 1) * LANES]) for p in range(2)]
    h = _state_update(x_split, bt, w, dec, off, h_ref)
    ys = []
    for pair in range(2):
        h_top, h_bot = _split_heads(h[:, pair * LANES:(pair + 1) * LANES])
        lhs = []
        for r in (2 * pair, 2 * pair + 1):
            k = off + r
            col = jnp.broadcast_to(rt[:, k:k + 1], (CHUNK, CHUNK))
            decay = jnp.exp(jnp.where(mask, col - acs[k:k + 1, :], -jnp.inf))
            lhs.append((cb * decay * dt[k:k + 1, :]).astype(BF16))
            lhs.append((cc * jnp.exp(col)).astype(BF16))
        rhs = jnp.concatenate([x_split[pair][0], h_top, x_split[pair][1], h_bot], axis=0)
        ys.append(_dot(jnp.concatenate(lhs, axis=1), rhs))
    return jnp.concatenate(ys, axis=1)


def _ssd_kernel(xs_ref, b_ref, c_ref, z_ref, dtr_ref, cxs_ref, cbm_ref, cdtr_ref,
                cwx_ref, cwb_ref, cwc_ref, bx_ref, bb_ref, bc_ref,
                dtb_ref, alog_ref, dsk_ref, nw_ref, tri_ref,
                y_ref,
                xs_scr, bt_scr, cc_scr, cb_scr, y_scr, hf_scr, hb_scr):
    dt_bias = dtb_ref[0]
    a_neg = -jnp.exp(alog_ref[0])
    cwx, cwb, cwc = cwx_ref[...], cwb_ref[...], cwc_ref[...]
    bx, bb, bc = bx_ref[...], bb_ref[...], bc_ref[...]

    hf_scr[...] = jnp.zeros(hf_scr.shape, F32)
    hb_scr[...] = jnp.zeros(hb_scr.shape, F32)
    ctx_tiles = []
    for c in range(N_CTX_CHUNKS):
        xs = _conv_silu_tile(cxs_ref, c, N_CTX_CHUNKS, cwx, bx)
        bt = _conv_silu_tile(cbm_ref, c, N_CTX_CHUNKS, cwb, bb).T
        ctx_tiles.append(([_split_heads(xs[:, p * LANES:(p + 1) * LANES]) for p in range(2)], bt))
    for c in range(N_CTX_CHUNKS):
        _, _, w, dec = _decay_rows(cdtr_ref[0, 0, c], dt_bias, a_neg, tri_ref[0], True)
        _state_update(ctx_tiles[c][0], ctx_tiles[c][1], w, dec, 0, hf_scr)
    for c in reversed(range(N_CTX_CHUNKS)):
        _, _, w, dec = _decay_rows(cdtr_ref[0, 0, c], dt_bias, a_neg, tri_ref[1], False)
        _state_update(ctx_tiles[c][0], ctx_tiles[c][1], w, dec, 4, hb_scr)

    def prepare(c):
        r0 = pl.multiple_of(c * CHUNK, CHUNK)
        xs = _conv_silu_tile(xs_ref, c, N_CHUNKS, cwx, bx)
        bt = _conv_silu_tile(b_ref, c, N_CHUNKS, cwb, bb).T
        cc = _conv_silu_tile(c_ref, c, N_CHUNKS, cwc, bc)
        cb = _dot(cc.astype(BF16), bt.astype(BF16))
        xs_scr[pl.ds(r0, CHUNK), :] = xs
        bt_scr[c] = bt
        cc_scr[c] = cc
        cb_scr[c] = cb
        return xs, bt, cc, cb

    def recall(c):
        r0 = pl.multiple_of(c * CHUNK, CHUNK)
        return xs_scr[pl.ds(r0, CHUNK), :], bt_scr[c], cc_scr[c], cb_scr[c]

    def scan(c, tiles, causal):
        xs, bt, cc, cb = tiles
        return _chunk_dir(xs, bt, cc, cb, dtr_ref[0, 0, c], dt_bias, a_neg,
                          tri_ref[0 if causal else 1], 0 if causal else 4,
                          hf_scr if causal else hb_scr, causal)

    d_sum = dsk_ref[0, 0:1, :] + dsk_ref[0, 1:2, :]
    nw = nw_ref[...]

    def finish(c, y_dir, xs):
        r0 = pl.multiple_of(c * CHUNK, CHUNK)
        y = y_scr[pl.ds(r0, CHUNK), :] + y_dir + xs * d_sum
        gated = y * _silu(z_ref[0, pl.ds(r0, CHUNK), :].astype(F32))
        ms = jnp.mean(gated * gated, axis=-1, keepdims=True)
        y_ref[0, pl.ds(r0, CHUNK), :] = (gated * lax.rsqrt(ms + EPS) * nw).astype(y_ref.dtype)

    def first_half(i, carry):
        for c, causal in ((i, True), (N_CHUNKS - 1 - i, False)):
            r0 = pl.multiple_of(c * CHUNK, CHUNK)
            y_scr[pl.ds(r0, CHUNK), :] = scan(c, prepare(c), causal)
        return carry

    def second_half(i, carry):
        for c, causal in ((i, True), (N_CHUNKS - 1 - i, False)):
            tiles = recall(c)
            finish(c, scan(c, tiles, causal), tiles[0])
        return carry

    lax.fori_loop(0, N_CHUNKS // 2, first_half, 0)
    lax.fori_loop(N_CHUNKS // 2, N_CHUNKS, second_half, 0)


def _ssd(proj, dtr, cproj, cdtr, conv_w, conv_b, dtb, alog, dsk, norm_w, tri):
    gx = COL_X // GROUP_W
    gz = COL_Z // GROUP_W
    gb = COL_B // SSD_STATE
    gc = COL_C // SSD_STATE
    in_specs = [
        pl.BlockSpec((1, SEQ, GROUP_W), lambda b, g: (b, 0, gx + g)),
        pl.BlockSpec((1, SEQ, SSD_STATE), lambda b, g: (b, 0, gb + g)),
        pl.BlockSpec((1, SEQ, SSD_STATE), lambda b, g: (b, 0, gc + g)),
        pl.BlockSpec((1, SEQ, GROUP_W), lambda b, g: (b, 0, gz + g)),
        pl.BlockSpec((1, 1, N_CHUNKS, 8, CHUNK), lambda b, g: (g, b, 0, 0, 0)),
        pl.BlockSpec((1, CTX_LEN, GROUP_W), lambda b, g: (b, 0, gx + g)),
        pl.BlockSpec((1, CTX_LEN, SSD_STATE), lambda b, g: (b, 0, gb + g)),
        pl.BlockSpec((1, 1, N_CTX_CHUNKS, 8, CHUNK), lambda b, g: (g, b, 0, 0, 0)),
        pl.BlockSpec((3, GROUP_W), lambda b, g: (0, gx + g)),
        pl.BlockSpec((3, SSD_STATE), lambda b, g: (0, gb + g)),
        pl.BlockSpec((3, SSD_STATE), lambda b, g: (0, gc + g)),
        pl.BlockSpec((1, GROUP_W), lambda b, g: (0, gx + g)),
        pl.BlockSpec((1, SSD_STATE), lambda b, g: (0, gb + g)),
        pl.BlockSpec((1, SSD_STATE), lambda b, g: (0, gc + g)),
        pl.BlockSpec((1, 8, 1), lambda b, g: (g, 0, 0)),
        pl.BlockSpec((1, 8, 1), lambda b, g: (g, 0, 0)),
        pl.BlockSpec((1, 2, GROUP_W), lambda b, g: (g, 0, 0)),
        pl.BlockSpec((1, GROUP_W), lambda b, g: (0, g)),
        pl.BlockSpec((2, CHUNK, CHUNK), lambda b, g: (0, 0, 0)),
    ]
    return pl.pallas_call(
        _ssd_kernel,
        grid=(BATCH, SSD_GROUPS),
        in_specs=in_specs,
        out_specs=pl.BlockSpec((1, SEQ, GROUP_W), lambda b, g: (b, 0, g)),
        out_shape=jax.ShapeDtypeStruct((BATCH, SEQ, D_MODEL), BF16),
        scratch_shapes=[pltpu.VMEM((SEQ, GROUP_W), F32),
                        pltpu.VMEM((N_CHUNKS, CHUNK, CHUNK), F32),
                        pltpu.VMEM((N_CHUNKS, CHUNK, CHUNK), F32),
                        pltpu.VMEM((N_CHUNKS, CHUNK, CHUNK), F32),
                        pltpu.VMEM((SEQ, GROUP_W), F32),
                        pltpu.VMEM((SSD_STATE, GROUP_W), F32),
                        pltpu.VMEM((SSD_STATE, GROUP_W), F32)],
        compiler_params=_params(("arbitrary", "arbitrary")),
        name="ssd",
    )(proj, proj, proj, proj, dtr, cproj, cproj, cdtr,
      conv_w, conv_w, conv_w, conv_b, conv_b, conv_b, dtb, alog, dsk, norm_w, tri)


def _pool_kernel(v_ref, win_ref, inv_ref, pw_ref, ps_ref, o_ref):
    v = v_ref[...]
    sums = _dot(win_ref[0], v)
    p = sums * inv_ref[0] - v.astype(F32)
    o_ref[...] = (_dot(p.astype(BF16), pw_ref[0]) * ps_ref[...]).astype(o_ref.dtype)


def _pool_constants(tm):
    t = np.arange(tm)
    col = t % GRID_W
    wins, invs = [], []
    for w in POOL_WINDOWS:
        start = np.clip(col - w // 2, 0, GRID_W)
        end = np.clip(col + w - w // 2, 0, GRID_W)
        same_row = (t[:, None] // GRID_W) == (t[None, :] // GRID_W)
        inside = (col[None, :] >= start[:, None]) & (col[None, :] < end[:, None])
        wins.append((same_row & inside).astype(np.float32))
        invs.append((1.0 / (end - start).astype(np.float32))[:, None])
    return jnp.asarray(np.stack(wins), BF16), jnp.asarray(np.stack(invs), F32)


def _pool(proj2d, pool_w, pool_scale, tm=512):
    win, inv = _pool_constants(tm)
    gv = COL_V // POOL_GW
    return pl.pallas_call(
        _pool_kernel,
        grid=(TOKENS // tm, len(POOL_WINDOWS)),
        in_specs=[pl.BlockSpec((tm, POOL_GW), lambda i, k: (i, gv + k)),
                  pl.BlockSpec((1, tm, tm), lambda i, k: (k, 0, 0)),
                  pl.BlockSpec((1, tm, 1), lambda i, k: (k, 0, 0)),
                  pl.BlockSpec((1, POOL_GW, POOL_GW), lambda i, k: (k, 0, 0)),
                  pl.BlockSpec((1, POOL_GW), lambda i, k: (0, k))],
        out_specs=pl.BlockSpec((tm, POOL_GW), lambda i, k: (i, k)),
        out_shape=jax.ShapeDtypeStruct((TOKENS, D_MODEL), BF16),
        compiler_params=_params(("arbitrary", "arbitrary")),
        name="pool",
    )(proj2d, win, inv, pool_w, pool_scale)


def _merge_kernel(ys_ref, yp_ref, ws_ref, wp_ref, gs_ref, gp_ref, o_ref):
    a = jax.nn.sigmoid(gs_ref[...].astype(F32)) * _dot(ys_ref[...], ws_ref[...])
    b = jax.nn.sigmoid(gp_ref[...].astype(F32)) * _dot(yp_ref[...], wp_ref[...])
    o_ref[...] = (a + b).astype(o_ref.dtype)


def _merge(y_ssd, y_pool, w_s, w_p, proj2d, tm=1024, tn=512):
    ggs = COL_GS // tn
    ggp = COL_GP // tn
    return pl.pallas_call(
        _merge_kernel,
        grid=(TOKENS // tm, D_MODEL // tn),
        in_specs=[pl.BlockSpec((tm, D_MODEL), lambda i, j: (i, 0)),
                  pl.BlockSpec((tm, D_MODEL), lambda i, j: (i, 0)),
                  pl.BlockSpec((D_MODEL, tn), lambda i, j: (0, j)),
                  pl.BlockSpec((D_MODEL, tn), lambda i, j: (0, j)),
                  pl.BlockSpec((tm, tn), lambda i, j: (i, ggs + j)),
                  pl.BlockSpec((tm, tn), lambda i, j: (i, ggp + j))],
        out_specs=pl.BlockSpec((tm, tn), lambda i, j: (i, j)),
        out_shape=jax.ShapeDtypeStruct((TOKENS, D_MODEL), BF16),
        compiler_params=_params(("arbitrary", "arbitrary")),
        name="merge",
    )(y_ssd, y_pool, w_s, w_p, proj2d, proj2d)


def _outproj_kernel(m_ref, wo_ref, x_ref, g1_ref, sh_ref, sc_ref, nw_ref, x1_ref, h2_ref):
    x1_ref[...] = x_ref[...] + g1_ref[0] * _dot(m_ref[...], wo_ref[...])
    _modulate_rows(x1_ref, nw_ref[...], sc_ref[0], sh_ref[0], h2_ref, x1_ref.shape[0], rc=128)


def _outproj(merged, w_o, x2d, g1, sh2, sc2, nw2, tm=256):
    tpb = SEQ // tm
    mod_spec = pl.BlockSpec((1, 1, D_MODEL), lambda i: (i // tpb, 0, 0))
    return pl.pallas_call(
        _outproj_kernel,
        grid=(TOKENS // tm,),
        in_specs=[pl.BlockSpec((tm, D_MODEL), lambda i: (i, 0)),
                  pl.BlockSpec((D_MODEL, D_MODEL), lambda i: (0, 0)),
                  pl.BlockSpec((tm, D_MODEL), lambda i: (i, 0)),
                  mod_spec, mod_spec, mod_spec,
                  pl.BlockSpec((1, D_MODEL), lambda i: (0, 0))],
        out_specs=[pl.BlockSpec((tm, D_MODEL), lambda i: (i, 0)),
                   pl.BlockSpec((tm, D_MODEL), lambda i: (i, 0))],
        out_shape=[jax.ShapeDtypeStruct((TOKENS, D_MODEL), F32),
                   jax.ShapeDtypeStruct((TOKENS, D_MODEL), BF16)],
        compiler_params=_params(("arbitrary",)),
        name="outproj",
    )(merged, w_o, x2d, g1, sh2, sc2, nw2)


def _up_kernel(h_ref, w_ref, o_ref):
    o_ref[...] = _dot(h_ref[...], w_ref[...]).astype(o_ref.dtype)


def _up(h2, w_up, tm=1024, tn=1024):
    n = w_up.shape[1]
    return pl.pallas_call(
        _up_kernel,
        grid=(TOKENS // tm, n // tn),
        in_specs=[pl.BlockSpec((tm, D_MODEL), lambda i, j: (i, 0)),
                  pl.BlockSpec((D_MODEL, tn), lambda i, j: (0, j))],
        out_specs=pl.BlockSpec((tm, tn), lambda i, j: (i, j)),
        out_shape=jax.ShapeDtypeStruct((TOKENS, n), BF16),
        compiler_params=_params(("arbitrary", "arbitrary")),
        name="up",
    )(h2, w_up)


def _down_kernel(a_ref, g_ref, gp_ref, gn_ref, fw_ref, fb_ref, wd_ref, x1_ref, g2_ref, nw_ref,
                 o_ref, acc_ref):
    i = pl.program_id(0)
    kk = pl.program_id(1)
    tm = a_ref.shape[0]

    @pl.when(kk == 0)
    def _():
        acc_ref[...] = jnp.zeros(acc_ref.shape, F32)

    has_prev = ((i * tm) % SEQ) != 0
    has_next = (((i + 1) * tm) % SEQ) != 0
    prev = jnp.where(has_prev, gp_ref[...].astype(F32), 0.0)
    nxt = jnp.where(has_next, gn_ref[...].astype(F32), 0.0)
    ext = jnp.concatenate([prev, g_ref[...].astype(F32), nxt], axis=0)
    fw = fw_ref[...]
    cv = (ext[0:tm] * fw[0:1, :] + fb_ref[...] + ext[GRID_W:GRID_W + tm] * fw[1:2, :]
          + ext[2 * GRID_W:2 * GRID_W + tm] * fw[2:3, :])
    gelu = 0.5 * cv * (1.0 + lax.erf(cv * np.float32(np.sqrt(0.5))))
    u = gelu * a_ref[...].astype(F32)
    acc_ref[...] += _dot(u.astype(BF16), wd_ref[...])

    @pl.when(kk == pl.num_programs(1) - 1)
    def _():
        x2 = x1_ref[...] + g2_ref[0] * acc_ref[...]
        ms = jnp.mean(x2 * x2, axis=-1, keepdims=True)
        o_ref[...] = x2 * lax.rsqrt(ms + EPS) * nw_ref[...]


def _down(up, ffn_conv_w, ffn_conv_b, w_down, x1, g2, fnw, tm=512, tk=512):
    nk = D_FF // tk
    tpb = SEQ // tm
    hb = tm // GRID_W
    n_halo = TOKENS // GRID_W
    return pl.pallas_call(
        _down_kernel,
        grid=(TOKENS // tm, nk),
        in_specs=[pl.BlockSpec((tm, tk), lambda i, k: (i, k)),
                  pl.BlockSpec((tm, tk), lambda i, k: (i, nk + k)),
                  pl.BlockSpec((GRID_W, tk), lambda i, k: (jnp.maximum(i * hb - 1, 0), nk + k)),
                  pl.BlockSpec((GRID_W, tk),
                               lambda i, k: (jnp.minimum((i + 1) * hb, n_halo - 1), nk + k)),
                  pl.BlockSpec((3, tk), lambda i, k: (0, k)),
                  pl.BlockSpec((1, tk), lambda i, k: (0, k)),
                  pl.BlockSpec((tk, D_MODEL), lambda i, k: (k, 0)),
                  pl.BlockSpec((tm, D_MODEL), lambda i, k: (i, 0)),
                  pl.BlockSpec((1, 1, D_MODEL), lambda i, k: (i // tpb, 0, 0)),
                  pl.BlockSpec((1, D_MODEL), lambda i, k: (0, 0))],
        out_specs=pl.BlockSpec((tm, D_MODEL), lambda i, k: (i, 0)),
        out_shape=jax.ShapeDtypeStruct((TOKENS, D_MODEL), F32),
        scratch_shapes=[pltpu.VMEM((tm, D_MODEL), F32)],
        compiler_params=_params(("arbitrary", "arbitrary")),
        name="down",
    )(up, up, up, up, ffn_conv_w, ffn_conv_b, w_down, x1, g2, fnw)


def _dt_rows(dt_raw, batch, n_chunks):
    d = dt_raw[:, :2 * SSD_HEADS].reshape(batch, n_chunks, CHUNK, 2, SSD_GROUPS, SSD_REP)
    d = jnp.transpose(d, (4, 0, 1, 3, 5, 2))
    return d.reshape(SSD_GROUPS, batch, n_chunks, 2 * SSD_REP, CHUNK)


def _group_rows(p):
    p = p.reshape(2, SSD_GROUPS, SSD_REP)
    return jnp.transpose(p, (1, 0, 2)).reshape(SSD_GROUPS, 2 * SSD_REP, 1)


def kernel(x, c, ctx, c_ctx, w_ada, b_ada, norm1_w, w_in, ssd_conv_w, ssd_conv_b, dt_bias, a_log,
           d_skip, ssd_norm_w, w_ssd_out, pool_w, pool_scale, w_pool_out, w_o, norm2_w, w_up,
           ffn_conv_w, ffn_conv_b, w_down, final_norm_w):
    assert w_in.shape[0] == 1, "single-layer block only"
    d = D_MODEL
    x2d = x.reshape(TOKENS, d)
    ctx2d = ctx.reshape(BATCH * CTX_LEN, d)

    cs = jnp.zeros((8, d), F32).at[:BATCH].set(c).at[BATCH].set(c_ctx)
    mod = _ada(cs, w_ada[0], b_ada)
    sh1, sc1, g1, sh2, sc2, g2 = [mod[:BATCH, k * d:(k + 1) * d].reshape(BATCH, 1, d)
                                  for k in range(6)]
    csh1 = mod[BATCH:BATCH + 1, 0:d].reshape(1, 1, d)
    csc1 = mod[BATCH:BATCH + 1, d:2 * d].reshape(1, 1, d)

    w_in0 = w_in[0]
    n_dt = 2 * SSD_HEADS
    w_dt = jnp.pad(w_in0[:, :n_dt], ((0, 0), (0, LANES - n_dt))).astype(BF16)
    w_cat = w_in0[:, n_dt:].astype(BF16)
    w_s = w_ssd_out[0].astype(BF16)
    w_p = w_pool_out[0].astype(BF16)
    w_ob = w_o[0].astype(BF16)
    w_upb = w_up[0].astype(BF16)
    w_db = w_down[0].astype(BF16)
    pool_wb = pool_w[0].astype(BF16)

    proj, dt_raw = _inproj(x2d, sh1, sc1, norm1_w, w_cat, w_dt, n_cols=N_PROJ, rows_per_mod=SEQ)
    cproj, cdt_raw = _inproj(ctx2d, csh1, csc1, norm1_w, w_cat, w_dt, n_cols=N_CTX_PROJ,
                             rows_per_mod=BATCH * CTX_LEN)

    dtr = _dt_rows(dt_raw, BATCH, N_CHUNKS)
    cdtr = _dt_rows(cdt_raw, BATCH, N_CTX_CHUNKS)
    dsk = jnp.repeat(jnp.transpose(d_skip[0].reshape(2, SSD_GROUPS, SSD_REP), (1, 0, 2)),
                     SSD_HEAD_DIM, axis=-1)
    idx = np.arange(CHUNK)
    tri = jnp.asarray(np.stack([idx[:, None] <= idx[None, :], idx[:, None] >= idx[None, :]])
                      .astype(np.float32), BF16)

    y_ssd = _ssd(proj.reshape(BATCH, SEQ, N_PROJ), dtr,
                 cproj.reshape(BATCH, CTX_LEN, N_CTX_PROJ), cdtr,
                 ssd_conv_w[0], ssd_conv_b, _group_rows(dt_bias[0]), _group_rows(a_log[0]),
                 dsk, ssd_norm_w, tri)
    y_pool = _pool(proj, pool_wb, pool_scale)
    merged = _merge(y_ssd.reshape(TOKENS, d), y_pool, w_s, w_p, proj)
    x1, h2 = _outproj(merged, w_ob, x2d, g1, sh2, sc2, norm2_w)
    up = _up(h2, w_upb)
    out = _down(up, ffn_conv_w[0], ffn_conv_b, w_db, x1, g2, final_norm_w.reshape(1, d))
    return out.reshape(BATCH, SEQ, d)
```

```python
import functools

import numpy as np
import jax
import jax.numpy as jnp
from jax import lax
from jax.experimental import pallas as pl
from jax.experimental.pallas import tpu as pltpu

F32 = jnp.float32
BF16 = jnp.bfloat16

D_MODEL = 2048
BATCH = 4
SEQ = 4096
TOKENS = BATCH * SEQ
GRID_W = 64
CTX_LEN = 256

SSD_HEAD_DIM = 64
SSD_HEADS = 32
SSD_GROUPS = 8
SSD_REP = 4
SSD_STATE = 128
CHUNK = 128
GROUP_W = SSD_REP * SSD_HEAD_DIM
N_CHUNKS = SEQ // CHUNK
N_CTX_CHUNKS = CTX_LEN // CHUNK
SCAN_UNROLL = 2
GLU_ROWS = 64

POOL_WINDOWS = (2, 4, 8, 16)
POOL_GW = 512
D_FF = 5632
EPS = 1e-6

COL_X = 0
COL_B = 2048
COL_C = 3072
COL_Z = 4096
COL_V = 6144
COL_GS = 8192
COL_GP = 10240
N_PROJ = 12288
N_CTX_PROJ = 3072

LANES = 128
VMEM_LIMIT = 56 * 1024 * 1024


def _dot(a, b):
    return jnp.dot(a, b, preferred_element_type=F32)


def _silu(x):
    return x * jax.nn.sigmoid(x)


def _softplus(x):
    return jnp.maximum(x, 0.0) + jnp.log1p(jnp.exp(-jnp.abs(x)))


def _params(sem):
    return pltpu.CompilerParams(dimension_semantics=sem, vmem_limit_bytes=VMEM_LIMIT)


def _ada_kernel(c_ref, w_ref, b_ref, o_ref):
    s = _silu(c_ref[...])
    o_ref[...] = jnp.dot(s, w_ref[...], precision=lax.Precision.HIGHEST,
                         preferred_element_type=F32) + b_ref[...]


def _ada(cs, w, b):
    tn = 1024
    n = w.shape[1]
    return pl.pallas_call(
        _ada_kernel,
        grid=(n // tn,),
        in_specs=[pl.BlockSpec((8, D_MODEL), lambda j: (0, 0)),
                  pl.BlockSpec((D_MODEL, tn), lambda j: (0, j)),
                  pl.BlockSpec((1, tn), lambda j: (0, j))],
        out_specs=pl.BlockSpec((8, tn), lambda j: (0, j)),
        out_shape=jax.ShapeDtypeStruct((8, n), F32),
        compiler_params=_params(("arbitrary",)),
        name="ada",
    )(cs, w, b)


def _modulate_rows(x_ref, nw, sc, sh, out_ref, rows, rc=256):
    def body(k, carry):
        r = pl.multiple_of(k * rc, rc)
        x = x_ref[pl.ds(r, rc), :]
        ms = jnp.mean(x * x, axis=-1, keepdims=True)
        y = x * lax.rsqrt(ms + EPS) * nw
        out_ref[pl.ds(r, rc), :] = (y * (1.0 + sc) + sh).astype(out_ref.dtype)
        return carry
    lax.fori_loop(0, rows // rc, body, 0)


def _inproj_kernel(x_ref, sh_ref, sc_ref, nw_ref, w_ref, wdt_ref, o_ref, dt_ref, h_scr):
    @pl.when(pl.program_id(1) == 0)
    def _():
        _modulate_rows(x_ref, nw_ref[...], sc_ref[0], sh_ref[0], h_scr, h_scr.shape[0])
        dt_ref[...] = _dot(h_scr[...], wdt_ref[...])

    o_ref[...] = _dot(h_scr[...], w_ref[...]).astype(o_ref.dtype)


def _inproj(x2d, sh, sc, nw, w_cat, w_dt, *, n_cols, rows_per_mod, tm=1024, tn=1024):
    m = x2d.shape[0]
    tpb = rows_per_mod // tm
    return pl.pallas_call(
        _inproj_kernel,
        grid=(m // tm, n_cols // tn),
        in_specs=[pl.BlockSpec((tm, D_MODEL), lambda i, j: (i, 0)),
                  pl.BlockSpec((1, 1, D_MODEL), lambda i, j: (i // tpb, 0, 0)),
                  pl.BlockSpec((1, 1, D_MODEL), lambda i, j: (i // tpb, 0, 0)),
                  pl.BlockSpec((1, D_MODEL), lambda i, j: (0, 0)),
                  pl.BlockSpec((D_MODEL, tn), lambda i, j: (0, j)),
                  pl.BlockSpec((D_MODEL, LANES), lambda i, j: (0, 0))],
        out_specs=[pl.BlockSpec((tm, tn), lambda i, j: (i, j)),
                   pl.BlockSpec((tm, LANES), lambda i, j: (i, 0))],
        out_shape=[jax.ShapeDtypeStruct((m, n_cols), BF16),
                   jax.ShapeDtypeStruct((m, LANES), F32)],
        scratch_shapes=[pltpu.VMEM((tm, D_MODEL), BF16)],
        compiler_params=_params(("arbitrary", "arbitrary")),
        name="inproj",
    )(x2d, sh, sc, nw, w_cat, w_dt)


def _conv_silu_tile(ref, c, n_chunks, w, bias):
    length = n_chunks * CHUNK
    if isinstance(c, int):
        r0, p0, n0 = c * CHUNK, max(c * CHUNK - 16, 0), min(c * CHUNK + CHUNK, length - 16)
    else:
        r0 = pl.multiple_of(c * CHUNK, CHUNK)
        p0 = pl.multiple_of(jnp.maximum(c * CHUNK - 16, 0), 16)
        n0 = pl.multiple_of(jnp.minimum(c * CHUNK + CHUNK, length - 16), 16)
    x = ref[0, pl.ds(r0, CHUNK), :].astype(F32)
    prev_row = ref[0, pl.ds(p0, 16), :].astype(F32)[15:16, :]
    prev_row = jnp.where(c > 0, prev_row, 0.0)
    next_row = ref[0, pl.ds(n0, 16), :].astype(F32)[0:1, :]
    next_row = jnp.where(c < n_chunks - 1, next_row, 0.0)
    rows = lax.broadcasted_iota(jnp.int32, x.shape, 0)
    xm = jnp.where(rows == 0, prev_row, pltpu.roll(x, 1, axis=0))
    xp = jnp.where(rows == CHUNK - 1, next_row, pltpu.roll(x, CHUNK - 1, axis=0))
    y = xm * w[0:1, :] + x * w[1:2, :] + xp * w[2:3, :] + bias
    return _silu(y)


def _decay_rows(dt_raw, dt_bias, a_neg, tri, causal):
    dt = _softplus(dt_raw + dt_bias)
    a = dt * a_neg
    hi = a.astype(BF16)
    lo = (a - hi.astype(F32)).astype(BF16)
    r = _dot(jnp.concatenate([hi, lo], axis=0), tri)
    acs = r[0:8] + r[8:16]
    tot = acs[:, CHUNK - 1:CHUNK] if causal else acs[:, 0:1]
    return acs, dt, jnp.exp(tot - acs) * dt, jnp.exp(tot)


def _col_form(acs):
    return jnp.concatenate([acs, jnp.zeros((CHUNK - 8, CHUNK), F32)], axis=0).T


def _head_lanes(dec, off):
    lane = lax.broadcasted_iota(jnp.int32, (1, LANES), 1)
    t0 = jnp.where(lane < SSD_HEAD_DIM, dec[off:off + 1, :], dec[off + 1:off + 2, :])
    t1 = jnp.where(lane < SSD_HEAD_DIM, dec[off + 2:off + 3, :], dec[off + 3:off + 4, :])
    return jnp.concatenate([t0, t1], axis=1)


def _split_heads(t):
    lane = lax.broadcasted_iota(jnp.int32, t.shape, 1)
    return (jnp.where(lane < SSD_HEAD_DIM, t, 0.0).astype(BF16),
            jnp.where(lane >= SSD_HEAD_DIM, t, 0.0).astype(BF16))


def _state_update(x_split, bt, w, dec, off, h_ref):
    h = h_ref[...]
    ss = []
    for pair in range(2):
        bw = [(bt * w[off + r:off + r + 1, :]).astype(BF16) for r in (2 * pair, 2 * pair + 1)]
        ss.append(_dot(jnp.concatenate(bw, axis=1), jnp.concatenate(x_split[pair], axis=0)))
    h_ref[...] = h * _head_lanes(dec, off) + jnp.concatenate(ss, axis=1)
    return h


def _chunk_dir(xs, bt, cc, cb, dt_raw, dt_bias, a_neg, tri, off, h_ref, causal):
    acs, dt, w, dec = _decay_rows(dt_raw, dt_bias, a_neg, tri, causal)
    rt = _col_form(acs)
    l_io = lax.broadcasted_iota(jnp.int32, (CHUNK, CHUNK), 0)
    s_io = lax.broadcasted_iota(jnp.int32, (CHUNK, CHUNK), 1)
    mask = (l_io >= s_io) if causal else (l_io <= s_io)
    x_split = [_split_heads(xs[:, p * LANES:(p + 1) * LANES]) for p in range(2)]
    h = _state_update(x_split, bt, w, dec, off, h_ref)
    ys = []
    for pair in range(2):
        h_top, h_bot = _split_heads(h[:, pair * LANES:(pair + 1) * LANES])
        lhs = []
        for r in (2 * pair, 2 * pair + 1):
            k = off + r
            col = jnp.broadcast_to(rt[:, k:k + 1], (CHUNK, CHUNK))
            decay = jnp.exp(jnp.where(mask, col - acs[k:k + 1, :], -jnp.inf))
            lhs.append((cb * decay * dt[k:k + 1, :]).astype(BF16))
            lhs.append((cc * jnp.exp(col)).astype(BF16))
        rhs = jnp.concatenate([x_split[pair][0], h_top, x_split[pair][1], h_bot], axis=0)
        ys.append(_dot(jnp.concatenate(lhs, axis=1), rhs))
    return jnp.concatenate(ys, axis=1)


def _ssd_kernel(xs_ref, b_ref, c_ref, z_ref, dtr_ref, cxs_ref, cbm_ref, cdtr_ref,
                cwx_ref, cwb_ref, cwc_ref, bx_ref, bb_ref, bc_ref,
                dtb_ref, alog_ref, dsk_ref, nw_ref, tri_ref,
                y_ref,
                xs_scr, bt_scr, cc_scr, cb_scr, y_scr, hf_scr, hb_scr):
    dt_bias = dtb_ref[0]
    a_neg = -jnp.exp(alog_ref[0])
    cwx, cwb, cwc = cwx_ref[...], cwb_ref[...], cwc_ref[...]
    bx, bb, bc = bx_ref[...], bb_ref[...], bc_ref[...]

    hf_scr[...] = jnp.zeros(hf_scr.shape, F32)
    hb_scr[...] = jnp.zeros(hb_scr.shape, F32)
    ctx_tiles = []
    for c in range(N_CTX_CHUNKS):
        xs = _conv_silu_tile(cxs_ref, c, N_CTX_CHUNKS, cwx, bx)
        bt = _conv_silu_tile(cbm_ref, c, N_CTX_CHUNKS, cwb, bb).T
        ctx_tiles.append(([_split_heads(xs[:, p * LANES:(p + 1) * LANES]) for p in range(2)], bt))
    for c in range(N_CTX_CHUNKS):
        _, _, w, dec = _decay_rows(cdtr_ref[0, 0, c], dt_bias, a_neg, tri_ref[0], True)
        _state_update(ctx_tiles[c][0], ctx_tiles[c][1], w, dec, 0, hf_scr)
    for c in reversed(range(N_CTX_CHUNKS)):
        _, _, w, dec = _decay_rows(cdtr_ref[0, 0, c], dt_bias, a_neg, tri_ref[1], False)
        _state_update(ctx_tiles[c][0], ctx_tiles[c][1], w, dec, 4, hb_scr)

    def prepare(c):
        r0 = pl.multiple_of(c * CHUNK, CHUNK)
        xs = _conv_silu_tile(xs_ref, c, N_CHUNKS, cwx, bx)
        bt = _conv_silu_tile(b_ref, c, N_CHUNKS, cwb, bb).T
        cc = _conv_silu_tile(c_ref, c, N_CHUNKS, cwc, bc)
        cb = _dot(cc.astype(BF16), bt.astype(BF16))
        xs_scr[pl.ds(r0, CHUNK), :] = xs
        bt_scr[c] = bt
        cc_scr[c] = cc
        cb_scr[c] = cb
        return xs, bt, cc, cb

    def recall(c):
        r0 = pl.multiple_of(c * CHUNK, CHUNK)
        return xs_scr[pl.ds(r0, CHUNK), :], bt_scr[c], cc_scr[c], cb_scr[c]

    def scan(c, tiles, causal):
        xs, bt, cc, cb = tiles
        return _chunk_dir(xs, bt, cc, cb, dtr_ref[0, 0, c], dt_bias, a_neg,
                          tri_ref[0 if causal else 1], 0 if causal else 4,
                          hf_scr if causal else hb_scr, causal)

    d_sum = dsk_ref[0, 0:1, :] + dsk_ref[0, 1:2, :]
    nw = nw_ref[...]

    def finish(c, y_dir, xs):
        r0 = pl.multiple_of(c * CHUNK, CHUNK)
        y = y_scr[pl.ds(r0, CHUNK), :] + y_dir + xs * d_sum
        gated = y * _silu(z_ref[0, pl.ds(r0, CHUNK), :].astype(F32))
        ms = jnp.mean(gated * gated, axis=-1, keepdims=True)
        y_ref[0, pl.ds(r0, CHUNK), :] = (gated * lax.rsqrt(ms + EPS) * nw).astype(y_ref.dtype)

    def visits(i):
        fwd = [i * SCAN_UNROLL + u for u in range(SCAN_UNROLL)]
        return [(c, True) for c in fwd] + [(N_CHUNKS - 1 - c, False) for c in fwd]

    def first_half(i, carry):
        for c, causal in visits(i):
            r0 = pl.multiple_of(c * CHUNK, CHUNK)
            y_scr[pl.ds(r0, CHUNK), :] = scan(c, prepare(c), causal)
        return carry

    def second_half(i, carry):
        for c, causal in visits(i):
            tiles = recall(c)
            finish(c, scan(c, tiles, causal), tiles[0])
        return carry

    half = N_CHUNKS // 2 // SCAN_UNROLL
    lax.fori_loop(0, half, first_half, 0)
    lax.fori_loop(half, 2 * half, second_half, 0)


def _ssd(proj, dtr, cproj, cdtr, conv_w, conv_b, dtb, alog, dsk, norm_w, tri):
    gx = COL_X // GROUP_W
    gz = COL_Z // GROUP_W
    gb = COL_B // SSD_STATE
    gc = COL_C // SSD_STATE
    in_specs = [
        pl.BlockSpec((1, SEQ, GROUP_W), lambda b, g: (b, 0, gx + g)),
        pl.BlockSpec((1, SEQ, SSD_STATE), lambda b, g: (b, 0, gb + g)),
        pl.BlockSpec((1, SEQ, SSD_STATE), lambda b, g: (b, 0, gc + g)),
        pl.BlockSpec((1, SEQ, GROUP_W), lambda b, g: (b, 0, gz + g)),
        pl.BlockSpec((1, 1, N_CHUNKS, 8, CHUNK), lambda b, g: (g, b, 0, 0, 0)),
        pl.BlockSpec((1, CTX_LEN, GROUP_W), lambda b, g: (b, 0, gx + g)),
        pl.BlockSpec((1, CTX_LEN, SSD_STATE), lambda b, g: (b, 0, gb + g)),
        pl.BlockSpec((1, 1, N_CTX_CHUNKS, 8, CHUNK), lambda b, g: (g, b, 0, 0, 0)),
        pl.BlockSpec((3, GROUP_W), lambda b, g: (0, gx + g)),
        pl.BlockSpec((3, SSD_STATE), lambda b, g: (0, gb + g)),
        pl.BlockSpec((3, SSD_STATE), lambda b, g: (0, gc + g)),
        pl.BlockSpec((1, GROUP_W), lambda b, g: (0, gx + g)),
        pl.BlockSpec((1, SSD_STATE), lambda b, g: (0, gb + g)),
        pl.BlockSpec((1, SSD_STATE), lambda b, g: (0, gc + g)),
        pl.BlockSpec((1, 8, 1), lambda b, g: (g, 0, 0)),
        pl.BlockSpec((1, 8, 1), lambda b, g: (g, 0, 0)),
        pl.BlockSpec((1, 2, GROUP_W), lambda b, g: (g, 0, 0)),
        pl.BlockSpec((1, GROUP_W), lambda b, g: (0, g)),
        pl.BlockSpec((2, CHUNK, CHUNK), lambda b, g: (0, 0, 0)),
    ]
    return pl.pallas_call(
        _ssd_kernel,
        grid=(BATCH, SSD_GROUPS),
        in_specs=in_specs,
        out_specs=pl.BlockSpec((1, SEQ, GROUP_W), lambda b, g: (b, 0, g)),
        out_shape=jax.ShapeDtypeStruct((BATCH, SEQ, D_MODEL), BF16),
        scratch_shapes=[pltpu.VMEM((SEQ, GROUP_W), F32),
                        pltpu.VMEM((N_CHUNKS, CHUNK, CHUNK), F32),
                        pltpu.VMEM((N_CHUNKS, CHUNK, CHUNK), F32),
                        pltpu.VMEM((N_CHUNKS, CHUNK, CHUNK), F32),
                        pltpu.VMEM((SEQ, GROUP_W), F32),
                        pltpu.VMEM((SSD_STATE, GROUP_W), F32),
                        pltpu.VMEM((SSD_STATE, GROUP_W), F32)],
        compiler_params=_params(("arbitrary", "arbitrary")),
        name="ssd",
    )(proj, proj, proj, proj, dtr, cproj, cproj, cdtr,
      conv_w, conv_w, conv_w, conv_b, conv_b, conv_b, dtb, alog, dsk, norm_w, tri)


def _pool_kernel(v_ref, win_ref, inv_ref, pw_ref, ps_ref, o_ref):
    v = v_ref[...]
    sums = _dot(win_ref[0], v)
    p = sums * inv_ref[0] - v.astype(F32)
    o_ref[...] = (_dot(p.astype(BF16), pw_ref[0]) * ps_ref[...]).astype(o_ref.dtype)


def _pool_constants(tm):
    t = np.arange(tm)
    col = t % GRID_W
    wins, invs = [], []
    for w in POOL_WINDOWS:
        start = np.clip(col - w // 2, 0, GRID_W)
        end = np.clip(col + w - w // 2, 0, GRID_W)
        same_row = (t[:, None] // GRID_W) == (t[None, :] // GRID_W)
        inside = (col[None, :] >= start[:, None]) & (col[None, :] < end[:, None])
        wins.append((same_row & inside).astype(np.float32))
        invs.append((1.0 / (end - start).astype(np.float32))[:, None])
    return jnp.asarray(np.stack(wins), BF16), jnp.asarray(np.stack(invs), F32)


def _pool(proj2d, pool_w, pool_scale, tm=512):
    win, inv = _pool_constants(tm)
    gv = COL_V // POOL_GW
    return pl.pallas_call(
        _pool_kernel,
        grid=(TOKENS // tm, len(POOL_WINDOWS)),
        in_specs=[pl.BlockSpec((tm, POOL_GW), lambda i, k: (i, gv + k)),
                  pl.BlockSpec((1, tm, tm), lambda i, k: (k, 0, 0)),
                  pl.BlockSpec((1, tm, 1), lambda i, k: (k, 0, 0)),
                  pl.BlockSpec((1, POOL_GW, POOL_GW), lambda i, k: (k, 0, 0)),
                  pl.BlockSpec((1, POOL_GW), lambda i, k: (0, k))],
        out_specs=pl.BlockSpec((tm, POOL_GW), lambda i, k: (i, k)),
        out_shape=jax.ShapeDtypeStruct((TOKENS, D_MODEL), BF16),
        compiler_params=_params(("arbitrary", "arbitrary")),
        name="pool",
    )(proj2d, win, inv, pool_w, pool_scale)


def _merge_kernel(ys_ref, yp_ref, ws_ref, wp_ref, gs_ref, gp_ref, o_ref):
    a = jax.nn.sigmoid(gs_ref[...].astype(F32)) * _dot(ys_ref[...], ws_ref[...])
    b = jax.nn.sigmoid(gp_ref[...].astype(F32)) * _dot(yp_ref[...], wp_ref[...])
    o_ref[...] = (a + b).astype(o_ref.dtype)


def _merge(y_ssd, y_pool, w_s, w_p, proj2d, tm=1024, tn=512):
    ggs = COL_GS // tn
    ggp = COL_GP // tn
    return pl.pallas_call(
        _merge_kernel,
        grid=(TOKENS // tm, D_MODEL // tn),
        in_specs=[pl.BlockSpec((tm, D_MODEL), lambda i, j: (i, 0)),
                  pl.BlockSpec((tm, D_MODEL), lambda i, j: (i, 0)),
                  pl.BlockSpec((D_MODEL, tn), lambda i, j: (0, j)),
                  pl.BlockSpec((D_MODEL, tn), lambda i, j: (0, j)),
                  pl.BlockSpec((tm, tn), lambda i, j: (i, ggs + j)),
                  pl.BlockSpec((tm, tn), lambda i, j: (i, ggp + j))],
        out_specs=pl.BlockSpec((tm, tn), lambda i, j: (i, j)),
        out_shape=jax.ShapeDtypeStruct((TOKENS, D_MODEL), BF16),
        compiler_params=_params(("arbitrary", "arbitrary")),
        name="merge",
    )(y_ssd, y_pool, w_s, w_p, proj2d, proj2d)


def _outproj_kernel(m_ref, wo_ref, x_ref, g1_ref, sh_ref, sc_ref, nw_ref, x1_ref, h2_ref):
    x1_ref[...] = x_ref[...] + g1_ref[0] * _dot(m_ref[...], wo_ref[...])
    _modulate_rows(x1_ref, nw_ref[...], sc_ref[0], sh_ref[0], h2_ref, x1_ref.shape[0], rc=128)


def _outproj(merged, w_o, x2d, g1, sh2, sc2, nw2, tm=256):
    tpb = SEQ // tm
    mod_spec = pl.BlockSpec((1, 1, D_MODEL), lambda i: (i // tpb, 0, 0))
    return pl.pallas_call(
        _outproj_kernel,
        grid=(TOKENS // tm,),
        in_specs=[pl.BlockSpec((tm, D_MODEL), lambda i: (i, 0)),
                  pl.BlockSpec((D_MODEL, D_MODEL), lambda i: (0, 0)),
                  pl.BlockSpec((tm, D_MODEL), lambda i: (i, 0)),
                  mod_spec, mod_spec, mod_spec,
                  pl.BlockSpec((1, D_MODEL), lambda i: (0, 0))],
        out_specs=[pl.BlockSpec((tm, D_MODEL), lambda i: (i, 0)),
                   pl.BlockSpec((tm, D_MODEL), lambda i: (i, 0))],
        out_shape=[jax.ShapeDtypeStruct((TOKENS, D_MODEL), F32),
                   jax.ShapeDtypeStruct((TOKENS, D_MODEL), BF16)],
        compiler_params=_params(("arbitrary",)),
        name="outproj",
    )(merged, w_o, x2d, g1, sh2, sc2, nw2)


def _glu_kernel(h_ref, hp_ref, hn_ref, wa_ref, wg_ref, fw_ref, fb_ref, u_ref,
                hext_scr, a_even, g_even, a_odd, g_odd, *, n_chunks, n_steps):
    s = pl.program_id(0)
    tm = h_ref.shape[0]
    cur = jnp.minimum(s, n_steps - 2)
    i = cur // n_chunks

    @pl.when(s == 0)
    def _():
        a_odd[...] = jnp.zeros(a_odd.shape, F32)
        g_odd[...] = jnp.zeros(g_odd.shape, F32)

    @pl.when(cur % n_chunks == 0)
    def _():
        has_prev = ((i * tm) % SEQ) != 0
        has_next = (((i + 1) * tm) % SEQ) != 0
        hext_scr[0:GRID_W, :] = jnp.where(has_prev, hp_ref[...], jnp.zeros_like(hp_ref[...]))
        hext_scr[GRID_W:GRID_W + tm, :] = h_ref[...]
        hext_scr[GRID_W + tm:, :] = jnp.where(has_next, hn_ref[...], jnp.zeros_like(hn_ref[...]))

    def step(wr_a, wr_g, rd_a, rd_g):
        fw = fw_ref[...]
        fb = fb_ref[...]
        for r in range(0, tm, GLU_ROWS):
            cv = (rd_g[r:r + GLU_ROWS, :] * fw[0:1, :] + fb
                  + rd_g[r + GRID_W:r + GRID_W + GLU_ROWS, :] * fw[1:2, :]
                  + rd_g[r + 2 * GRID_W:r + 2 * GRID_W + GLU_ROWS, :] * fw[2:3, :])
            gelu = 0.5 * cv * (1.0 + lax.erf(cv * np.float32(np.sqrt(0.5))))
            u_ref[r:r + GLU_ROWS, :] = (gelu * rd_a[r:r + GLU_ROWS, :]).astype(u_ref.dtype)
        wr_a[...] = _dot(hext_scr[GRID_W:GRID_W + tm, :], wa_ref[...])
        wr_g[...] = _dot(hext_scr[...], wg_ref[...])

    @pl.when(s % 2 == 0)
    def _():
        step(a_even, g_even, a_odd, g_odd)

    @pl.when(s % 2 == 1)
    def _():
        step(a_odd, g_odd, a_even, g_even)


def _glu(h2, w_up, ffn_conv_w, ffn_conv_b, tm=1024, tn=512):
    n_chunks = D_FF // tn
    n_tiles = TOKENS // tm
    n_steps = n_tiles * n_chunks + 1
    hb = tm // GRID_W
    n_halo = TOKENS // GRID_W

    def cur(s):
        return jnp.minimum(s, n_steps - 2)

    def prev(s):
        return jnp.maximum(s - 1, 0)

    return pl.pallas_call(
        functools.partial(_glu_kernel, n_chunks=n_chunks, n_steps=n_steps),
        grid=(n_steps,),
        in_specs=[pl.BlockSpec((tm, D_MODEL), lambda s: (cur(s) // n_chunks, 0)),
                  pl.BlockSpec((GRID_W, D_MODEL),
                               lambda s: (jnp.maximum((cur(s) // n_chunks) * hb - 1, 0), 0)),
                  pl.BlockSpec((GRID_W, D_MODEL),
                               lambda s: (jnp.minimum((cur(s) // n_chunks + 1) * hb, n_halo - 1), 0)),
                  pl.BlockSpec((D_MODEL, tn), lambda s: (0, cur(s) % n_chunks)),
                  pl.BlockSpec((D_MODEL, tn), lambda s: (0, n_chunks + cur(s) % n_chunks)),
                  pl.BlockSpec((3, tn), lambda s: (0, prev(s) % n_chunks)),
                  pl.BlockSpec((1, tn), lambda s: (0, prev(s) % n_chunks))],
        out_specs=pl.BlockSpec((tm, tn), lambda s: (prev(s) // n_chunks, prev(s) % n_chunks)),
        out_shape=jax.ShapeDtypeStruct((TOKENS, D_FF), BF16),
        scratch_shapes=[pltpu.VMEM((tm + 2 * GRID_W, D_MODEL), BF16),
                        pltpu.VMEM((tm, tn), F32), pltpu.VMEM((tm + 2 * GRID_W, tn), F32),
                        pltpu.VMEM((tm, tn), F32), pltpu.VMEM((tm + 2 * GRID_W, tn), F32)],
        compiler_params=_params(("arbitrary",)),
        name="glu",
    )(h2, h2, h2, w_up, w_up, ffn_conv_w, ffn_conv_b)


def _down_kernel(u_ref, wd_ref, x1_ref, g2_ref, nw_ref, o_ref):
    x2 = x1_ref[...] + g2_ref[0] * _dot(u_ref[...], wd_ref[...])
    ms = jnp.mean(x2 * x2, axis=-1, keepdims=True)
    o_ref[...] = x2 * lax.rsqrt(ms + EPS) * nw_ref[...]


def _down(u, w_down, x1, g2, fnw, tm=512):
    tpb = SEQ // tm
    return pl.pallas_call(
        _down_kernel,
        grid=(TOKENS // tm,),
        in_specs=[pl.BlockSpec((tm, D_FF), lambda i: (i, 0)),
                  pl.BlockSpec((D_FF, D_MODEL), lambda i: (0, 0), pipeline_mode=pl.Buffered(1)),
                  pl.BlockSpec((tm, D_MODEL), lambda i: (i, 0)),
                  pl.BlockSpec((1, 1, D_MODEL), lambda i: (i // tpb, 0, 0)),
                  pl.BlockSpec((1, D_MODEL), lambda i: (0, 0))],
        out_specs=pl.BlockSpec((tm, D_MODEL), lambda i: (i, 0)),
        out_shape=jax.ShapeDtypeStruct((TOKENS, D_MODEL), F32),
        compiler_params=_params(("arbitrary",)),
        name="down",
    )(u, w_down, x1, g2, fnw)


def _dt_rows(dt_raw, batch, n_chunks):
    d = dt_raw[:, :2 * SSD_HEADS].reshape(batch, n_chunks, CHUNK, 2, SSD_GROUPS, SSD_REP)
    d = jnp.transpose(d, (4, 0, 1, 3, 5, 2))
    return d.reshape(SSD_GROUPS, batch, n_chunks, 2 * SSD_REP, CHUNK)


def _group_rows(p):
    p = p.reshape(2, SSD_GROUPS, SSD_REP)
    return jnp.transpose(p, (1, 0, 2)).reshape(SSD_GROUPS, 2 * SSD_REP, 1)


def kernel(x, c, ctx, c_ctx, w_ada, b_ada, norm1_w, w_in, ssd_conv_w, ssd_conv_b, dt_bias, a_log,
           d_skip, ssd_norm_w, w_ssd_out, pool_w, pool_scale, w_pool_out, w_o, norm2_w, w_up,
           ffn_conv_w, ffn_conv_b, w_down, final_norm_w):
    assert w_in.shape[0] == 1, "single-layer block only"
    d = D_MODEL
    x2d = x.reshape(TOKENS, d)
    ctx2d = ctx.reshape(BATCH * CTX_LEN, d)

    cs = jnp.zeros((8, d), F32).at[:BATCH].set(c).at[BATCH].set(c_ctx)
    mod = _ada(cs, w_ada[0], b_ada)
    sh1, sc1, g1, sh2, sc2, g2 = [mod[:BATCH, k * d:(k + 1) * d].reshape(BATCH, 1, d)
                                  for k in range(6)]
    csh1 = mod[BATCH:BATCH + 1, 0:d].reshape(1, 1, d)
    csc1 = mod[BATCH:BATCH + 1, d:2 * d].reshape(1, 1, d)

    w_in0 = w_in[0]
    n_dt = 2 * SSD_HEADS
    w_dt = jnp.pad(w_in0[:, :n_dt], ((0, 0), (0, LANES - n_dt))).astype(BF16)
    w_cat = w_in0[:, n_dt:].astype(BF16)
    w_s = w_ssd_out[0].astype(BF16)
    w_p = w_pool_out[0].astype(BF16)
    w_ob = w_o[0].astype(BF16)
    w_upb = w_up[0].astype(BF16)
    w_db = w_down[0].astype(BF16)
    pool_wb = pool_w[0].astype(BF16)

    proj, dt_raw = _inproj(x2d, sh1, sc1, norm1_w, w_cat, w_dt, n_cols=N_PROJ, rows_per_mod=SEQ)
    cproj, cdt_raw = _inproj(ctx2d, csh1, csc1, norm1_w, w_cat, w_dt, n_cols=N_CTX_PROJ,
                             rows_per_mod=BATCH * CTX_LEN)

    dtr = _dt_rows(dt_raw, BATCH, N_CHUNKS)
    cdtr = _dt_rows(cdt_raw, BATCH, N_CTX_CHUNKS)
    dsk = jnp.repeat(jnp.transpose(d_skip[0].reshape(2, SSD_GROUPS, SSD_REP), (1, 0, 2)),
                     SSD_HEAD_DIM, axis=-1)
    idx = np.arange(CHUNK)
    tri = jnp.asarray(np.stack([idx[:, None] <= idx[None, :], idx[:, None] >= idx[None, :]])
                      .astype(np.float32), BF16)

    y_ssd = _ssd(proj.reshape(BATCH, SEQ, N_PROJ), dtr,
                 cproj.reshape(BATCH, CTX_LEN, N_CTX_PROJ), cdtr,
                 ssd_conv_w[0], ssd_conv_b, _group_rows(dt_bias[0]), _group_rows(a_log[0]),
                 dsk, ssd_norm_w, tri)
    y_pool = _pool(proj, pool_wb, pool_scale)
    merged = _merge(y_ssd.reshape(TOKENS, d), y_pool, w_s, w_p, proj)
    x1, h2 = _outproj(merged, w_ob, x2d, g1, sh2, sc2, norm2_w)
    u = _glu(h2, w_upb, ffn_conv_w[0], ffn_conv_b)
    out = _down(u, w_db, x1, g2, final_norm_w.reshape(1, d))
    return out.reshape(BATCH, SEQ, d)
```

```python
import functools

import numpy as np
import jax
import jax.numpy as jnp
from jax import lax
from jax.experimental import pallas as pl
from jax.experimental.pallas import tpu as pltpu

F32 = jnp.float32
BF16 = jnp.bfloat16

D_MODEL = 2048
BATCH = 4
SEQ = 4096
TOKENS = BATCH * SEQ
GRID_W = 64
CTX_LEN = 256

SSD_HEAD_DIM = 64
SSD_HEADS = 32
SSD_GROUPS = 8
SSD_REP = 4
SSD_STATE = 128
CHUNK = 128
GROUP_W = SSD_REP * SSD_HEAD_DIM
N_CHUNKS = SEQ // CHUNK
N_CTX_CHUNKS = CTX_LEN // CHUNK
SCAN_UNROLL = 2
GLU_ROWS = 64

POOL_WINDOWS = (2, 4, 8, 16)
POOL_GW = 512
D_FF = 5632
EPS = 1e-6

COL_X = 0
COL_B = 2048
COL_C = 3072
COL_Z = 4096
COL_V = 6144
COL_GS = 8192
COL_GP = 10240
N_PROJ = 12288
N_CTX_PROJ = 3072

LANES = 128
VMEM_LIMIT = 56 * 1024 * 1024


def _dot(a, b):
    return jnp.dot(a, b, preferred_element_type=F32)


def _silu(x):
    return x * jax.nn.sigmoid(x)


def _softplus(x):
    return jnp.maximum(x, 0.0) + jnp.log1p(jnp.exp(-jnp.abs(x)))


def _params(sem):
    return pltpu.CompilerParams(dimension_semantics=sem, vmem_limit_bytes=VMEM_LIMIT)


def _ada_kernel(c_ref, w_ref, b_ref, o_ref):
    s = _silu(c_ref[...])
    o_ref[...] = jnp.dot(s, w_ref[...], precision=lax.Precision.HIGHEST,
                         preferred_element_type=F32) + b_ref[...]


def _ada(cs, w, b):
    tn = 1024
    n = w.shape[1]
    return pl.pallas_call(
        _ada_kernel,
        grid=(n // tn,),
        in_specs=[pl.BlockSpec((8, D_MODEL), lambda j: (0, 0)),
                  pl.BlockSpec((D_MODEL, tn), lambda j: (0, j)),
                  pl.BlockSpec((1, tn), lambda j: (0, j))],
        out_specs=pl.BlockSpec((8, tn), lambda j: (0, j)),
        out_shape=jax.ShapeDtypeStruct((8, n), F32),
        compiler_params=_params(("arbitrary",)),
        name="ada",
    )(cs, w, b)


def _modulate_rows(x_ref, nw, sc, sh, out_ref, rows, rc=256):
    def body(k, carry):
        r = pl.multiple_of(k * rc, rc)
        x = x_ref[pl.ds(r, rc), :]
        ms = jnp.mean(x * x, axis=-1, keepdims=True)
        y = x * lax.rsqrt(ms + EPS) * nw
        out_ref[pl.ds(r, rc), :] = (y * (1.0 + sc) + sh).astype(out_ref.dtype)
        return carry
    lax.fori_loop(0, rows // rc, body, 0)


def _inproj_kernel(x_ref, sh_ref, sc_ref, nw_ref, w_ref, wdt_ref, o_ref, dt_ref, h_scr):
    @pl.when(pl.program_id(1) == 0)
    def _():
        _modulate_rows(x_ref, nw_ref[...], sc_ref[0], sh_ref[0], h_scr, h_scr.shape[0])
        dt_ref[...] = _dot(h_scr[...], wdt_ref[...].astype(BF16))

    o_ref[...] = _dot(h_scr[...], w_ref[...]).astype(o_ref.dtype)


def _inproj(x2d, sh, sc, nw, w_cat, w_dt, *, n_cols, rows_per_mod, tm=1024, tn=1024):
    m = x2d.shape[0]
    tpb = rows_per_mod // tm
    return pl.pallas_call(
        _inproj_kernel,
        grid=(m // tm, n_cols // tn),
        in_specs=[pl.BlockSpec((tm, D_MODEL), lambda i, j: (i, 0)),
                  pl.BlockSpec((1, 1, D_MODEL), lambda i, j: (i // tpb, 0, 0)),
                  pl.BlockSpec((1, 1, D_MODEL), lambda i, j: (i // tpb, 0, 0)),
                  pl.BlockSpec((1, D_MODEL), lambda i, j: (0, 0)),
                  pl.BlockSpec((D_MODEL, tn), lambda i, j: (0, j)),
                  pl.BlockSpec((D_MODEL, LANES), lambda i, j: (0, 0))],
        out_specs=[pl.BlockSpec((tm, tn), lambda i, j: (i, j)),
                   pl.BlockSpec((tm, LANES), lambda i, j: (i, 0))],
        out_shape=[jax.ShapeDtypeStruct((m, n_cols), BF16),
                   jax.ShapeDtypeStruct((m, LANES), F32)],
        scratch_shapes=[pltpu.VMEM((tm, D_MODEL), BF16)],
        compiler_params=_params(("arbitrary", "arbitrary")),
        name="inproj",
    )(x2d, sh, sc, nw, w_cat, w_dt)


def _conv_silu_tile(ref, c, n_chunks, w, bias):
    length = n_chunks * CHUNK
    if isinstance(c, int):
        r0, p0, n0 = c * CHUNK, max(c * CHUNK - 16, 0), min(c * CHUNK + CHUNK, length - 16)
    else:
        r0 = pl.multiple_of(c * CHUNK, CHUNK)
        p0 = pl.multiple_of(jnp.maximum(c * CHUNK - 16, 0), 16)
        n0 = pl.multiple_of(jnp.minimum(c * CHUNK + CHUNK, length - 16), 16)
    x = ref[0, pl.ds(r0, CHUNK), :].astype(F32)
    prev_row = ref[0, pl.ds(p0, 16), :].astype(F32)[15:16, :]
    prev_row = jnp.where(c > 0, prev_row, 0.0)
    next_row = ref[0, pl.ds(n0, 16), :].astype(F32)[0:1, :]
    next_row = jnp.where(c < n_chunks - 1, next_row, 0.0)
    rows = lax.broadcasted_iota(jnp.int32, x.shape, 0)
    xm = jnp.where(rows == 0, prev_row, pltpu.roll(x, 1, axis=0))
    xp = jnp.where(rows == CHUNK - 1, next_row, pltpu.roll(x, CHUNK - 1, axis=0))
    y = xm * w[0:1, :] + x * w[1:2, :] + xp * w[2:3, :] + bias
    return _silu(y)


def _decay_rows(dt_raw, dt_bias, a_neg, tri, causal):
    dt = _softplus(dt_raw + dt_bias)
    a = dt * a_neg
    hi = a.astype(BF16)
    lo = (a - hi.astype(F32)).astype(BF16)
    r = _dot(jnp.concatenate([hi, lo], axis=0), tri)
    acs = r[0:8] + r[8:16]
    tot = acs[:, CHUNK - 1:CHUNK] if causal else acs[:, 0:1]
    return acs, dt, jnp.exp(tot - acs) * dt, jnp.exp(tot)


def _split_bf16(x, parts):
    out = []
    for _ in range(parts - 1):
        hi = x.astype(BF16).astype(F32)
        out.append(hi)
        x = x - hi
    out.append(x.astype(BF16).astype(F32))
    return out


TAB_ACS, TAB_ONE, TAB_EXP, TAB_W, TAB_DEC, TAB_USED = 0, 3, 6, 8, 10, 12
SEG_ROWS = 48


def _decay_tables(dtr_ref, dt_bias, a_neg, tri_ref, a_scr, bsel_scr, dt_scr):
    n = N_CHUNKS * 8
    dt = _softplus(dtr_ref[0, 0] + dt_bias)
    a = (dt * a_neg).reshape(n, CHUNK)
    dt = dt.reshape(n, CHUNK)
    hi, lo = _split_bf16(a, 2)
    lhs = jnp.concatenate([hi, lo], axis=0).astype(BF16)
    pre = _dot(lhs, tri_ref[0])
    suf = _dot(lhs, tri_ref[1])
    row = lax.broadcasted_iota(jnp.int32, (n, CHUNK), 0)
    fwd = jnp.bitwise_and(row, 7) < SSD_REP
    acs = jnp.where(fwd, pre[:n] + pre[n:], suf[:n] + suf[n:])
    tot = jnp.where(fwd[:, 0:1], acs[:, CHUNK - 1:CHUNK], acs[:, 0:1])
    ones = jnp.ones((n, CHUNK), F32)
    comps_a = (_split_bf16(acs, 3) + [ones, ones, ones] + _split_bf16(jnp.exp(acs), 2)
               + _split_bf16(jnp.exp(tot - acs) * dt, 2)
               + _split_bf16(jnp.broadcast_to(jnp.exp(tot), acs.shape), 2))
    comps_b = [ones, ones, ones] + [-p for p in _split_bf16(acs, 3)]
    row48 = jnp.bitwise_and(lax.broadcasted_iota(jnp.int32, (SEG_ROWS, CHUNK), 0), 7)
    pad = jnp.zeros((CHUNK - 8 * TAB_USED, CHUNK), F32)
    for c in range(N_CHUNKS):
        sl = slice(c * 8, c * 8 + 8)
        a_scr[c] = jnp.concatenate([p[sl] for p in comps_a] + [pad], axis=0).T.astype(BF16)
        b_all = jnp.concatenate([p[sl] for p in comps_b], axis=0)
        for k in range(8):
            bsel_scr[c, :, k * CHUNK:(k + 1) * CHUNK] = jnp.where(row48 == k, b_all, 0.0).astype(BF16)
        dt_scr[c] = dt[sl]


def _head_lanes(dec, off):
    lane = lax.broadcasted_iota(jnp.int32, (1, LANES), 1)
    t0 = jnp.where(lane < SSD_HEAD_DIM, dec[off:off + 1, :], dec[off + 1:off + 2, :])
    t1 = jnp.where(lane < SSD_HEAD_DIM, dec[off + 2:off + 3, :], dec[off + 3:off + 4, :])
    return jnp.concatenate([t0, t1], axis=1)


def _split_heads(t):
    lane = lax.broadcasted_iota(jnp.int32, t.shape, 1)
    return (jnp.where(lane < SSD_HEAD_DIM, t, 0.0).astype(BF16),
            jnp.where(lane >= SSD_HEAD_DIM, t, 0.0).astype(BF16))


def _state_update(x_split, bt, w, dec, off, h_ref):
    h = h_ref[...]
    ss = []
    for pair in range(2):
        bw = [(bt * w[off + r:off + r + 1, :]).astype(BF16) for r in (2 * pair, 2 * pair + 1)]
        ss.append(_dot(jnp.concatenate(bw, axis=1), jnp.concatenate(x_split[pair], axis=0)))
    h_ref[...] = h * _head_lanes(dec, off) + jnp.concatenate(ss, axis=1)
    return h


def _chunk_dir(xs, x_pairs, bt, cc, cb, tab, bsel, dt, sel, off, h_ref, causal):
    l_io = lax.broadcasted_iota(jnp.int32, (CHUNK, CHUNK), 0)
    s_io = lax.broadcasted_iota(jnp.int32, (CHUNK, CHUNK), 1)
    mask = (l_io >= s_io) if causal else (l_io <= s_io)
    seg = _dot(tab, jnp.concatenate([bsel, jnp.zeros((CHUNK - SEG_ROWS, bsel.shape[1]), BF16)],
                                    axis=0))
    spread = _dot(tab, sel)
    e_exp, w_exp, dec = spread[:, 0:GROUP_W], spread[:, GROUP_W:2 * GROUP_W], spread[0:1, 2 * GROUP_W:]
    h = h_ref[...]
    h_ref[...] = h * dec + _dot(bt, (xs * w_exp).astype(BF16))
    y = _dot(cc, h.astype(BF16)) * e_exp
    ys = []
    for pair in range(2):
        ms = []
        for r in (2 * pair, 2 * pair + 1):
            decay = jnp.exp(jnp.where(mask, seg[:, r * CHUNK:(r + 1) * CHUNK], -jnp.inf))
            ms.append((cb * decay * dt[off + r:off + r + 1, :]).astype(BF16))
        ys.append(_dot(jnp.concatenate(ms, axis=1), x_pairs[pair]))
    return y + jnp.concatenate(ys, axis=1)


def _ssd_kernel(xs_ref, b_ref, c_ref, z_ref, dtr_ref, cxs_ref, cbm_ref, cdtr_ref,
                cwx_ref, cwb_ref, cwc_ref, bx_ref, bb_ref, bc_ref,
                dtb_ref, alog_ref, dsk_ref, nw_ref, tri_ref, sel_ref,
                y_ref,
                xs_scr, xp_scr, bt_scr, cc_scr, cb_scr, y_scr, hf_scr, hb_scr,
                tab_scr, bsel_scr, dt_scr):
    dt_bias = dtb_ref[0]
    a_neg = -jnp.exp(alog_ref[0])
    cwx, cwb, cwc = cwx_ref[...], cwb_ref[...], cwc_ref[...]
    bx, bb, bc = bx_ref[...], bb_ref[...], bc_ref[...]

    hf_scr[...] = jnp.zeros(hf_scr.shape, F32)
    hb_scr[...] = jnp.zeros(hb_scr.shape, F32)
    ctx_tiles = []
    for c in range(N_CTX_CHUNKS):
        xs = _conv_silu_tile(cxs_ref, c, N_CTX_CHUNKS, cwx, bx)
        bt = _conv_silu_tile(cbm_ref, c, N_CTX_CHUNKS, cwb, bb).T
        ctx_tiles.append(([_split_heads(xs[:, p * LANES:(p + 1) * LANES]) for p in range(2)], bt))
    for c in range(N_CTX_CHUNKS):
        _, _, w, dec = _decay_rows(cdtr_ref[0, 0, c], dt_bias, a_neg, tri_ref[0], True)
        _state_update(ctx_tiles[c][0], ctx_tiles[c][1], w, dec, 0, hf_scr)
    for c in reversed(range(N_CTX_CHUNKS)):
        _, _, w, dec = _decay_rows(cdtr_ref[0, 0, c], dt_bias, a_neg, tri_ref[1], False)
        _state_update(ctx_tiles[c][0], ctx_tiles[c][1], w, dec, 4, hb_scr)

    _decay_tables(dtr_ref, dt_bias, a_neg, tri_ref, tab_scr, bsel_scr, dt_scr)

    def prepare(c):
        r0 = pl.multiple_of(c * CHUNK, CHUNK)
        xs = _conv_silu_tile(xs_ref, c, N_CHUNKS, cwx, bx)
        bt = _conv_silu_tile(b_ref, c, N_CHUNKS, cwb, bb).T.astype(BF16)
        cc = _conv_silu_tile(c_ref, c, N_CHUNKS, cwc, bc).astype(BF16)
        cb = _dot(cc, bt)
        x_pairs = [jnp.concatenate(_split_heads(xs[:, p * LANES:(p + 1) * LANES]), axis=0)
                   for p in range(2)]
        xs_scr[pl.ds(r0, CHUNK), :] = xs
        xp_scr[c, 0] = x_pairs[0]
        xp_scr[c, 1] = x_pairs[1]
        bt_scr[c] = bt
        cc_scr[c] = cc
        cb_scr[c] = cb
        return xs, x_pairs, bt, cc, cb

    def recall(c):
        r0 = pl.multiple_of(c * CHUNK, CHUNK)
        return (xs_scr[pl.ds(r0, CHUNK), :], [xp_scr[c, 0], xp_scr[c, 1]], bt_scr[c], cc_scr[c],
                cb_scr[c])

    def scan(c, tiles, causal):
        xs, x_pairs, bt, cc, cb = tiles
        d = 0 if causal else 1
        bsel = bsel_scr[c, :, d * SSD_REP * CHUNK:(d + 1) * SSD_REP * CHUNK]
        return _chunk_dir(xs, x_pairs, bt, cc, cb, tab_scr[c], bsel, dt_scr[c], sel_ref[d],
                          d * SSD_REP, hf_scr if causal else hb_scr, causal)

    d_sum = dsk_ref[0, 0:1, :] + dsk_ref[0, 1:2, :]
    nw = nw_ref[...]

    def finish(c, y_dir, xs):
        r0 = pl.multiple_of(c * CHUNK, CHUNK)
        y = y_scr[pl.ds(r0, CHUNK), :] + y_dir + xs * d_sum
        gated = y * _silu(z_ref[0, pl.ds(r0, CHUNK), :].astype(F32))
        ms = jnp.mean(gated * gated, axis=-1, keepdims=True)
        y_ref[0, pl.ds(r0, CHUNK), :] = (gated * lax.rsqrt(ms + EPS) * nw).astype(y_ref.dtype)

    def visits(i):
        fwd = [i * SCAN_UNROLL + u for u in range(SCAN_UNROLL)]
        return [(c, True) for c in fwd] + [(N_CHUNKS - 1 - c, False) for c in fwd]

    def first_half(i, carry):
        for c, causal in visits(i):
            r0 = pl.multiple_of(c * CHUNK, CHUNK)
            y_scr[pl.ds(r0, CHUNK), :] = scan(c, prepare(c), causal)
        return carry

    def second_half(i, carry):
        for c, causal in visits(i):
            tiles = recall(c)
            finish(c, scan(c, tiles, causal), tiles[0])
        return carry

    half = N_CHUNKS // 2 // SCAN_UNROLL
    lax.fori_loop(0, half, first_half, 0)
    lax.fori_loop(half, 2 * half, second_half, 0)


def _ssd(proj, dtr, cproj, cdtr, conv_w, conv_b, dtb, alog, dsk, norm_w, tri):
    gx = COL_X // GROUP_W
    gz = COL_Z // GROUP_W
    gb = COL_B // SSD_STATE
    gc = COL_C // SSD_STATE
    in_specs = [
        pl.BlockSpec((1, SEQ, GROUP_W), lambda b, g: (b, 0, gx + g)),
        pl.BlockSpec((1, SEQ, SSD_STATE), lambda b, g: (b, 0, gb + g)),
        pl.BlockSpec((1, SEQ, SSD_STATE), lambda b, g: (b, 0, gc + g)),
        pl.BlockSpec((1, SEQ, GROUP_W), lambda b, g: (b, 0, gz + g)),
        pl.BlockSpec((1, 1, N_CHUNKS, 8, CHUNK), lambda b, g: (g, b, 0, 0, 0)),
        pl.BlockSpec((1, CTX_LEN, GROUP_W), lambda b, g: (b, 0, gx + g)),
        pl.BlockSpec((1, CTX_LEN, SSD_STATE), lambda b, g: (b, 0, gb + g)),
        pl.BlockSpec((1, 1, N_CTX_CHUNKS, 8, CHUNK), lambda b, g: (g, b, 0, 0, 0)),
        pl.BlockSpec((3, GROUP_W), lambda b, g: (0, gx + g)),
        pl.BlockSpec((3, SSD_STATE), lambda b, g: (0, gb + g)),
        pl.BlockSpec((3, SSD_STATE), lambda b, g: (0, gc + g)),
        pl.BlockSpec((1, GROUP_W), lambda b, g: (0, gx + g)),
        pl.BlockSpec((1, SSD_STATE), lambda b, g: (0, gb + g)),
        pl.BlockSpec((1, SSD_STATE), lambda b, g: (0, gc + g)),
        pl.BlockSpec((1, 8, 1), lambda b, g: (g, 0, 0)),
        pl.BlockSpec((1, 8, 1), lambda b, g: (g, 0, 0)),
        pl.BlockSpec((1, 2, GROUP_W), lambda b, g: (g, 0, 0)),
        pl.BlockSpec((1, GROUP_W), lambda b, g: (0, g)),
        pl.BlockSpec((2, CHUNK, CHUNK), lambda b, g: (0, 0, 0)),
        pl.BlockSpec((2, CHUNK, 3 * GROUP_W), lambda b, g: (0, 0, 0)),
    ]
    return pl.pallas_call(
        _ssd_kernel,
        grid=(BATCH, SSD_GROUPS),
        in_specs=in_specs,
        out_specs=pl.BlockSpec((1, SEQ, GROUP_W), lambda b, g: (b, 0, g)),
        out_shape=jax.ShapeDtypeStruct((BATCH, SEQ, D_MODEL), BF16),
        scratch_shapes=[pltpu.VMEM((SEQ, GROUP_W), F32),
                        pltpu.VMEM((N_CHUNKS, 2, 2 * CHUNK, LANES), BF16),
                        pltpu.VMEM((N_CHUNKS, CHUNK, CHUNK), BF16),
                        pltpu.VMEM((N_CHUNKS, CHUNK, CHUNK), BF16),
                        pltpu.VMEM((N_CHUNKS, CHUNK, CHUNK), F32),
                        pltpu.VMEM((SEQ, GROUP_W), F32),
                        pltpu.VMEM((SSD_STATE, GROUP_W), F32),
                        pltpu.VMEM((SSD_STATE, GROUP_W), F32),
                        pltpu.VMEM((N_CHUNKS, CHUNK, CHUNK), BF16),
                        pltpu.VMEM((N_CHUNKS, SEG_ROWS, 8 * CHUNK), BF16),
                        pltpu.VMEM((N_CHUNKS, 8, CHUNK), F32)],
        compiler_params=_params(("arbitrary", "arbitrary")),
        name="ssd",
    )(proj, proj, proj, proj, dtr, cproj, cproj, cdtr,
      conv_w, conv_w, conv_w, conv_b, conv_b, conv_b, dtb, alog, dsk, norm_w, tri, _spread_matrix())


def _spread_matrix():
    sel = np.zeros((2, CHUNK, 3 * GROUP_W), np.float32)
    for d in range(2):
        for grp, comp in enumerate((TAB_EXP, TAB_W, TAB_DEC)):
            for part in range(2):
                for h in range(SSD_REP):
                    lanes = slice(grp * GROUP_W + h * SSD_HEAD_DIM,
                                  grp * GROUP_W + (h + 1) * SSD_HEAD_DIM)
                    sel[d, (comp + part) * 8 + d * SSD_REP + h, lanes] = 1.0
    return jnp.asarray(sel, BF16)


def _pool_kernel(v_ref, win_ref, inv_ref, pw_ref, ps_ref, o_ref, p_scr):
    slab = win_ref.shape[1]
    for r in range(0, v_ref.shape[0], slab):
        v = v_ref[r:r + slab, :]
        p_scr[r:r + slab, :] = (_dot(win_ref[0], v) * inv_ref[0] - v.astype(F32)).astype(BF16)
    o_ref[...] = (_dot(p_scr[...], pw_ref[0]) * ps_ref[...]).astype(o_ref.dtype)


def _pool_constants(tm):
    t = np.arange(tm)
    col = t % GRID_W
    wins, invs = [], []
    for w in POOL_WINDOWS:
        start = np.clip(col - w // 2, 0, GRID_W)
        end = np.clip(col + w - w // 2, 0, GRID_W)
        same_row = (t[:, None] // GRID_W) == (t[None, :] // GRID_W)
        inside = (col[None, :] >= start[:, None]) & (col[None, :] < end[:, None])
        wins.append((same_row & inside).astype(np.float32))
        invs.append((1.0 / (end - start).astype(np.float32))[:, None])
    return jnp.asarray(np.stack(wins), BF16), jnp.asarray(np.stack(invs), F32)


def _pool(proj2d, pool_w, pool_scale, tm=2048, slab=128):
    win, inv = _pool_constants(slab)
    gv = COL_V // POOL_GW
    return pl.pallas_call(
        _pool_kernel,
        grid=(TOKENS // tm, len(POOL_WINDOWS)),
        in_specs=[pl.BlockSpec((tm, POOL_GW), lambda i, k: (i, gv + k)),
                  pl.BlockSpec((1, slab, slab), lambda i, k: (k, 0, 0)),
                  pl.BlockSpec((1, slab, 1), lambda i, k: (k, 0, 0)),
                  pl.BlockSpec((1, POOL_GW, POOL_GW), lambda i, k: (k, 0, 0)),
                  pl.BlockSpec((1, POOL_GW), lambda i, k: (0, k))],
        out_specs=pl.BlockSpec((tm, POOL_GW), lambda i, k: (i, k)),
        out_shape=jax.ShapeDtypeStruct((TOKENS, D_MODEL), BF16),
        scratch_shapes=[pltpu.VMEM((tm, POOL_GW), BF16)],
        compiler_params=_params(("arbitrary", "arbitrary")),
        name="pool",
    )(proj2d, win, inv, pool_w, pool_scale)


def _merge_kernel(ys_ref, yp_ref, ws_ref, wp_ref, gs_ref, gp_ref, o_ref):
    a = jax.nn.sigmoid(gs_ref[...].astype(F32)) * _dot(ys_ref[...], ws_ref[...])
    b = jax.nn.sigmoid(gp_ref[...].astype(F32)) * _dot(yp_ref[...], wp_ref[...])
    o_ref[...] = (a + b).astype(o_ref.dtype)


def _merge(y_ssd, y_pool, w_s, w_p, proj2d, tm=1024, tn=512):
    ggs = COL_GS // tn
    ggp = COL_GP // tn
    return pl.pallas_call(
        _merge_kernel,
        grid=(TOKENS // tm, D_MODEL // tn),
        in_specs=[pl.BlockSpec((tm, D_MODEL), lambda i, j: (i, 0)),
                  pl.BlockSpec((tm, D_MODEL), lambda i, j: (i, 0)),
                  pl.BlockSpec((D_MODEL, tn), lambda i, j: (0, j)),
                  pl.BlockSpec((D_MODEL, tn), lambda i, j: (0, j)),
                  pl.BlockSpec((tm, tn), lambda i, j: (i, ggs + j)),
                  pl.BlockSpec((tm, tn), lambda i, j: (i, ggp + j))],
        out_specs=pl.BlockSpec((tm, tn), lambda i, j: (i, j)),
        out_shape=jax.ShapeDtypeStruct((TOKENS, D_MODEL), BF16),
        compiler_params=_params(("arbitrary", "arbitrary")),
        name="merge",
    )(y_ssd, y_pool, w_s, w_p, proj2d, proj2d)


def _outproj_kernel(m_ref, wo_ref, x_ref, g1_ref, sh_ref, sc_ref, nw_ref, x1_ref, h2_ref):
    x1_ref[...] = x_ref[...] + g1_ref[0] * _dot(m_ref[...], wo_ref[...])
    _modulate_rows(x1_ref, nw_ref[...], sc_ref[0], sh_ref[0], h2_ref, x1_ref.shape[0], rc=128)


def _outproj(merged, w_o, x2d, g1, sh2, sc2, nw2, tm=256):
    tpb = SEQ // tm
    mod_spec = pl.BlockSpec((1, 1, D_MODEL), lambda i: (i // tpb, 0, 0))
    return pl.pallas_call(
        _outproj_kernel,
        grid=(TOKENS // tm,),
        in_specs=[pl.BlockSpec((tm, D_MODEL), lambda i: (i, 0)),
                  pl.BlockSpec((D_MODEL, D_MODEL), lambda i: (0, 0)),
                  pl.BlockSpec((tm, D_MODEL), lambda i: (i, 0)),
                  mod_spec, mod_spec, mod_spec,
                  pl.BlockSpec((1, D_MODEL), lambda i: (0, 0))],
        out_specs=[pl.BlockSpec((tm, D_MODEL), lambda i: (i, 0)),
                   pl.BlockSpec((tm, D_MODEL), lambda i: (i, 0))],
        out_shape=[jax.ShapeDtypeStruct((TOKENS, D_MODEL), F32),
                   jax.ShapeDtypeStruct((TOKENS, D_MODEL), BF16)],
        compiler_params=_params(("arbitrary",)),
        name="outproj",
    )(merged, w_o, x2d, g1, sh2, sc2, nw2)


def _glu_kernel(h_ref, hp_ref, hn_ref, wa_ref, wg_ref, fw_ref, fb_ref, u_ref,
                hext_scr, a_even, g_even, a_odd, g_odd, *, n_chunks, n_steps):
    s = pl.program_id(0)
    tm = h_ref.shape[0]
    cur = jnp.minimum(s, n_steps - 2)
    i = cur // n_chunks

    @pl.when(s == 0)
    def _():
        a_odd[...] = jnp.zeros(a_odd.shape, F32)
        g_odd[...] = jnp.zeros(g_odd.shape, F32)

    @pl.when(cur % n_chunks == 0)
    def _():
        has_prev = ((i * tm) % SEQ) != 0
        has_next = (((i + 1) * tm) % SEQ) != 0
        hext_scr[0:GRID_W, :] = jnp.where(has_prev, hp_ref[...], jnp.zeros_like(hp_ref[...]))
        hext_scr[GRID_W:GRID_W + tm, :] = h_ref[...]
        hext_scr[GRID_W + tm:, :] = jnp.where(has_next, hn_ref[...], jnp.zeros_like(hn_ref[...]))

    def step(wr_a, wr_g, rd_a, rd_g):
        fw = fw_ref[...]
        fb = fb_ref[...]
        for r in range(0, tm, GLU_ROWS):
            cv = (rd_g[r:r + GLU_ROWS, :] * fw[0:1, :] + fb
                  + rd_g[r + GRID_W:r + GRID_W + GLU_ROWS, :] * fw[1:2, :]
                  + rd_g[r + 2 * GRID_W:r + 2 * GRID_W + GLU_ROWS, :] * fw[2:3, :])
            gelu = 0.5 * cv * (1.0 + lax.erf(cv * np.float32(np.sqrt(0.5))))
            u_ref[r:r + GLU_ROWS, :] = (gelu * rd_a[r:r + GLU_ROWS, :]).astype(u_ref.dtype)
        wr_a[...] = _dot(hext_scr[GRID_W:GRID_W + tm, :], wa_ref[...])
        wr_g[...] = _dot(hext_scr[...], wg_ref[...])

    @pl.when(s % 2 == 0)
    def _():
        step(a_even, g_even, a_odd, g_odd)

    @pl.when(s % 2 == 1)
    def _():
        step(a_odd, g_odd, a_even, g_even)


def _glu(h2, w_up, ffn_conv_w, ffn_conv_b, tm=1024, tn=512):
    n_chunks = D_FF // tn
    n_tiles = TOKENS // tm
    n_steps = n_tiles * n_chunks + 1
    hb = tm // GRID_W
    n_halo = TOKENS // GRID_W

    def cur(s):
        return jnp.minimum(s, n_steps - 2)

    def prev(s):
        return jnp.maximum(s - 1, 0)

    return pl.pallas_call(
        functools.partial(_glu_kernel, n_chunks=n_chunks, n_steps=n_steps),
        grid=(n_steps,),
        in_specs=[pl.BlockSpec((tm, D_MODEL), lambda s: (cur(s) // n_chunks, 0)),
                  pl.BlockSpec((GRID_W, D_MODEL),
                               lambda s: (jnp.maximum((cur(s) // n_chunks) * hb - 1, 0), 0)),
                  pl.BlockSpec((GRID_W, D_MODEL),
                               lambda s: (jnp.minimum((cur(s) // n_chunks + 1) * hb, n_halo - 1), 0)),
                  pl.BlockSpec((D_MODEL, tn), lambda s: (0, cur(s) % n_chunks)),
                  pl.BlockSpec((D_MODEL, tn), lambda s: (0, n_chunks + cur(s) % n_chunks)),
                  pl.BlockSpec((3, tn), lambda s: (0, prev(s) % n_chunks)),
                  pl.BlockSpec((1, tn), lambda s: (0, prev(s) % n_chunks))],
        out_specs=pl.BlockSpec((tm, tn), lambda s: (prev(s) // n_chunks, prev(s) % n_chunks)),
        out_shape=jax.ShapeDtypeStruct((TOKENS, D_FF), BF16),
        scratch_shapes=[pltpu.VMEM((tm + 2 * GRID_W, D_MODEL), BF16),
                        pltpu.VMEM((tm, tn), F32), pltpu.VMEM((tm + 2 * GRID_W, tn), F32),
                        pltpu.VMEM((tm, tn), F32), pltpu.VMEM((tm + 2 * GRID_W, tn), F32)],
        compiler_params=_params(("arbitrary",)),
        name="glu",
    )(h2, h2, h2, w_up, w_up, ffn_conv_w, ffn_conv_b)


def _down_kernel(u_ref, wd_ref, x1_ref, g2_ref, nw_ref, o_ref):
    x2 = x1_ref[...] + g2_ref[0] * _dot(u_ref[...], wd_ref[...])
    ms = jnp.mean(x2 * x2, axis=-1, keepdims=True)
    o_ref[...] = x2 * lax.rsqrt(ms + EPS) * nw_ref[...]


def _down(u, w_down, x1, g2, fnw, tm=512):
    tpb = SEQ // tm
    return pl.pallas_call(
        _down_kernel,
        grid=(TOKENS // tm,),
        in_specs=[pl.BlockSpec((tm, D_FF), lambda i: (i, 0)),
                  pl.BlockSpec((D_FF, D_MODEL), lambda i: (0, 0), pipeline_mode=pl.Buffered(1)),
                  pl.BlockSpec((tm, D_MODEL), lambda i: (i, 0)),
                  pl.BlockSpec((1, 1, D_MODEL), lambda i: (i // tpb, 0, 0)),
                  pl.BlockSpec((1, D_MODEL), lambda i: (0, 0))],
        out_specs=pl.BlockSpec((tm, D_MODEL), lambda i: (i, 0)),
        out_shape=jax.ShapeDtypeStruct((TOKENS, D_MODEL), F32),
        compiler_params=_params(("arbitrary",)),
        name="down",
    )(u, w_down, x1, g2, fnw)


def _dt_rows(dt_raw, batch, n_chunks):
    d = dt_raw[:, :2 * SSD_HEADS].reshape(batch, n_chunks, CHUNK, 2, SSD_GROUPS, SSD_REP)
    d = jnp.transpose(d, (4, 0, 1, 3, 5, 2))
    return d.reshape(SSD_GROUPS, batch, n_chunks, 2 * SSD_REP, CHUNK)


def _group_rows(p):
    p = p.reshape(2, SSD_GROUPS, SSD_REP)
    return jnp.transpose(p, (1, 0, 2)).reshape(SSD_GROUPS, 2 * SSD_REP, 1)


def kernel(x, c, ctx, c_ctx, w_ada, b_ada, norm1_w, w_in, ssd_conv_w, ssd_conv_b, dt_bias, a_log,
           d_skip, ssd_norm_w, w_ssd_out, pool_w, pool_scale, w_pool_out, w_o, norm2_w, w_up,
           ffn_conv_w, ffn_conv_b, w_down, final_norm_w):
    assert w_in.shape[0] == 1, "single-layer block only"
    d = D_MODEL
    x2d = x.reshape(TOKENS, d)
    ctx2d = ctx.reshape(BATCH * CTX_LEN, d)

    cs = jnp.zeros((8, d), F32).at[:BATCH].set(c).at[BATCH].set(c_ctx)
    mod = _ada(cs, w_ada[0], b_ada)
    sh1, sc1, g1, sh2, sc2, g2 = [mod[:BATCH, k * d:(k + 1) * d].reshape(BATCH, 1, d)
                                  for k in range(6)]
    csh1 = mod[BATCH:BATCH + 1, 0:d].reshape(1, 1, d)
    csc1 = mod[BATCH:BATCH + 1, d:2 * d].reshape(1, 1, d)

    w_in0 = w_in[0]
    n_dt = 2 * SSD_HEADS
    w_dt = jnp.pad(w_in0[:, :n_dt], ((0, 0), (0, LANES - n_dt)))
    w_cat = w_in0[:, n_dt:].astype(BF16)
    w_s = w_ssd_out[0].astype(BF16)
    w_p = w_pool_out[0].astype(BF16)
    w_ob = w_o[0].astype(BF16)
    w_upb = w_up[0].astype(BF16)
    w_db = w_down[0].astype(BF16)
    pool_wb = pool_w[0].astype(BF16)

    proj, dt_raw = _inproj(x2d, sh1, sc1, norm1_w, w_cat, w_dt, n_cols=N_PROJ, rows_per_mod=SEQ)
    cproj, cdt_raw = _inproj(ctx2d, csh1, csc1, norm1_w, w_cat, w_dt, n_cols=N_CTX_PROJ,
                             rows_per_mod=BATCH * CTX_LEN)

    dtr = _dt_rows(dt_raw, BATCH, N_CHUNKS)
    cdtr = _dt_rows(cdt_raw, BATCH, N_CTX_CHUNKS)
    dsk = jnp.repeat(jnp.transpose(d_skip[0].reshape(2, SSD_GROUPS, SSD_REP), (1, 0, 2)),
                     SSD_HEAD_DIM, axis=-1)
    idx = np.arange(CHUNK)
    tri = jnp.asarray(np.stack([idx[:, None] <= idx[None, :], idx[:, None] >= idx[None, :]])
                      .astype(np.float32), BF16)

    y_ssd = _ssd(proj.reshape(BATCH, SEQ, N_PROJ), dtr,
                 cproj.reshape(BATCH, CTX_LEN, N_CTX_PROJ), cdtr,
                 ssd_conv_w[0], ssd_conv_b, _group_rows(dt_bias[0]), _group_rows(a_log[0]),
                 dsk, ssd_norm_w, tri)
    y_pool = _pool(proj, pool_wb, pool_scale)
    merged = _merge(y_ssd.reshape(TOKENS, d), y_pool, w_s, w_p, proj)
    x1, h2 = _outproj(merged, w_ob, x2d, g1, sh2, sc2, norm2_w)
    u = _glu(h2, w_upb, ffn_conv_w[0], ffn_conv_b)
    out = _down(u, w_db, x1, g2, final_norm_w.reshape(1, d))
    return out.reshape(BATCH, SEQ, d)
```

```python
import functools

import numpy as np
import jax
import jax.numpy as jnp
from jax import lax
from jax.experimental import pallas as pl
from jax.experimental.pallas import tpu as pltpu

F32 = jnp.float32
BF16 = jnp.bfloat16

D_MODEL = 2048
BATCH = 4
SEQ = 4096
TOKENS = BATCH * SEQ
GRID_W = 64
CTX_LEN = 256

SSD_HEAD_DIM = 64
SSD_HEADS = 32
SSD_GROUPS = 8
SSD_REP = 4
SSD_STATE = 128
CHUNK = 128
GROUP_W = SSD_REP * SSD_HEAD_DIM
N_CHUNKS = SEQ // CHUNK
N_CTX_CHUNKS = CTX_LEN // CHUNK
SCAN_UNROLL = 2
GLU_ROWS = 64
INPROJ_TN, MERGE_TN, GLU_TN = 1024, 512, 512

POOL_WINDOWS = (2, 4, 8, 16)
POOL_GW = 512
D_FF = 5632
EPS = 1e-6

COL_X = 0
COL_B = 2048
COL_C = 3072
COL_Z = 4096
COL_V = 6144
COL_GS = 8192
COL_GP = 10240
N_PROJ = 12288
N_CTX_PROJ = 3072

LANES = 128
VMEM_LIMIT = 56 * 1024 * 1024


def _dot(a, b):
    return jnp.dot(a, b, preferred_element_type=F32)


def _silu(x):
    return x * jax.nn.sigmoid(x)


def _softplus(x):
    return jnp.maximum(x, 0.0) + jnp.log1p(jnp.exp(-jnp.abs(x)))


def _params(sem):
    return pltpu.CompilerParams(dimension_semantics=sem, vmem_limit_bytes=VMEM_LIMIT)


def _ada_kernel(c_ref, w_ref, b_ref, o_ref):
    s = _silu(c_ref[...])
    o_ref[...] = jnp.dot(s, w_ref[...], precision=lax.Precision.HIGHEST,
                         preferred_element_type=F32) + b_ref[...]


def _ada(cs, w, b):
    tn = 1024
    n = w.shape[1]
    return pl.pallas_call(
        _ada_kernel,
        grid=(n // tn,),
        in_specs=[pl.BlockSpec((8, D_MODEL), lambda j: (0, 0)),
                  pl.BlockSpec((D_MODEL, tn), lambda j: (0, j)),
                  pl.BlockSpec((1, tn), lambda j: (0, j))],
        out_specs=pl.BlockSpec((8, tn), lambda j: (0, j)),
        out_shape=jax.ShapeDtypeStruct((8, n), F32),
        compiler_params=_params(("arbitrary",)),
        name="ada",
    )(cs, w, b)


def _modulate_rows(x_ref, nw, sc, sh, out_ref, rows, rc=256):
    def body(k, carry):
        r = pl.multiple_of(k * rc, rc)
        x = x_ref[pl.ds(r, rc), :]
        ms = jnp.mean(x * x, axis=-1, keepdims=True)
        y = x * lax.rsqrt(ms + EPS) * nw
        out_ref[pl.ds(r, rc), :] = (y * (1.0 + sc) + sh).astype(out_ref.dtype)
        return carry
    lax.fori_loop(0, rows // rc, body, 0)


def _inproj_kernel(x_ref, sh_ref, sc_ref, nw_ref, w_ref, wdt_ref, o_ref, dt_ref, h_scr):
    @pl.when(pl.program_id(1) == 0)
    def _():
        _modulate_rows(x_ref, nw_ref[...], sc_ref[0], sh_ref[0], h_scr, h_scr.shape[0])
        dt_ref[...] = _dot(h_scr[...], wdt_ref[...].astype(BF16))

    o_ref[...] = _dot(h_scr[...], w_ref[0]).astype(o_ref.dtype)


def _inproj(x2d, sh, sc, nw, w_blocks, w_dt, *, n_cols, rows_per_mod, tm=1024):
    m = x2d.shape[0]
    tn = w_blocks.shape[2]
    tpb = rows_per_mod // tm
    return pl.pallas_call(
        _inproj_kernel,
        grid=(m // tm, n_cols // tn),
        in_specs=[pl.BlockSpec((tm, D_MODEL), lambda i, j: (i, 0)),
                  pl.BlockSpec((1, 1, D_MODEL), lambda i, j: (i // tpb, 0, 0)),
                  pl.BlockSpec((1, 1, D_MODEL), lambda i, j: (i // tpb, 0, 0)),
                  pl.BlockSpec((1, D_MODEL), lambda i, j: (0, 0)),
                  pl.BlockSpec((1, D_MODEL, tn), lambda i, j: (j, 0, 0)),
                  pl.BlockSpec((D_MODEL, LANES), lambda i, j: (0, 0))],
        out_specs=[pl.BlockSpec((tm, tn), lambda i, j: (i, j)),
                   pl.BlockSpec((tm, LANES), lambda i, j: (i, 0))],
        out_shape=[jax.ShapeDtypeStruct((m, n_cols), BF16),
                   jax.ShapeDtypeStruct((m, LANES), F32)],
        scratch_shapes=[pltpu.VMEM((tm, D_MODEL), BF16)],
        compiler_params=_params(("arbitrary", "arbitrary")),
        name="inproj",
    )(x2d, sh, sc, nw, w_blocks, w_dt)


def _conv_silu_tile(ref, c, n_chunks, w, bias):
    length = n_chunks * CHUNK
    if isinstance(c, int):
        r0, p0, n0 = c * CHUNK, max(c * CHUNK - 16, 0), min(c * CHUNK + CHUNK, length - 16)
    else:
        r0 = pl.multiple_of(c * CHUNK, CHUNK)
        p0 = pl.multiple_of(jnp.maximum(c * CHUNK - 16, 0), 16)
        n0 = pl.multiple_of(jnp.minimum(c * CHUNK + CHUNK, length - 16), 16)
    x = ref[0, pl.ds(r0, CHUNK), :].astype(F32)
    prev_row = ref[0, pl.ds(p0, 16), :].astype(F32)[15:16, :]
    prev_row = jnp.where(c > 0, prev_row, 0.0)
    next_row = ref[0, pl.ds(n0, 16), :].astype(F32)[0:1, :]
    next_row = jnp.where(c < n_chunks - 1, next_row, 0.0)
    rows = lax.broadcasted_iota(jnp.int32, x.shape, 0)
    xm = jnp.where(rows == 0, prev_row, pltpu.roll(x, 1, axis=0))
    xp = jnp.where(rows == CHUNK - 1, next_row, pltpu.roll(x, CHUNK - 1, axis=0))
    y = xm * w[0:1, :] + x * w[1:2, :] + xp * w[2:3, :] + bias
    return _silu(y)


def _decay_rows(dt_raw, dt_bias, a_neg, tri, causal):
    dt = _softplus(dt_raw + dt_bias)
    a = dt * a_neg
    hi = a.astype(BF16)
    lo = (a - hi.astype(F32)).astype(BF16)
    r = _dot(jnp.concatenate([hi, lo], axis=0), tri)
    acs = r[0:8] + r[8:16]
    tot = acs[:, CHUNK - 1:CHUNK] if causal else acs[:, 0:1]
    return acs, dt, jnp.exp(tot - acs) * dt, jnp.exp(tot)


def _split_bf16(x, parts):
    out = []
    for _ in range(parts - 1):
        hi = x.astype(BF16).astype(F32)
        out.append(hi)
        x = x - hi
    out.append(x.astype(BF16).astype(F32))
    return out


TAB_ACS, TAB_ONE, TAB_EXP, TAB_W, TAB_DEC, TAB_USED = 0, 3, 6, 8, 10, 12
SEG_ROWS = 48


def _decay_tables(dtr_ref, dt_bias, a_neg, tri_ref, a_scr, bsel_scr, dt_scr):
    n = N_CHUNKS * 8
    dt = _softplus(dtr_ref[0, 0] + dt_bias)
    a = (dt * a_neg).reshape(n, CHUNK)
    dt = dt.reshape(n, CHUNK)
    hi, lo = _split_bf16(a, 2)
    lhs = jnp.concatenate([hi, lo], axis=0).astype(BF16)
    pre = _dot(lhs, tri_ref[0])
    suf = _dot(lhs, tri_ref[1])
    row = lax.broadcasted_iota(jnp.int32, (n, CHUNK), 0)
    fwd = jnp.bitwise_and(row, 7) < SSD_REP
    acs = jnp.where(fwd, pre[:n] + pre[n:], suf[:n] + suf[n:])
    tot = jnp.where(fwd[:, 0:1], acs[:, CHUNK - 1:CHUNK], acs[:, 0:1])
    ones = jnp.ones((n, CHUNK), F32)
    comps_a = (_split_bf16(acs, 3) + [ones, ones, ones] + _split_bf16(jnp.exp(acs), 2)
               + _split_bf16(jnp.exp(tot - acs) * dt, 2)
               + _split_bf16(jnp.broadcast_to(jnp.exp(tot), acs.shape), 2))
    comps_b = [ones, ones, ones] + [-p for p in _split_bf16(acs, 3)]
    row48 = jnp.bitwise_and(lax.broadcasted_iota(jnp.int32, (SEG_ROWS, CHUNK), 0), 7)
    pad = jnp.zeros((CHUNK - 8 * TAB_USED, CHUNK), F32)
    for c in range(N_CHUNKS):
        sl = slice(c * 8, c * 8 + 8)
        a_scr[c] = jnp.concatenate([p[sl] for p in comps_a] + [pad], axis=0).T.astype(BF16)
        b_all = jnp.concatenate([p[sl] for p in comps_b], axis=0)
        for k in range(8):
            bsel_scr[c, :, k * CHUNK:(k + 1) * CHUNK] = jnp.where(row48 == k, b_all, 0.0).astype(BF16)
        dt_scr[c] = dt[sl]


def _head_lanes(dec, off):
    lane = lax.broadcasted_iota(jnp.int32, (1, LANES), 1)
    t0 = jnp.where(lane < SSD_HEAD_DIM, dec[off:off + 1, :], dec[off + 1:off + 2, :])
    t1 = jnp.where(lane < SSD_HEAD_DIM, dec[off + 2:off + 3, :], dec[off + 3:off + 4, :])
    return jnp.concatenate([t0, t1], axis=1)


def _split_heads(t):
    lane = lax.broadcasted_iota(jnp.int32, t.shape, 1)
    return (jnp.where(lane < SSD_HEAD_DIM, t, 0.0).astype(BF16),
            jnp.where(lane >= SSD_HEAD_DIM, t, 0.0).astype(BF16))


def _state_update(x_split, bt, w, dec, off, h_ref):
    h = h_ref[...]
    ss = []
    for pair in range(2):
        bw = [(bt * w[off + r:off + r + 1, :]).astype(BF16) for r in (2 * pair, 2 * pair + 1)]
        ss.append(_dot(jnp.concatenate(bw, axis=1), jnp.concatenate(x_split[pair], axis=0)))
    h_ref[...] = h * _head_lanes(dec, off) + jnp.concatenate(ss, axis=1)
    return h


def _chunk_dir(xs, x_pairs, bt, cc, cb, tab, bsel, dt, sel, off, h_ref, causal):
    l_io = lax.broadcasted_iota(jnp.int32, (CHUNK, CHUNK), 0)
    s_io = lax.broadcasted_iota(jnp.int32, (CHUNK, CHUNK), 1)
    mask = (l_io >= s_io) if causal else (l_io <= s_io)
    seg = _dot(tab, jnp.concatenate([bsel, jnp.zeros((CHUNK - SEG_ROWS, bsel.shape[1]), BF16)],
                                    axis=0))
    spread = _dot(tab, sel)
    e_exp, w_exp, dec = spread[:, 0:GROUP_W], spread[:, GROUP_W:2 * GROUP_W], spread[0:1, 2 * GROUP_W:]
    h = h_ref[...]
    h_ref[...] = h * dec + _dot(bt, (xs * w_exp).astype(BF16))
    y = _dot(cc, h.astype(BF16)) * e_exp
    ys = []
    for pair in range(2):
        ms = []
        for r in (2 * pair, 2 * pair + 1):
            decay = jnp.exp(jnp.where(mask, seg[:, r * CHUNK:(r + 1) * CHUNK], -jnp.inf))
            ms.append((cb * decay * dt[off + r:off + r + 1, :]).astype(BF16))
        ys.append(_dot(jnp.concatenate(ms, axis=1), x_pairs[pair]))
    return y + jnp.concatenate(ys, axis=1)


def _ssd_kernel(xs_ref, b_ref, c_ref, z_ref, dtr_ref, cxs_ref, cbm_ref, cdtr_ref,
                cwx_ref, cwb_ref, cwc_ref, bx_ref, bb_ref, bc_ref,
                dtb_ref, alog_ref, dsk_ref, nw_ref, tri_ref, sel_ref,
                y_ref,
                xs_scr, xp_scr, bt_scr, cc_scr, cb_scr, y_scr, hf_scr, hb_scr,
                tab_scr, bsel_scr, dt_scr):
    dt_bias = dtb_ref[0]
    a_neg = -jnp.exp(alog_ref[0])
    cwx, cwb, cwc = cwx_ref[...], cwb_ref[...], cwc_ref[...]
    bx, bb, bc = bx_ref[...], bb_ref[...], bc_ref[...]

    hf_scr[...] = jnp.zeros(hf_scr.shape, F32)
    hb_scr[...] = jnp.zeros(hb_scr.shape, F32)
    ctx_tiles = []
    for c in range(N_CTX_CHUNKS):
        xs = _conv_silu_tile(cxs_ref, c, N_CTX_CHUNKS, cwx, bx)
        bt = _conv_silu_tile(cbm_ref, c, N_CTX_CHUNKS, cwb, bb).T
        ctx_tiles.append(([_split_heads(xs[:, p * LANES:(p + 1) * LANES]) for p in range(2)], bt))
    for c in range(N_CTX_CHUNKS):
        _, _, w, dec = _decay_rows(cdtr_ref[0, 0, c], dt_bias, a_neg, tri_ref[0], True)
        _state_update(ctx_tiles[c][0], ctx_tiles[c][1], w, dec, 0, hf_scr)
    for c in reversed(range(N_CTX_CHUNKS)):
        _, _, w, dec = _decay_rows(cdtr_ref[0, 0, c], dt_bias, a_neg, tri_ref[1], False)
        _state_update(ctx_tiles[c][0], ctx_tiles[c][1], w, dec, 4, hb_scr)

    _decay_tables(dtr_ref, dt_bias, a_neg, tri_ref, tab_scr, bsel_scr, dt_scr)

    def prepare(c):
        r0 = pl.multiple_of(c * CHUNK, CHUNK)
        xs = _conv_silu_tile(xs_ref, c, N_CHUNKS, cwx, bx)
        bt = _conv_silu_tile(b_ref, c, N_CHUNKS, cwb, bb).T.astype(BF16)
        cc = _conv_silu_tile(c_ref, c, N_CHUNKS, cwc, bc).astype(BF16)
        cb = _dot(cc, bt)
        x_pairs = [jnp.concatenate(_split_heads(xs[:, p * LANES:(p + 1) * LANES]), axis=0)
                   for p in range(2)]
        xs_scr[pl.ds(r0, CHUNK), :] = xs
        xp_scr[c, 0] = x_pairs[0]
        xp_scr[c, 1] = x_pairs[1]
        bt_scr[c] = bt
        cc_scr[c] = cc
        cb_scr[c] = cb
        return xs, x_pairs, bt, cc, cb

    def recall(c):
        r0 = pl.multiple_of(c * CHUNK, CHUNK)
        return (xs_scr[pl.ds(r0, CHUNK), :], [xp_scr[c, 0], xp_scr[c, 1]], bt_scr[c], cc_scr[c],
                cb_scr[c])

    def scan(c, tiles, causal):
        xs, x_pairs, bt, cc, cb = tiles
        d = 0 if causal else 1
        bsel = bsel_scr[c, :, d * SSD_REP * CHUNK:(d + 1) * SSD_REP * CHUNK]
        return _chunk_dir(xs, x_pairs, bt, cc, cb, tab_scr[c], bsel, dt_scr[c], sel_ref[d],
                          d * SSD_REP, hf_scr if causal else hb_scr, causal)

    d_sum = dsk_ref[0, 0:1, :] + dsk_ref[0, 1:2, :]
    nw = nw_ref[...]

    def finish(c, y_dir, xs):
        r0 = pl.multiple_of(c * CHUNK, CHUNK)
        y = y_scr[pl.ds(r0, CHUNK), :] + y_dir + xs * d_sum
        gated = y * _silu(z_ref[0, pl.ds(r0, CHUNK), :].astype(F32))
        ms = jnp.mean(gated * gated, axis=-1, keepdims=True)
        y_ref[0, pl.ds(r0, CHUNK), :] = (gated * lax.rsqrt(ms + EPS) * nw).astype(y_ref.dtype)

    def visits(i):
        fwd = [i * SCAN_UNROLL + u for u in range(SCAN_UNROLL)]
        return [(c, True) for c in fwd] + [(N_CHUNKS - 1 - c, False) for c in fwd]

    def first_half(i, carry):
        for c, causal in visits(i):
            r0 = pl.multiple_of(c * CHUNK, CHUNK)
            y_scr[pl.ds(r0, CHUNK), :] = scan(c, prepare(c), causal)
        return carry

    def second_half(i, carry):
        for c, causal in visits(i):
            tiles = recall(c)
            finish(c, scan(c, tiles, causal), tiles[0])
        return carry

    half = N_CHUNKS // 2 // SCAN_UNROLL
    lax.fori_loop(0, half, first_half, 0)
    lax.fori_loop(half, 2 * half, second_half, 0)


def _ssd(proj, dtr, cproj, cdtr, conv_w, conv_b, dtb, alog, dsk, norm_w, tri):
    gx = COL_X // GROUP_W
    gz = COL_Z // GROUP_W
    gb = COL_B // SSD_STATE
    gc = COL_C // SSD_STATE
    in_specs = [
        pl.BlockSpec((1, SEQ, GROUP_W), lambda b, g: (b, 0, gx + g)),
        pl.BlockSpec((1, SEQ, SSD_STATE), lambda b, g: (b, 0, gb + g)),
        pl.BlockSpec((1, SEQ, SSD_STATE), lambda b, g: (b, 0, gc + g)),
        pl.BlockSpec((1, SEQ, GROUP_W), lambda b, g: (b, 0, gz + g)),
        pl.BlockSpec((1, 1, N_CHUNKS, 8, CHUNK), lambda b, g: (g, b, 0, 0, 0)),
        pl.BlockSpec((1, CTX_LEN, GROUP_W), lambda b, g: (b, 0, gx + g)),
        pl.BlockSpec((1, CTX_LEN, SSD_STATE), lambda b, g: (b, 0, gb + g)),
        pl.BlockSpec((1, 1, N_CTX_CHUNKS, 8, CHUNK), lambda b, g: (g, b, 0, 0, 0)),
        pl.BlockSpec((3, GROUP_W), lambda b, g: (0, gx + g)),
        pl.BlockSpec((3, SSD_STATE), lambda b, g: (0, gb + g)),
        pl.BlockSpec((3, SSD_STATE), lambda b, g: (0, gc + g)),
        pl.BlockSpec((1, GROUP_W), lambda b, g: (0, gx + g)),
        pl.BlockSpec((1, SSD_STATE), lambda b, g: (0, gb + g)),
        pl.BlockSpec((1, SSD_STATE), lambda b, g: (0, gc + g)),
        pl.BlockSpec((1, 8, 1), lambda b, g: (g, 0, 0)),
        pl.BlockSpec((1, 8, 1), lambda b, g: (g, 0, 0)),
        pl.BlockSpec((1, 2, GROUP_W), lambda b, g: (g, 0, 0)),
        pl.BlockSpec((1, GROUP_W), lambda b, g: (0, g)),
        pl.BlockSpec((2, CHUNK, CHUNK), lambda b, g: (0, 0, 0)),
        pl.BlockSpec((2, CHUNK, 3 * GROUP_W), lambda b, g: (0, 0, 0)),
    ]
    return pl.pallas_call(
        _ssd_kernel,
        grid=(BATCH, SSD_GROUPS),
        in_specs=in_specs,
        out_specs=pl.BlockSpec((1, SEQ, GROUP_W), lambda b, g: (b, 0, g)),
        out_shape=jax.ShapeDtypeStruct((BATCH, SEQ, D_MODEL), BF16),
        scratch_shapes=[pltpu.VMEM((SEQ, GROUP_W), F32),
                        pltpu.VMEM((N_CHUNKS, 2, 2 * CHUNK, LANES), BF16),
                        pltpu.VMEM((N_CHUNKS, CHUNK, CHUNK), BF16),
                        pltpu.VMEM((N_CHUNKS, CHUNK, CHUNK), BF16),
                        pltpu.VMEM((N_CHUNKS, CHUNK, CHUNK), F32),
                        pltpu.VMEM((SEQ, GROUP_W), F32),
                        pltpu.VMEM((SSD_STATE, GROUP_W), F32),
                        pltpu.VMEM((SSD_STATE, GROUP_W), F32),
                        pltpu.VMEM((N_CHUNKS, CHUNK, CHUNK), BF16),
                        pltpu.VMEM((N_CHUNKS, SEG_ROWS, 8 * CHUNK), BF16),
                        pltpu.VMEM((N_CHUNKS, 8, CHUNK), F32)],
        compiler_params=_params(("arbitrary", "arbitrary")),
        name="ssd",
    )(proj, proj, proj, proj, dtr, cproj, cproj, cdtr,
      conv_w, conv_w, conv_w, conv_b, conv_b, conv_b, dtb, alog, dsk, norm_w, tri, _spread_matrix())


def _spread_matrix():
    sel = np.zeros((2, CHUNK, 3 * GROUP_W), np.float32)
    for d in range(2):
        for grp, comp in enumerate((TAB_EXP, TAB_W, TAB_DEC)):
            for part in range(2):
                for h in range(SSD_REP):
                    lanes = slice(grp * GROUP_W + h * SSD_HEAD_DIM,
                                  grp * GROUP_W + (h + 1) * SSD_HEAD_DIM)
                    sel[d, (comp + part) * 8 + d * SSD_REP + h, lanes] = 1.0
    return jnp.asarray(sel, BF16)


def _pool_kernel(v_ref, win_ref, inv_ref, pw_ref, ps_ref, o_ref, p_scr):
    slab = win_ref.shape[1]
    for r in range(0, v_ref.shape[0], slab):
        v = v_ref[r:r + slab, :]
        p_scr[r:r + slab, :] = (_dot(win_ref[0], v) * inv_ref[0] - v.astype(F32)).astype(BF16)
    o_ref[...] = (_dot(p_scr[...], pw_ref[0]) * ps_ref[...]).astype(o_ref.dtype)


def _pool_constants(tm):
    t = np.arange(tm)
    col = t % GRID_W
    wins, invs = [], []
    for w in POOL_WINDOWS:
        start = np.clip(col - w // 2, 0, GRID_W)
        end = np.clip(col + w - w // 2, 0, GRID_W)
        same_row = (t[:, None] // GRID_W) == (t[None, :] // GRID_W)
        inside = (col[None, :] >= start[:, None]) & (col[None, :] < end[:, None])
        wins.append((same_row & inside).astype(np.float32))
        invs.append((1.0 / (end - start).astype(np.float32))[:, None])
    return jnp.asarray(np.stack(wins), BF16), jnp.asarray(np.stack(invs), F32)


def _pool(proj2d, pool_w, pool_scale, tm=2048, slab=128):
    win, inv = _pool_constants(slab)
    gv = COL_V // POOL_GW
    return pl.pallas_call(
        _pool_kernel,
        grid=(TOKENS // tm, len(POOL_WINDOWS)),
        in_specs=[pl.BlockSpec((tm, POOL_GW), lambda i, k: (i, gv + k)),
                  pl.BlockSpec((1, slab, slab), lambda i, k: (k, 0, 0)),
                  pl.BlockSpec((1, slab, 1), lambda i, k: (k, 0, 0)),
                  pl.BlockSpec((1, POOL_GW, POOL_GW), lambda i, k: (k, 0, 0)),
                  pl.BlockSpec((1, POOL_GW), lambda i, k: (0, k))],
        out_specs=pl.BlockSpec((tm, POOL_GW), lambda i, k: (i, k)),
        out_shape=jax.ShapeDtypeStruct((TOKENS, D_MODEL), BF16),
        scratch_shapes=[pltpu.VMEM((tm, POOL_GW), BF16)],
        compiler_params=_params(("arbitrary", "arbitrary")),
        name="pool",
    )(proj2d, win, inv, pool_w, pool_scale)


def _merge_kernel(ys_ref, yp_ref, ws_ref, wp_ref, gs_ref, gp_ref, o_ref):
    a = jax.nn.sigmoid(gs_ref[...].astype(F32)) * _dot(ys_ref[...], ws_ref[0])
    b = jax.nn.sigmoid(gp_ref[...].astype(F32)) * _dot(yp_ref[...], wp_ref[0])
    o_ref[...] = (a + b).astype(o_ref.dtype)


def _merge(y_ssd, y_pool, w_s, w_p, proj2d, tm=1024):
    tn = w_s.shape[2]
    ggs = COL_GS // tn
    ggp = COL_GP // tn
    return pl.pallas_call(
        _merge_kernel,
        grid=(TOKENS // tm, D_MODEL // tn),
        in_specs=[pl.BlockSpec((tm, D_MODEL), lambda i, j: (i, 0)),
                  pl.BlockSpec((tm, D_MODEL), lambda i, j: (i, 0)),
                  pl.BlockSpec((1, D_MODEL, tn), lambda i, j: (j, 0, 0)),
                  pl.BlockSpec((1, D_MODEL, tn), lambda i, j: (j, 0, 0)),
                  pl.BlockSpec((tm, tn), lambda i, j: (i, ggs + j)),
                  pl.BlockSpec((tm, tn), lambda i, j: (i, ggp + j))],
        out_specs=pl.BlockSpec((tm, tn), lambda i, j: (i, j)),
        out_shape=jax.ShapeDtypeStruct((TOKENS, D_MODEL), BF16),
        compiler_params=_params(("arbitrary", "arbitrary")),
        name="merge",
    )(y_ssd, y_pool, w_s, w_p, proj2d, proj2d)


def _outproj_kernel(m_ref, wo_ref, x_ref, g1_ref, sh_ref, sc_ref, nw_ref, x1_ref, h2_ref):
    x1_ref[...] = x_ref[...] + g1_ref[0] * _dot(m_ref[...], wo_ref[...])
    _modulate_rows(x1_ref, nw_ref[...], sc_ref[0], sh_ref[0], h2_ref, x1_ref.shape[0], rc=128)


def _outproj(merged, w_o, x2d, g1, sh2, sc2, nw2, tm=256):
    tpb = SEQ // tm
    mod_spec = pl.BlockSpec((1, 1, D_MODEL), lambda i: (i // tpb, 0, 0))
    return pl.pallas_call(
        _outproj_kernel,
        grid=(TOKENS // tm,),
        in_specs=[pl.BlockSpec((tm, D_MODEL), lambda i: (i, 0)),
                  pl.BlockSpec((D_MODEL, D_MODEL), lambda i: (0, 0)),
                  pl.BlockSpec((tm, D_MODEL), lambda i: (i, 0)),
                  mod_spec, mod_spec, mod_spec,
                  pl.BlockSpec((1, D_MODEL), lambda i: (0, 0))],
        out_specs=[pl.BlockSpec((tm, D_MODEL), lambda i: (i, 0)),
                   pl.BlockSpec((tm, D_MODEL), lambda i: (i, 0))],
        out_shape=[jax.ShapeDtypeStruct((TOKENS, D_MODEL), F32),
                   jax.ShapeDtypeStruct((TOKENS, D_MODEL), BF16)],
        compiler_params=_params(("arbitrary",)),
        name="outproj",
    )(merged, w_o, x2d, g1, sh2, sc2, nw2)


def _glu_kernel(h_ref, hp_ref, hn_ref, wa_ref, wg_ref, fw_ref, fb_ref, u_ref,
                hext_scr, a_even, g_even, a_odd, g_odd, *, n_chunks, n_steps):
    s = pl.program_id(0)
    tm = h_ref.shape[0]
    cur = jnp.minimum(s, n_steps - 2)
    i = cur // n_chunks

    @pl.when(s == 0)
    def _():
        a_odd[...] = jnp.zeros(a_odd.shape, F32)
        g_odd[...] = jnp.zeros(g_odd.shape, F32)

    @pl.when(cur % n_chunks == 0)
    def _():
        has_prev = ((i * tm) % SEQ) != 0
        has_next = (((i + 1) * tm) % SEQ) != 0
        hext_scr[0:GRID_W, :] = jnp.where(has_prev, hp_ref[...], jnp.zeros_like(hp_ref[...]))
        hext_scr[GRID_W:GRID_W + tm, :] = h_ref[...]
        hext_scr[GRID_W + tm:, :] = jnp.where(has_next, hn_ref[...], jnp.zeros_like(hn_ref[...]))

    def step(wr_a, wr_g, rd_a, rd_g):
        fw = fw_ref[...]
        fb = fb_ref[...]
        for r in range(0, tm, GLU_ROWS):
            cv = (rd_g[r:r + GLU_ROWS, :] * fw[0:1, :] + fb
                  + rd_g[r + GRID_W:r + GRID_W + GLU_ROWS, :] * fw[1:2, :]
                  + rd_g[r + 2 * GRID_W:r + 2 * GRID_W + GLU_ROWS, :] * fw[2:3, :])
            gelu = 0.5 * cv * (1.0 + lax.erf(cv * np.float32(np.sqrt(0.5))))
            u_ref[r:r + GLU_ROWS, :] = (gelu * rd_a[r:r + GLU_ROWS, :]).astype(u_ref.dtype)
        wr_a[...] = _dot(hext_scr[GRID_W:GRID_W + tm, :], wa_ref[0])
        wr_g[...] = _dot(hext_scr[...], wg_ref[0])

    @pl.when(s % 2 == 0)
    def _():
        step(a_even, g_even, a_odd, g_odd)

    @pl.when(s % 2 == 1)
    def _():
        step(a_odd, g_odd, a_even, g_even)


def _glu(h2, w_up, ffn_conv_w, ffn_conv_b, tm=1024):
    tn = w_up.shape[2]
    n_chunks = D_FF // tn
    n_tiles = TOKENS // tm
    n_steps = n_tiles * n_chunks + 1
    hb = tm // GRID_W
    n_halo = TOKENS // GRID_W

    def cur(s):
        return jnp.minimum(s, n_steps - 2)

    def prev(s):
        return jnp.maximum(s - 1, 0)

    return pl.pallas_call(
        functools.partial(_glu_kernel, n_chunks=n_chunks, n_steps=n_steps),
        grid=(n_steps,),
        in_specs=[pl.BlockSpec((tm, D_MODEL), lambda s: (cur(s) // n_chunks, 0)),
                  pl.BlockSpec((GRID_W, D_MODEL),
                               lambda s: (jnp.maximum((cur(s) // n_chunks) * hb - 1, 0), 0)),
                  pl.BlockSpec((GRID_W, D_MODEL),
                               lambda s: (jnp.minimum((cur(s) // n_chunks + 1) * hb, n_halo - 1), 0)),
                  pl.BlockSpec((1, D_MODEL, tn), lambda s: (cur(s) % n_chunks, 0, 0)),
                  pl.BlockSpec((1, D_MODEL, tn), lambda s: (n_chunks + cur(s) % n_chunks, 0, 0)),
                  pl.BlockSpec((3, tn), lambda s: (0, prev(s) % n_chunks)),
                  pl.BlockSpec((1, tn), lambda s: (0, prev(s) % n_chunks))],
        out_specs=pl.BlockSpec((tm, tn), lambda s: (prev(s) // n_chunks, prev(s) % n_chunks)),
        out_shape=jax.ShapeDtypeStruct((TOKENS, D_FF), BF16),
        scratch_shapes=[pltpu.VMEM((tm + 2 * GRID_W, D_MODEL), BF16),
                        pltpu.VMEM((tm, tn), F32), pltpu.VMEM((tm + 2 * GRID_W, tn), F32),
                        pltpu.VMEM((tm, tn), F32), pltpu.VMEM((tm + 2 * GRID_W, tn), F32)],
        compiler_params=_params(("arbitrary",)),
        name="glu",
    )(h2, h2, h2, w_up, w_up, ffn_conv_w, ffn_conv_b)


def _down_kernel(u_ref, wd_ref, x1_ref, g2_ref, nw_ref, o_ref):
    x2 = x1_ref[...] + g2_ref[0] * _dot(u_ref[...], wd_ref[...])
    ms = jnp.mean(x2 * x2, axis=-1, keepdims=True)
    o_ref[...] = x2 * lax.rsqrt(ms + EPS) * nw_ref[...]


def _down(u, w_down, x1, g2, fnw, tm=512):
    tpb = SEQ // tm
    return pl.pallas_call(
        _down_kernel,
        grid=(TOKENS // tm,),
        in_specs=[pl.BlockSpec((tm, D_FF), lambda i: (i, 0)),
                  pl.BlockSpec((D_FF, D_MODEL), lambda i: (0, 0), pipeline_mode=pl.Buffered(1)),
                  pl.BlockSpec((tm, D_MODEL), lambda i: (i, 0)),
                  pl.BlockSpec((1, 1, D_MODEL), lambda i: (i // tpb, 0, 0)),
                  pl.BlockSpec((1, D_MODEL), lambda i: (0, 0))],
        out_specs=pl.BlockSpec((tm, D_MODEL), lambda i: (i, 0)),
        out_shape=jax.ShapeDtypeStruct((TOKENS, D_MODEL), F32),
        compiler_params=_params(("arbitrary",)),
        name="down",
    )(u, w_down, x1, g2, fnw)


def _dt_rows(dt_raw, batch, n_chunks):
    d = dt_raw[:, :2 * SSD_HEADS].reshape(batch, n_chunks, CHUNK, 2, SSD_GROUPS, SSD_REP)
    d = jnp.transpose(d, (4, 0, 1, 3, 5, 2))
    return d.reshape(SSD_GROUPS, batch, n_chunks, 2 * SSD_REP, CHUNK)


def _col_blocks(w, tn):
    k, n = w.shape
    return jnp.transpose(w.reshape(k, n // tn, tn), (1, 0, 2))


def _group_rows(p):
    p = p.reshape(2, SSD_GROUPS, SSD_REP)
    return jnp.transpose(p, (1, 0, 2)).reshape(SSD_GROUPS, 2 * SSD_REP, 1)


def kernel(x, c, ctx, c_ctx, w_ada, b_ada, norm1_w, w_in, ssd_conv_w, ssd_conv_b, dt_bias, a_log,
           d_skip, ssd_norm_w, w_ssd_out, pool_w, pool_scale, w_pool_out, w_o, norm2_w, w_up,
           ffn_conv_w, ffn_conv_b, w_down, final_norm_w):
    assert w_in.shape[0] == 1, "single-layer block only"
    d = D_MODEL
    x2d = x.reshape(TOKENS, d)
    ctx2d = ctx.reshape(BATCH * CTX_LEN, d)

    cs = jnp.zeros((8, d), F32).at[:BATCH].set(c).at[BATCH].set(c_ctx)
    mod = _ada(cs, w_ada[0], b_ada)
    sh1, sc1, g1, sh2, sc2, g2 = [mod[:BATCH, k * d:(k + 1) * d].reshape(BATCH, 1, d)
                                  for k in range(6)]
    csh1 = mod[BATCH:BATCH + 1, 0:d].reshape(1, 1, d)
    csc1 = mod[BATCH:BATCH + 1, d:2 * d].reshape(1, 1, d)

    w_in0 = w_in[0]
    n_dt = 2 * SSD_HEADS
    w_dt = jnp.pad(w_in0[:, :n_dt], ((0, 0), (0, LANES - n_dt)))
    w_cat = _col_blocks(w_in0[:, n_dt:].astype(BF16), INPROJ_TN)
    w_s = _col_blocks(w_ssd_out[0].astype(BF16), MERGE_TN)
    w_p = _col_blocks(w_pool_out[0].astype(BF16), MERGE_TN)
    w_ob = w_o[0].astype(BF16)
    w_upb = _col_blocks(w_up[0].astype(BF16), GLU_TN)
    w_db = w_down[0].astype(BF16)
    pool_wb = pool_w[0].astype(BF16)

    proj, dt_raw = _inproj(x2d, sh1, sc1, norm1_w, w_cat, w_dt, n_cols=N_PROJ, rows_per_mod=SEQ)
    cproj, cdt_raw = _inproj(ctx2d, csh1, csc1, norm1_w, w_cat, w_dt, n_cols=N_CTX_PROJ,
                             rows_per_mod=BATCH * CTX_LEN)

    dtr = _dt_rows(dt_raw, BATCH, N_CHUNKS)
    cdtr = _dt_rows(cdt_raw, BATCH, N_CTX_CHUNKS)
    dsk = jnp.repeat(jnp.transpose(d_skip[0].reshape(2, SSD_GROUPS, SSD_REP), (1, 0, 2)),
                     SSD_HEAD_DIM, axis=-1)
    idx = np.arange(CHUNK)
    tri = jnp.asarray(np.stack([idx[:, None] <= idx[None, :], idx[:, None] >= idx[None, :]])
                      .astype(np.float32), BF16)

    y_ssd = _ssd(proj.reshape(BATCH, SEQ, N_PROJ), dtr,
                 cproj.reshape(BATCH, CTX_LEN, N_CTX_PROJ), cdtr,
                 ssd_conv_w[0], ssd_conv_b, _group_rows(dt_bias[0]), _group_rows(a_log[0]),
                 dsk, ssd_norm_w, tri)
    y_pool = _pool(proj, pool_wb, pool_scale)
    merged = _merge(y_ssd.reshape(TOKENS, d), y_pool, w_s, w_p, proj)
    x1, h2 = _outproj(merged, w_ob, x2d, g1, sh2, sc2, norm2_w)
    u = _glu(h2, w_upb, ffn_conv_w[0], ffn_conv_b)
    out = _down(u, w_db, x1, g2, final_norm_w.reshape(1, d))
    return out.reshape(BATCH, SEQ, d)
```

```python
import functools

import numpy as np
import jax
import jax.numpy as jnp
from jax import lax
from jax.experimental import pallas as pl
from jax.experimental.pallas import tpu as pltpu

F32 = jnp.float32
BF16 = jnp.bfloat16

D_MODEL = 2048
BATCH = 4
SEQ = 4096
TOKENS = BATCH * SEQ
GRID_W = 64
CTX_LEN = 256

SSD_HEAD_DIM = 64
SSD_HEADS = 32
SSD_GROUPS = 8
SSD_REP = 4
SSD_STATE = 128
CHUNK = 128
GROUP_W = SSD_REP * SSD_HEAD_DIM
N_CHUNKS = SEQ // CHUNK
N_CTX_CHUNKS = CTX_LEN // CHUNK
SCAN_UNROLL = 2
GLU_ROWS = 64
INPROJ_TN, MERGE_TN, GLU_TN = 1536, 512, 512

POOL_WINDOWS = (2, 4, 8, 16)
POOL_GW = 512
D_FF = 5632
EPS = 1e-6

COL_X = 0
COL_B = 2048
COL_C = 3072
COL_Z = 4096
COL_V = 6144
COL_GS = 8192
COL_GP = 10240
N_PROJ = 12288
N_CTX_PROJ = 3072

LANES = 128
VMEM_LIMIT = 56 * 1024 * 1024


def _dot(a, b):
    return jnp.dot(a, b, preferred_element_type=F32)


def _silu(x):
    return x * jax.nn.sigmoid(x)


def _softplus(x):
    return jnp.maximum(x, 0.0) + jnp.log1p(jnp.exp(-jnp.abs(x)))


def _params(sem):
    return pltpu.CompilerParams(dimension_semantics=sem, vmem_limit_bytes=VMEM_LIMIT)


def _ada_kernel(c_ref, w_ref, b_ref, o_ref):
    s_hi, s_lo = _split_bf16(_silu(c_ref[...]), 2)
    lhs = jnp.concatenate([s_hi, s_lo], axis=0).astype(BF16)
    w = w_ref[...]
    w_hi = w.astype(BF16)
    w_lo = (w - w_hi.astype(F32)).astype(BF16)
    r = _dot(lhs, w_hi) + _dot(lhs, w_lo)
    o_ref[...] = r[0:8] + r[8:16] + b_ref[...]


def _ada(cs, w, b):
    tn = 1024
    n = w.shape[1]
    return pl.pallas_call(
        _ada_kernel,
        grid=(n // tn,),
        in_specs=[pl.BlockSpec((8, D_MODEL), lambda j: (0, 0)),
                  pl.BlockSpec((D_MODEL, tn), lambda j: (0, j)),
                  pl.BlockSpec((1, tn), lambda j: (0, j))],
        out_specs=pl.BlockSpec((8, tn), lambda j: (0, j)),
        out_shape=jax.ShapeDtypeStruct((8, n), F32),
        compiler_params=_params(("arbitrary",)),
        name="ada",
    )(cs, w, b)


def _modulate_rows(x_ref, nw, sc, sh, out_ref, rows, rc=256):
    def body(k, carry):
        r = pl.multiple_of(k * rc, rc)
        x = x_ref[pl.ds(r, rc), :]
        ms = jnp.mean(x * x, axis=-1, keepdims=True)
        y = x * lax.rsqrt(ms + EPS) * nw
        out_ref[pl.ds(r, rc), :] = (y * (1.0 + sc) + sh).astype(out_ref.dtype)
        return carry
    lax.fori_loop(0, rows // rc, body, 0)


def _inproj_kernel(x_ref, sh_ref, sc_ref, nw_ref, w_ref, wdt_ref, o_ref, dt_ref, h_scr):
    @pl.when(pl.program_id(1) == 0)
    def _():
        _modulate_rows(x_ref, nw_ref[...], sc_ref[0], sh_ref[0], h_scr, h_scr.shape[0])
        dt_ref[...] = _dot(h_scr[...], wdt_ref[...].astype(BF16))

    o_ref[...] = _dot(h_scr[...], w_ref[...]).astype(o_ref.dtype)


def _inproj(x2d, sh, sc, nw, w_cat, w_dt, *, n_cols, rows_per_mod, tm=1024, tn=INPROJ_TN):
    m = x2d.shape[0]
    tpb = rows_per_mod // tm
    return pl.pallas_call(
        _inproj_kernel,
        grid=(m // tm, n_cols // tn),
        in_specs=[pl.BlockSpec((tm, D_MODEL), lambda i, j: (i, 0)),
                  pl.BlockSpec((1, 1, D_MODEL), lambda i, j: (i // tpb, 0, 0)),
                  pl.BlockSpec((1, 1, D_MODEL), lambda i, j: (i // tpb, 0, 0)),
                  pl.BlockSpec((1, D_MODEL), lambda i, j: (0, 0)),
                  pl.BlockSpec((D_MODEL, tn), lambda i, j: (0, j)),
                  pl.BlockSpec((D_MODEL, LANES), lambda i, j: (0, 0))],
        out_specs=[pl.BlockSpec((tm, tn), lambda i, j: (i, j)),
                   pl.BlockSpec((tm, LANES), lambda i, j: (i, 0))],
        out_shape=[jax.ShapeDtypeStruct((m, n_cols), BF16),
                   jax.ShapeDtypeStruct((m, LANES), F32)],
        scratch_shapes=[pltpu.VMEM((tm, D_MODEL), BF16)],
        compiler_params=_params(("arbitrary", "arbitrary")),
        name="inproj",
    )(x2d, sh, sc, nw, w_cat, w_dt)


def _conv_silu_tile(ref, c, n_chunks, w, bias):
    length = n_chunks * CHUNK
    if isinstance(c, int):
        r0, p0, n0 = c * CHUNK, max(c * CHUNK - 16, 0), min(c * CHUNK + CHUNK, length - 16)
    else:
        r0 = pl.multiple_of(c * CHUNK, CHUNK)
        p0 = pl.multiple_of(jnp.maximum(c * CHUNK - 16, 0), 16)
        n0 = pl.multiple_of(jnp.minimum(c * CHUNK + CHUNK, length - 16), 16)
    x = ref[0, pl.ds(r0, CHUNK), :].astype(F32)
    prev_row = ref[0, pl.ds(p0, 16), :].astype(F32)[15:16, :]
    prev_row = jnp.where(c > 0, prev_row, 0.0)
    next_row = ref[0, pl.ds(n0, 16), :].astype(F32)[0:1, :]
    next_row = jnp.where(c < n_chunks - 1, next_row, 0.0)
    rows = lax.broadcasted_iota(jnp.int32, x.shape, 0)
    xm = jnp.where(rows == 0, prev_row, pltpu.roll(x, 1, axis=0))
    xp = jnp.where(rows == CHUNK - 1, next_row, pltpu.roll(x, CHUNK - 1, axis=0))
    y = xm * w[0:1, :] + x * w[1:2, :] + xp * w[2:3, :] + bias
    return _silu(y)


def _decay_rows(dt_raw, dt_bias, a_neg, tri, causal):
    dt = _softplus(dt_raw + dt_bias)
    a = dt * a_neg
    hi = a.astype(BF16)
    lo = (a - hi.astype(F32)).astype(BF16)
    r = _dot(jnp.concatenate([hi, lo], axis=0), tri)
    acs = r[0:8] + r[8:16]
    tot = acs[:, CHUNK - 1:CHUNK] if causal else acs[:, 0:1]
    return acs, dt, jnp.exp(tot - acs) * dt, jnp.exp(tot)


def _split_bf16(x, parts):
    out = []
    for _ in range(parts - 1):
        hi = x.astype(BF16).astype(F32)
        out.append(hi)
        x = x - hi
    out.append(x.astype(BF16).astype(F32))
    return out


TAB_ACS, TAB_ONE, TAB_EXP, TAB_W, TAB_DEC, TAB_USED = 0, 3, 6, 8, 10, 12
SEG_ROWS = 48


def _decay_tables(dtr_ref, dt_bias, a_neg, tri_ref, a_scr, bsel_scr, dt_scr):
    n = N_CHUNKS * 8
    dt = _softplus(dtr_ref[0, 0] + dt_bias)
    a = (dt * a_neg).reshape(n, CHUNK)
    dt = dt.reshape(n, CHUNK)
    hi, lo = _split_bf16(a, 2)
    lhs = jnp.concatenate([hi, lo], axis=0).astype(BF16)
    pre = _dot(lhs, tri_ref[0])
    suf = _dot(lhs, tri_ref[1])
    row = lax.broadcasted_iota(jnp.int32, (n, CHUNK), 0)
    fwd = jnp.bitwise_and(row, 7) < SSD_REP
    acs = jnp.where(fwd, pre[:n] + pre[n:], suf[:n] + suf[n:])
    tot = jnp.where(fwd[:, 0:1], acs[:, CHUNK - 1:CHUNK], acs[:, 0:1])
    ones = jnp.ones((n, CHUNK), F32)
    comps_a = (_split_bf16(acs, 3) + [ones, ones, ones] + _split_bf16(jnp.exp(acs), 2)
               + _split_bf16(jnp.exp(tot - acs) * dt, 2)
               + _split_bf16(jnp.broadcast_to(jnp.exp(tot), acs.shape), 2))
    comps_b = [ones, ones, ones] + [-p for p in _split_bf16(acs, 3)]
    row48 = jnp.bitwise_and(lax.broadcasted_iota(jnp.int32, (SEG_ROWS, CHUNK), 0), 7)
    pad = jnp.zeros((CHUNK - 8 * TAB_USED, CHUNK), F32)
    for c in range(N_CHUNKS):
        sl = slice(c * 8, c * 8 + 8)
        a_scr[c] = jnp.concatenate([p[sl] for p in comps_a] + [pad], axis=0).T.astype(BF16)
        b_all = jnp.concatenate([p[sl] for p in comps_b], axis=0)
        for k in range(8):
            bsel_scr[c, :, k * CHUNK:(k + 1) * CHUNK] = jnp.where(row48 == k, b_all, 0.0).astype(BF16)
        dt_scr[c] = dt[sl]


def _head_lanes(dec, off):
    lane = lax.broadcasted_iota(jnp.int32, (1, LANES), 1)
    t0 = jnp.where(lane < SSD_HEAD_DIM, dec[off:off + 1, :], dec[off + 1:off + 2, :])
    t1 = jnp.where(lane < SSD_HEAD_DIM, dec[off + 2:off + 3, :], dec[off + 3:off + 4, :])
    return jnp.concatenate([t0, t1], axis=1)


def _split_heads(t):
    lane = lax.broadcasted_iota(jnp.int32, t.shape, 1)
    return (jnp.where(lane < SSD_HEAD_DIM, t, 0.0).astype(BF16),
            jnp.where(lane >= SSD_HEAD_DIM, t, 0.0).astype(BF16))


def _state_update(x_split, bt, w, dec, off, h_ref):
    h = h_ref[...]
    ss = []
    for pair in range(2):
        bw = [(bt * w[off + r:off + r + 1, :]).astype(BF16) for r in (2 * pair, 2 * pair + 1)]
        ss.append(_dot(jnp.concatenate(bw, axis=1), jnp.concatenate(x_split[pair], axis=0)))
    h_ref[...] = h * _head_lanes(dec, off) + jnp.concatenate(ss, axis=1)
    return h


def _chunk_dir(xs, x_pairs, bt, cc, cb, tab, bsel, dt, sel, off, h_ref, causal):
    l_io = lax.broadcasted_iota(jnp.int32, (CHUNK, CHUNK), 0)
    s_io = lax.broadcasted_iota(jnp.int32, (CHUNK, CHUNK), 1)
    mask = (l_io >= s_io) if causal else (l_io <= s_io)
    seg = _dot(tab, jnp.concatenate([bsel, jnp.zeros((CHUNK - SEG_ROWS, bsel.shape[1]), BF16)],
                                    axis=0))
    spread = _dot(tab, sel)
    e_exp, w_exp, dec = spread[:, 0:GROUP_W], spread[:, GROUP_W:2 * GROUP_W], spread[0:1, 2 * GROUP_W:]
    h = h_ref[...]
    h_ref[...] = h * dec + _dot(bt, (xs * w_exp).astype(BF16))
    y = _dot(cc, h.astype(BF16)) * e_exp
    ys = []
    for pair in range(2):
        ms = []
        for r in (2 * pair, 2 * pair + 1):
            decay = jnp.exp(jnp.where(mask, seg[:, r * CHUNK:(r + 1) * CHUNK], -jnp.inf))
            ms.append((cb * decay * dt[off + r:off + r + 1, :]).astype(BF16))
        ys.append(_dot(jnp.concatenate(ms, axis=1), x_pairs[pair]))
    return y + jnp.concatenate(ys, axis=1)


def _ssd_kernel(xs_ref, b_ref, c_ref, z_ref, dtr_ref, cxs_ref, cbm_ref, cdtr_ref,
                cwx_ref, cwb_ref, cwc_ref, bx_ref, bb_ref, bc_ref,
                dtb_ref, alog_ref, dsk_ref, nw_ref, tri_ref, sel_ref,
                y_ref,
                xs_scr, xp_scr, bt_scr, cc_scr, cb_scr, y_scr, hf_scr, hb_scr,
                tab_scr, bsel_scr, dt_scr):
    dt_bias = dtb_ref[0]
    a_neg = -jnp.exp(alog_ref[0])
    cwx, cwb, cwc = cwx_ref[...], cwb_ref[...], cwc_ref[...]
    bx, bb, bc = bx_ref[...], bb_ref[...], bc_ref[...]

    hf_scr[...] = jnp.zeros(hf_scr.shape, F32)
    hb_scr[...] = jnp.zeros(hb_scr.shape, F32)
    ctx_tiles = []
    for c in range(N_CTX_CHUNKS):
        xs = _conv_silu_tile(cxs_ref, c, N_CTX_CHUNKS, cwx, bx)
        bt = _conv_silu_tile(cbm_ref, c, N_CTX_CHUNKS, cwb, bb).T
        ctx_tiles.append(([_split_heads(xs[:, p * LANES:(p + 1) * LANES]) for p in range(2)], bt))
    for c in range(N_CTX_CHUNKS):
        _, _, w, dec = _decay_rows(cdtr_ref[0, 0, c], dt_bias, a_neg, tri_ref[0], True)
        _state_update(ctx_tiles[c][0], ctx_tiles[c][1], w, dec, 0, hf_scr)
    for c in reversed(range(N_CTX_CHUNKS)):
        _, _, w, dec = _decay_rows(cdtr_ref[0, 0, c], dt_bias, a_neg, tri_ref[1], False)
        _state_update(ctx_tiles[c][0], ctx_tiles[c][1], w, dec, 4, hb_scr)

    _decay_tables(dtr_ref, dt_bias, a_neg, tri_ref, tab_scr, bsel_scr, dt_scr)

    def prepare(c):
        r0 = pl.multiple_of(c * CHUNK, CHUNK)
        xs = _conv_silu_tile(xs_ref, c, N_CHUNKS, cwx, bx)
        bt = _conv_silu_tile(b_ref, c, N_CHUNKS, cwb, bb).T.astype(BF16)
        cc = _conv_silu_tile(c_ref, c, N_CHUNKS, cwc, bc).astype(BF16)
        cb = _dot(cc, bt)
        x_pairs = [jnp.concatenate(_split_heads(xs[:, p * LANES:(p + 1) * LANES]), axis=0)
                   for p in range(2)]
        xs_scr[pl.ds(r0, CHUNK), :] = xs
        xp_scr[c, 0] = x_pairs[0]
        xp_scr[c, 1] = x_pairs[1]
        bt_scr[c] = bt
        cc_scr[c] = cc
        cb_scr[c] = cb
        return xs, x_pairs, bt, cc, cb

    def recall(c):
        r0 = pl.multiple_of(c * CHUNK, CHUNK)
        return (xs_scr[pl.ds(r0, CHUNK), :], [xp_scr[c, 0], xp_scr[c, 1]], bt_scr[c], cc_scr[c],
                cb_scr[c])

    def scan(c, tiles, causal):
        xs, x_pairs, bt, cc, cb = tiles
        d = 0 if causal else 1
        bsel = bsel_scr[c, :, d * SSD_REP * CHUNK:(d + 1) * SSD_REP * CHUNK]
        return _chunk_dir(xs, x_pairs, bt, cc, cb, tab_scr[c], bsel, dt_scr[c], sel_ref[d],
                          d * SSD_REP, hf_scr if causal else hb_scr, causal)

    d_sum = dsk_ref[0, 0:1, :] + dsk_ref[0, 1:2, :]
    nw = nw_ref[...]

    def finish(c, y_dir, xs):
        r0 = pl.multiple_of(c * CHUNK, CHUNK)
        y = y_scr[pl.ds(r0, CHUNK), :] + y_dir + xs * d_sum
        gated = y * _silu(z_ref[0, pl.ds(r0, CHUNK), :].astype(F32))
        ms = jnp.mean(gated * gated, axis=-1, keepdims=True)
        y_ref[0, pl.ds(r0, CHUNK), :] = (gated * lax.rsqrt(ms + EPS) * nw).astype(y_ref.dtype)

    def visits(i):
        fwd = [i * SCAN_UNROLL + u for u in range(SCAN_UNROLL)]
        return [(c, True) for c in fwd] + [(N_CHUNKS - 1 - c, False) for c in fwd]

    def first_half(i, carry):
        for c, causal in visits(i):
            r0 = pl.multiple_of(c * CHUNK, CHUNK)
            y_scr[pl.ds(r0, CHUNK), :] = scan(c, prepare(c), causal)
        return carry

    def second_half(i, carry):
        for c, causal in visits(i):
            tiles = recall(c)
            finish(c, scan(c, tiles, causal), tiles[0])
        return carry

    half = N_CHUNKS // 2 // SCAN_UNROLL
    lax.fori_loop(0, half, first_half, 0)
    lax.fori_loop(half, 2 * half, second_half, 0)


def _ssd(proj, dtr, cproj, cdtr, conv_w, conv_b, dtb, alog, dsk, norm_w, tri):
    gx = COL_X // GROUP_W
    gz = COL_Z // GROUP_W
    gb = COL_B // SSD_STATE
    gc = COL_C // SSD_STATE
    in_specs = [
        pl.BlockSpec((1, SEQ, GROUP_W), lambda b, g: (b, 0, gx + g)),
        pl.BlockSpec((1, SEQ, SSD_STATE), lambda b, g: (b, 0, gb + g)),
        pl.BlockSpec((1, SEQ, SSD_STATE), lambda b, g: (b, 0, gc + g)),
        pl.BlockSpec((1, SEQ, GROUP_W), lambda b, g: (b, 0, gz + g)),
        pl.BlockSpec((1, 1, N_CHUNKS, 8, CHUNK), lambda b, g: (g, b, 0, 0, 0)),
        pl.BlockSpec((1, CTX_LEN, GROUP_W), lambda b, g: (b, 0, gx + g)),
        pl.BlockSpec((1, CTX_LEN, SSD_STATE), lambda b, g: (b, 0, gb + g)),
        pl.BlockSpec((1, 1, N_CTX_CHUNKS, 8, CHUNK), lambda b, g: (g, b, 0, 0, 0)),
        pl.BlockSpec((3, GROUP_W), lambda b, g: (0, gx + g)),
        pl.BlockSpec((3, SSD_STATE), lambda b, g: (0, gb + g)),
        pl.BlockSpec((3, SSD_STATE), lambda b, g: (0, gc + g)),
        pl.BlockSpec((1, GROUP_W), lambda b, g: (0, gx + g)),
        pl.BlockSpec((1, SSD_STATE), lambda b, g: (0, gb + g)),
        pl.BlockSpec((1, SSD_STATE), lambda b, g: (0, gc + g)),
        pl.BlockSpec((1, 8, 1), lambda b, g: (g, 0, 0)),
        pl.BlockSpec((1, 8, 1), lambda b, g: (g, 0, 0)),
        pl.BlockSpec((1, 2, GROUP_W), lambda b, g: (g, 0, 0)),
        pl.BlockSpec((1, GROUP_W), lambda b, g: (0, g)),
        pl.BlockSpec((2, CHUNK, CHUNK), lambda b, g: (0, 0, 0)),
        pl.BlockSpec((2, CHUNK, 3 * GROUP_W), lambda b, g: (0, 0, 0)),
    ]
    return pl.pallas_call(
        _ssd_kernel,
        grid=(BATCH, SSD_GROUPS),
        in_specs=in_specs,
        out_specs=pl.BlockSpec((1, SEQ, GROUP_W), lambda b, g: (b, 0, g)),
        out_shape=jax.ShapeDtypeStruct((BATCH, SEQ, D_MODEL), BF16),
        scratch_shapes=[pltpu.VMEM((SEQ, GROUP_W), F32),
                        pltpu.VMEM((N_CHUNKS, 2, 2 * CHUNK, LANES), BF16),
                        pltpu.VMEM((N_CHUNKS, CHUNK, CHUNK), BF16),
                        pltpu.VMEM((N_CHUNKS, CHUNK, CHUNK), BF16),
                        pltpu.VMEM((N_CHUNKS, CHUNK, CHUNK), F32),
                        pltpu.VMEM((SEQ, GROUP_W), F32),
                        pltpu.VMEM((SSD_STATE, GROUP_W), F32),
                        pltpu.VMEM((SSD_STATE, GROUP_W), F32),
                        pltpu.VMEM((N_CHUNKS, CHUNK, CHUNK), BF16),
                        pltpu.VMEM((N_CHUNKS, SEG_ROWS, 8 * CHUNK), BF16),
                        pltpu.VMEM((N_CHUNKS, 8, CHUNK), F32)],
        compiler_params=_params(("arbitrary", "arbitrary")),
        name="ssd",
    )(proj, proj, proj, proj, dtr, cproj, cproj, cdtr,
      conv_w, conv_w, conv_w, conv_b, conv_b, conv_b, dtb, alog, dsk, norm_w, tri, _spread_matrix())


def _spread_matrix():
    sel = np.zeros((2, CHUNK, 3 * GROUP_W), np.float32)
    for d in range(2):
        for grp, comp in enumerate((TAB_EXP, TAB_W, TAB_DEC)):
            for part in range(2):
                for h in range(SSD_REP):
                    lanes = slice(grp * GROUP_W + h * SSD_HEAD_DIM,
                                  grp * GROUP_W + (h + 1) * SSD_HEAD_DIM)
                    sel[d, (comp + part) * 8 + d * SSD_REP + h, lanes] = 1.0
    return jnp.asarray(sel, BF16)


def _pool_kernel(v_ref, win_ref, inv_ref, pw_ref, ps_ref, o_ref, p_scr):
    slab = win_ref.shape[1]
    for r in range(0, v_ref.shape[0], slab):
        v = v_ref[r:r + slab, :]
        p_scr[r:r + slab, :] = (_dot(win_ref[0], v) * inv_ref[0] - v.astype(F32)).astype(BF16)
    o_ref[...] = (_dot(p_scr[...], pw_ref[0]) * ps_ref[...]).astype(o_ref.dtype)


def _pool_constants(tm):
    t = np.arange(tm)
    col = t % GRID_W
    wins, invs = [], []
    for w in POOL_WINDOWS:
        start = np.clip(col - w // 2, 0, GRID_W)
        end = np.clip(col + w - w // 2, 0, GRID_W)
        same_row = (t[:, None] // GRID_W) == (t[None, :] // GRID_W)
        inside = (col[None, :] >= start[:, None]) & (col[None, :] < end[:, None])
        wins.append((same_row & inside).astype(np.float32))
        invs.append((1.0 / (end - start).astype(np.float32))[:, None])
    return jnp.asarray(np.stack(wins), BF16), jnp.asarray(np.stack(invs), F32)


def _pool(proj2d, pool_w, pool_scale, tm=2048, slab=128):
    win, inv = _pool_constants(slab)
    gv = COL_V // POOL_GW
    return pl.pallas_call(
        _pool_kernel,
        grid=(TOKENS // tm, len(POOL_WINDOWS)),
        in_specs=[pl.BlockSpec((tm, POOL_GW), lambda i, k: (i, gv + k)),
                  pl.BlockSpec((1, slab, slab), lambda i, k: (k, 0, 0)),
                  pl.BlockSpec((1, slab, 1), lambda i, k: (k, 0, 0)),
                  pl.BlockSpec((1, POOL_GW, POOL_GW), lambda i, k: (k, 0, 0)),
                  pl.BlockSpec((1, POOL_GW), lambda i, k: (0, k))],
        out_specs=pl.BlockSpec((tm, POOL_GW), lambda i, k: (i, k)),
        out_shape=jax.ShapeDtypeStruct((TOKENS, D_MODEL), BF16),
        scratch_shapes=[pltpu.VMEM((tm, POOL_GW), BF16)],
        compiler_params=_params(("arbitrary", "arbitrary")),
        name="pool",
    )(proj2d, win, inv, pool_w, pool_scale)


def _merge_kernel(ys_ref, yp_ref, ws_ref, wp_ref, gs_ref, gp_ref, o_ref):
    a = jax.nn.sigmoid(gs_ref[...].astype(F32)) * _dot(ys_ref[...], ws_ref[...])
    b = jax.nn.sigmoid(gp_ref[...].astype(F32)) * _dot(yp_ref[...], wp_ref[...])
    o_ref[...] = (a + b).astype(o_ref.dtype)


def _merge(y_ssd, y_pool, w_s, w_p, proj2d, tm=1024, tn=MERGE_TN):
    ggs = COL_GS // tn
    ggp = COL_GP // tn
    return pl.pallas_call(
        _merge_kernel,
        grid=(TOKENS // tm, D_MODEL // tn),
        in_specs=[pl.BlockSpec((tm, D_MODEL), lambda i, j: (i, 0)),
                  pl.BlockSpec((tm, D_MODEL), lambda i, j: (i, 0)),
                  pl.BlockSpec((D_MODEL, tn), lambda i, j: (0, j)),
                  pl.BlockSpec((D_MODEL, tn), lambda i, j: (0, j)),
                  pl.BlockSpec((tm, tn), lambda i, j: (i, ggs + j)),
                  pl.BlockSpec((tm, tn), lambda i, j: (i, ggp + j))],
        out_specs=pl.BlockSpec((tm, tn), lambda i, j: (i, j)),
        out_shape=jax.ShapeDtypeStruct((TOKENS, D_MODEL), BF16),
        compiler_params=_params(("arbitrary", "arbitrary")),
        name="merge",
    )(y_ssd, y_pool, w_s, w_p, proj2d, proj2d)


def _outproj_kernel(m_ref, wo_ref, x_ref, g1_ref, sh_ref, sc_ref, nw_ref, x1_ref, h2_ref):
    x1_ref[...] = x_ref[...] + g1_ref[0] * _dot(m_ref[...], wo_ref[...])
    _modulate_rows(x1_ref, nw_ref[...], sc_ref[0], sh_ref[0], h2_ref, x1_ref.shape[0], rc=128)


def _outproj(merged, w_o, x2d, g1, sh2, sc2, nw2, tm=512):
    tpb = SEQ // tm
    mod_spec = pl.BlockSpec((1, 1, D_MODEL), lambda i: (i // tpb, 0, 0))
    return pl.pallas_call(
        _outproj_kernel,
        grid=(TOKENS // tm,),
        in_specs=[pl.BlockSpec((tm, D_MODEL), lambda i: (i, 0)),
                  pl.BlockSpec((D_MODEL, D_MODEL), lambda i: (0, 0), pipeline_mode=pl.Buffered(1)),
                  pl.BlockSpec((tm, D_MODEL), lambda i: (i, 0)),
                  mod_spec, mod_spec, mod_spec,
                  pl.BlockSpec((1, D_MODEL), lambda i: (0, 0))],
        out_specs=[pl.BlockSpec((tm, D_MODEL), lambda i: (i, 0)),
                   pl.BlockSpec((tm, D_MODEL), lambda i: (i, 0))],
        out_shape=[jax.ShapeDtypeStruct((TOKENS, D_MODEL), F32),
                   jax.ShapeDtypeStruct((TOKENS, D_MODEL), BF16)],
        compiler_params=_params(("arbitrary",)),
        name="outproj",
    )(merged, w_o, x2d, g1, sh2, sc2, nw2)


def _glu_kernel(h_ref, hp_ref, hn_ref, wa_ref, wg_ref, fw_ref, fb_ref, u_ref,
                hext_scr, a_even, g_even, a_odd, g_odd, *, n_chunks, n_steps):
    s = pl.program_id(0)
    tm = h_ref.shape[0]
    cur = jnp.minimum(s, n_steps - 2)
    i = cur // n_chunks

    @pl.when(s == 0)
    def _():
        a_odd[...] = jnp.zeros(a_odd.shape, F32)
        g_odd[...] = jnp.zeros(g_odd.shape, F32)

    @pl.when(cur % n_chunks == 0)
    def _():
        has_prev = ((i * tm) % SEQ) != 0
        has_next = (((i + 1) * tm) % SEQ) != 0
        hext_scr[0:GRID_W, :] = jnp.where(has_prev, hp_ref[...], jnp.zeros_like(hp_ref[...]))
        hext_scr[GRID_W:GRID_W + tm, :] = h_ref[...]
        hext_scr[GRID_W + tm:, :] = jnp.where(has_next, hn_ref[...], jnp.zeros_like(hn_ref[...]))

    def step(wr_a, wr_g, rd_a, rd_g):
        fw = fw_ref[...]
        fb = fb_ref[...]
        for r in range(0, tm, GLU_ROWS):
            cv = (rd_g[r:r + GLU_ROWS, :] * fw[0:1, :] + fb
                  + rd_g[r + GRID_W:r + GRID_W + GLU_ROWS, :] * fw[1:2, :]
                  + rd_g[r + 2 * GRID_W:r + 2 * GRID_W + GLU_ROWS, :] * fw[2:3, :])
            gelu = 0.5 * cv * (1.0 + lax.erf(cv * np.float32(np.sqrt(0.5))))
            u_ref[r:r + GLU_ROWS, :] = (gelu * rd_a[r:r + GLU_ROWS, :]).astype(u_ref.dtype)
        wr_a[...] = _dot(hext_scr[GRID_W:GRID_W + tm, :], wa_ref[...])
        wr_g[...] = _dot(hext_scr[...], wg_ref[...])

    @pl.when(s % 2 == 0)
    def _():
        step(a_even, g_even, a_odd, g_odd)

    @pl.when(s % 2 == 1)
    def _():
        step(a_odd, g_odd, a_even, g_even)


def _glu(h2, w_up, ffn_conv_w, ffn_conv_b, tm=1024, tn=GLU_TN):
    n_chunks = D_FF // tn
    n_tiles = TOKENS // tm
    n_steps = n_tiles * n_chunks + 1
    hb = tm // GRID_W
    n_halo = TOKENS // GRID_W

    def cur(s):
        return jnp.minimum(s, n_steps - 2)

    def prev(s):
        return jnp.maximum(s - 1, 0)

    return pl.pallas_call(
        functools.partial(_glu_kernel, n_chunks=n_chunks, n_steps=n_steps),
        grid=(n_steps,),
        in_specs=[pl.BlockSpec((tm, D_MODEL), lambda s: (cur(s) // n_chunks, 0)),
                  pl.BlockSpec((GRID_W, D_MODEL),
                               lambda s: (jnp.maximum((cur(s) // n_chunks) * hb - 1, 0), 0)),
                  pl.BlockSpec((GRID_W, D_MODEL),
                               lambda s: (jnp.minimum((cur(s) // n_chunks + 1) * hb, n_halo - 1), 0)),
                  pl.BlockSpec((D_MODEL, tn), lambda s: (0, cur(s) % n_chunks)),
                  pl.BlockSpec((D_MODEL, tn), lambda s: (0, n_chunks + cur(s) % n_chunks)),
                  pl.BlockSpec((3, tn), lambda s: (0, prev(s) % n_chunks)),
                  pl.BlockSpec((1, tn), lambda s: (0, prev(s) % n_chunks))],
        out_specs=pl.BlockSpec((tm, tn), lambda s: (prev(s) // n_chunks, prev(s) % n_chunks)),
        out_shape=jax.ShapeDtypeStruct((TOKENS, D_FF), BF16),
        scratch_shapes=[pltpu.VMEM((tm + 2 * GRID_W, D_MODEL), BF16),
                        pltpu.VMEM((tm, tn), F32), pltpu.VMEM((tm + 2 * GRID_W, tn), F32),
                        pltpu.VMEM((tm, tn), F32), pltpu.VMEM((tm + 2 * GRID_W, tn), F32)],
        compiler_params=_params(("arbitrary",)),
        name="glu",
    )(h2, h2, h2, w_up, w_up, ffn_conv_w, ffn_conv_b)


def _down_kernel(u_ref, wd_ref, x1_ref, g2_ref, nw_ref, o_ref):
    x2 = x1_ref[...] + g2_ref[0] * _dot(u_ref[...], wd_ref[...])
    ms = jnp.mean(x2 * x2, axis=-1, keepdims=True)
    o_ref[...] = x2 * lax.rsqrt(ms + EPS) * nw_ref[...]


def _down(u, w_down, x1, g2, fnw, tm=512):
    tpb = SEQ // tm
    return pl.pallas_call(
        _down_kernel,
        grid=(TOKENS // tm,),
        in_specs=[pl.BlockSpec((tm, D_FF), lambda i: (i, 0)),
                  pl.BlockSpec((D_FF, D_MODEL), lambda i: (0, 0), pipeline_mode=pl.Buffered(1)),
                  pl.BlockSpec((tm, D_MODEL), lambda i: (i, 0)),
                  pl.BlockSpec((1, 1, D_MODEL), lambda i: (i // tpb, 0, 0)),
                  pl.BlockSpec((1, D_MODEL), lambda i: (0, 0))],
        out_specs=pl.BlockSpec((tm, D_MODEL), lambda i: (i, 0)),
        out_shape=jax.ShapeDtypeStruct((TOKENS, D_MODEL), F32),
        compiler_params=_params(("arbitrary",)),
        name="down",
    )(u, w_down, x1, g2, fnw)


def _dt_rows(dt_raw, batch, n_chunks):
    d = dt_raw[:, :2 * SSD_HEADS].reshape(batch, n_chunks, CHUNK, 2, SSD_GROUPS, SSD_REP)
    d = jnp.transpose(d, (4, 0, 1, 3, 5, 2))
    return d.reshape(SSD_GROUPS, batch, n_chunks, 2 * SSD_REP, CHUNK)


def _group_rows(p):
    p = p.reshape(2, SSD_GROUPS, SSD_REP)
    return jnp.transpose(p, (1, 0, 2)).reshape(SSD_GROUPS, 2 * SSD_REP, 1)


def kernel(x, c, ctx, c_ctx, w_ada, b_ada, norm1_w, w_in, ssd_conv_w, ssd_conv_b, dt_bias, a_log,
           d_skip, ssd_norm_w, w_ssd_out, pool_w, pool_scale, w_pool_out, w_o, norm2_w, w_up,
           ffn_conv_w, ffn_conv_b, w_down, final_norm_w):
    assert w_in.shape[0] == 1, "single-layer block only"
    d = D_MODEL
    x2d = x.reshape(TOKENS, d)
    ctx2d = ctx.reshape(BATCH * CTX_LEN, d)

    cs = jnp.zeros((8, d), F32).at[:BATCH].set(c).at[BATCH].set(c_ctx)
    mod = _ada(cs, w_ada[0], b_ada)
    sh1, sc1, g1, sh2, sc2, g2 = [mod[:BATCH, k * d:(k + 1) * d].reshape(BATCH, 1, d)
                                  for k in range(6)]
    csh1 = mod[BATCH:BATCH + 1, 0:d].reshape(1, 1, d)
    csc1 = mod[BATCH:BATCH + 1, d:2 * d].reshape(1, 1, d)

    w_in0 = w_in[0]
    n_dt = 2 * SSD_HEADS
    w_dt = jnp.pad(w_in0[:, :n_dt], ((0, 0), (0, LANES - n_dt)))
    w_cat = w_in0[:, n_dt:].astype(BF16)
    w_s = w_ssd_out[0].astype(BF16)
    w_p = w_pool_out[0].astype(BF16)
    w_ob = w_o[0].astype(BF16)
    w_upb = w_up[0].astype(BF16)
    w_db = w_down[0].astype(BF16)
    pool_wb = pool_w[0].astype(BF16)

    proj, dt_raw = _inproj(x2d, sh1, sc1, norm1_w, w_cat, w_dt, n_cols=N_PROJ, rows_per_mod=SEQ)
    cproj, cdt_raw = _inproj(ctx2d, csh1, csc1, norm1_w, w_cat, w_dt, n_cols=N_CTX_PROJ,
                             rows_per_mod=BATCH * CTX_LEN)

    dtr = _dt_rows(dt_raw, BATCH, N_CHUNKS)
    cdtr = _dt_rows(cdt_raw, BATCH, N_CTX_CHUNKS)
    dsk = jnp.repeat(jnp.transpose(d_skip[0].reshape(2, SSD_GROUPS, SSD_REP), (1, 0, 2)),
                     SSD_HEAD_DIM, axis=-1)
    idx = np.arange(CHUNK)
    tri = jnp.asarray(np.stack([idx[:, None] <= idx[None, :], idx[:, None] >= idx[None, :]])
                      .astype(np.float32), BF16)

    y_ssd = _ssd(proj.reshape(BATCH, SEQ, N_PROJ), dtr,
                 cproj.reshape(BATCH, CTX_LEN, N_CTX_PROJ), cdtr,
                 ssd_conv_w[0], ssd_conv_b, _group_rows(dt_bias[0]), _group_rows(a_log[0]),
                 dsk, ssd_norm_w, tri)
    y_pool = _pool(proj, pool_wb, pool_scale)
    merged = _merge(y_ssd.reshape(TOKENS, d), y_pool, w_s, w_p, proj)
    x1, h2 = _outproj(merged, w_ob, x2d, g1, sh2, sc2, norm2_w)
    u = _glu(h2, w_upb, ffn_conv_w[0], ffn_conv_b)
    out = _down(u, w_db, x1, g2, final_norm_w.reshape(1, d))
    return out.reshape(BATCH, SEQ, d)
```

```python
import functools

import numpy as np
import jax
import jax.numpy as jnp
from jax import lax
from jax.experimental import pallas as pl
from jax.experimental.pallas import tpu as pltpu

F32 = jnp.float32
BF16 = jnp.bfloat16

D_MODEL = 2048
BATCH = 4
SEQ = 4096
TOKENS = BATCH * SEQ
GRID_W = 64
CTX_LEN = 256

SSD_HEAD_DIM = 64
SSD_HEADS = 32
SSD_GROUPS = 8
SSD_REP = 4
SSD_STATE = 128
CHUNK = 128
GROUP_W = SSD_REP * SSD_HEAD_DIM
N_CHUNKS = SEQ // CHUNK
N_CTX_CHUNKS = CTX_LEN // CHUNK
SCAN_UNROLL = 8
GLU_ROWS = 64
INPROJ_TN, MERGE_TN, GLU_TN = 1536, 512, 512

POOL_WINDOWS = (2, 4, 8, 16)
POOL_GW = 512
D_FF = 5632
EPS = 1e-6

COL_X = 0
COL_B = 2048
COL_C = 3072
COL_Z = 4096
COL_V = 6144
COL_GS = 8192
COL_GP = 10240
N_PROJ = 12288
N_CTX_PROJ = 3072

LANES = 128
VMEM_LIMIT = 56 * 1024 * 1024


def _dot(a, b):
    return jnp.dot(a, b, preferred_element_type=F32)


def _silu(x):
    return x * jax.nn.sigmoid(x)


def _softplus(x):
    return jnp.maximum(x, 0.0) + jnp.log1p(jnp.exp(-jnp.abs(x)))


def _params(sem):
    return pltpu.CompilerParams(dimension_semantics=sem, vmem_limit_bytes=VMEM_LIMIT)


def _ada_kernel(c_ref, w_ref, b_ref, o_ref):
    s_hi, s_lo = _split_bf16(_silu(c_ref[...]), 2)
    lhs = jnp.concatenate([s_hi, s_lo], axis=0).astype(BF16)
    w = w_ref[...]
    w_hi = w.astype(BF16)
    w_lo = (w - w_hi.astype(F32)).astype(BF16)
    r = _dot(lhs, w_hi) + _dot(lhs, w_lo)
    o_ref[...] = r[0:8] + r[8:16] + b_ref[...]


def _ada(cs, w, b):
    tn = 1024
    n = w.shape[1]
    return pl.pallas_call(
        _ada_kernel,
        grid=(n // tn,),
        in_specs=[pl.BlockSpec((8, D_MODEL), lambda j: (0, 0)),
                  pl.BlockSpec((D_MODEL, tn), lambda j: (0, j)),
                  pl.BlockSpec((1, tn), lambda j: (0, j))],
        out_specs=pl.BlockSpec((8, tn), lambda j: (0, j)),
        out_shape=jax.ShapeDtypeStruct((8, n), F32),
        compiler_params=_params(("arbitrary",)),
        name="ada",
    )(cs, w, b)


def _modulate_rows(x_ref, nw, sc, sh, out_ref, rows, rc=256):
    def body(k, carry):
        r = pl.multiple_of(k * rc, rc)
        x = x_ref[pl.ds(r, rc), :]
        ms = jnp.mean(x * x, axis=-1, keepdims=True)
        y = x * lax.rsqrt(ms + EPS) * nw
        out_ref[pl.ds(r, rc), :] = (y * (1.0 + sc) + sh).astype(out_ref.dtype)
        return carry
    lax.fori_loop(0, rows // rc, body, 0)


def _inproj_kernel(x_ref, sh_ref, sc_ref, nw_ref, w_ref, wdt_ref, o_ref, dt_ref, h_scr):
    @pl.when(pl.program_id(1) == 0)
    def _():
        _modulate_rows(x_ref, nw_ref[...], sc_ref[0], sh_ref[0], h_scr, h_scr.shape[0])
        dt_ref[...] = _dot(h_scr[...], wdt_ref[...].astype(BF16))

    o_ref[...] = _dot(h_scr[...], w_ref[...]).astype(o_ref.dtype)


def _inproj(x2d, sh, sc, nw, w_cat, w_dt, *, n_cols, rows_per_mod, tm=1024, tn=INPROJ_TN):
    m = x2d.shape[0]
    tpb = rows_per_mod // tm
    return pl.pallas_call(
        _inproj_kernel,
        grid=(m // tm, n_cols // tn),
        in_specs=[pl.BlockSpec((tm, D_MODEL), lambda i, j: (i, 0)),
                  pl.BlockSpec((1, 1, D_MODEL), lambda i, j: (i // tpb, 0, 0)),
                  pl.BlockSpec((1, 1, D_MODEL), lambda i, j: (i // tpb, 0, 0)),
                  pl.BlockSpec((1, D_MODEL), lambda i, j: (0, 0)),
                  pl.BlockSpec((D_MODEL, tn), lambda i, j: (0, j)),
                  pl.BlockSpec((D_MODEL, LANES), lambda i, j: (0, 0))],
        out_specs=[pl.BlockSpec((tm, tn), lambda i, j: (i, j)),
                   pl.BlockSpec((tm, LANES), lambda i, j: (i, 0))],
        out_shape=[jax.ShapeDtypeStruct((m, n_cols), BF16),
                   jax.ShapeDtypeStruct((m, LANES), F32)],
        scratch_shapes=[pltpu.VMEM((tm, D_MODEL), BF16)],
        compiler_params=_params(("arbitrary", "arbitrary")),
        name="inproj",
    )(x2d, sh, sc, nw, w_cat, w_dt)


def _conv_silu_tile(ref, c, n_chunks, w, bias):
    length = n_chunks * CHUNK
    if isinstance(c, int):
        r0, p0, n0 = c * CHUNK, max(c * CHUNK - 16, 0), min(c * CHUNK + CHUNK, length - 16)
    else:
        r0 = pl.multiple_of(c * CHUNK, CHUNK)
        p0 = pl.multiple_of(jnp.maximum(c * CHUNK - 16, 0), 16)
        n0 = pl.multiple_of(jnp.minimum(c * CHUNK + CHUNK, length - 16), 16)
    x = ref[0, pl.ds(r0, CHUNK), :].astype(F32)
    prev_row = ref[0, pl.ds(p0, 16), :].astype(F32)[15:16, :]
    prev_row = jnp.where(c > 0, prev_row, 0.0)
    next_row = ref[0, pl.ds(n0, 16), :].astype(F32)[0:1, :]
    next_row = jnp.where(c < n_chunks - 1, next_row, 0.0)
    rows = lax.broadcasted_iota(jnp.int32, x.shape, 0)
    xm = jnp.where(rows == 0, prev_row, pltpu.roll(x, 1, axis=0))
    xp = jnp.where(rows == CHUNK - 1, next_row, pltpu.roll(x, CHUNK - 1, axis=0))
    y = xm * w[0:1, :] + x * w[1:2, :] + xp * w[2:3, :] + bias
    return _silu(y)


def _decay_rows(dt_raw, dt_bias, a_neg, tri, causal):
    dt = _softplus(dt_raw + dt_bias)
    a = dt * a_neg
    hi = a.astype(BF16)
    lo = (a - hi.astype(F32)).astype(BF16)
    r = _dot(jnp.concatenate([hi, lo], axis=0), tri)
    acs = r[0:8] + r[8:16]
    tot = acs[:, CHUNK - 1:CHUNK] if causal else acs[:, 0:1]
    return acs, dt, jnp.exp(tot - acs) * dt, jnp.exp(tot)


def _split_bf16(x, parts):
    out = []
    for _ in range(parts - 1):
        hi = x.astype(BF16).astype(F32)
        out.append(hi)
        x = x - hi
    out.append(x.astype(BF16).astype(F32))
    return out


TAB_ACS, TAB_ONE, TAB_EXP, TAB_W, TAB_DEC, TAB_USED = 0, 3, 6, 8, 10, 12
SEG_ROWS = 48


def _decay_tables(dtr_ref, dt_bias, a_neg, tri_ref, a_scr, bsel_scr, dt_scr):
    n = N_CHUNKS * 8
    dt = _softplus(dtr_ref[0, 0] + dt_bias)
    a = (dt * a_neg).reshape(n, CHUNK)
    dt = dt.reshape(n, CHUNK)
    hi, lo = _split_bf16(a, 2)
    lhs = jnp.concatenate([hi, lo], axis=0).astype(BF16)
    pre = _dot(lhs, tri_ref[0])
    suf = _dot(lhs, tri_ref[1])
    row = lax.broadcasted_iota(jnp.int32, (n, CHUNK), 0)
    fwd = jnp.bitwise_and(row, 7) < SSD_REP
    acs = jnp.where(fwd, pre[:n] + pre[n:], suf[:n] + suf[n:])
    tot = jnp.where(fwd[:, 0:1], acs[:, CHUNK - 1:CHUNK], acs[:, 0:1])
    ones = jnp.ones((n, CHUNK), F32)
    comps_a = (_split_bf16(acs, 3) + [ones, ones, ones] + _split_bf16(jnp.exp(acs), 2)
               + _split_bf16(jnp.exp(tot - acs) * dt, 2)
               + _split_bf16(jnp.broadcast_to(jnp.exp(tot), acs.shape), 2))
    comps_b = [ones, ones, ones] + [-p for p in _split_bf16(acs, 3)]
    row48 = jnp.bitwise_and(lax.broadcasted_iota(jnp.int32, (SEG_ROWS, CHUNK), 0), 7)
    pad = jnp.zeros((CHUNK - 8 * TAB_USED, CHUNK), F32)
    for c in range(N_CHUNKS):
        sl = slice(c * 8, c * 8 + 8)
        a_scr[c] = jnp.concatenate([p[sl] for p in comps_a] + [pad], axis=0).T.astype(BF16)
        b_all = jnp.concatenate([p[sl] for p in comps_b], axis=0)
        for k in range(8):
            bsel_scr[c, :, k * CHUNK:(k + 1) * CHUNK] = jnp.where(row48 == k, b_all, 0.0).astype(BF16)
        dt_scr[c] = dt[sl]


def _head_lanes(dec, off):
    lane = lax.broadcasted_iota(jnp.int32, (1, LANES), 1)
    t0 = jnp.where(lane < SSD_HEAD_DIM, dec[off:off + 1, :], dec[off + 1:off + 2, :])
    t1 = jnp.where(lane < SSD_HEAD_DIM, dec[off + 2:off + 3, :], dec[off + 3:off + 4, :])
    return jnp.concatenate([t0, t1], axis=1)


def _split_heads(t):
    lane = lax.broadcasted_iota(jnp.int32, t.shape, 1)
    return (jnp.where(lane < SSD_HEAD_DIM, t, 0.0).astype(BF16),
            jnp.where(lane >= SSD_HEAD_DIM, t, 0.0).astype(BF16))


def _state_update(x_split, bt, w, dec, off, h_ref):
    h = h_ref[...]
    ss = []
    for pair in range(2):
        bw = [(bt * w[off + r:off + r + 1, :]).astype(BF16) for r in (2 * pair, 2 * pair + 1)]
        ss.append(_dot(jnp.concatenate(bw, axis=1), jnp.concatenate(x_split[pair], axis=0)))
    h_ref[...] = h * _head_lanes(dec, off) + jnp.concatenate(ss, axis=1)
    return h


def _chunk_dir(xs, x_pairs, bt, cc, cb, tab, bsel, dt, sel, off, h_ref, causal):
    l_io = lax.broadcasted_iota(jnp.int32, (CHUNK, CHUNK), 0)
    s_io = lax.broadcasted_iota(jnp.int32, (CHUNK, CHUNK), 1)
    mask = (l_io >= s_io) if causal else (l_io <= s_io)
    seg = _dot(tab, jnp.concatenate([bsel, jnp.zeros((CHUNK - SEG_ROWS, bsel.shape[1]), BF16)],
                                    axis=0))
    spread = _dot(tab, sel)
    e_exp, w_exp, dec = spread[:, 0:GROUP_W], spread[:, GROUP_W:2 * GROUP_W], spread[0:1, 2 * GROUP_W:]
    h = h_ref[...]
    h_ref[...] = h * dec + _dot(bt, (xs * w_exp).astype(BF16))
    y = _dot(cc, h.astype(BF16)) * e_exp
    ys = []
    for pair in range(2):
        ms = []
        for r in (2 * pair, 2 * pair + 1):
            decay = jnp.exp(jnp.where(mask, seg[:, r * CHUNK:(r + 1) * CHUNK], -jnp.inf))
            ms.append((cb * decay * dt[off + r:off + r + 1, :]).astype(BF16))
        ys.append(_dot(jnp.concatenate(ms, axis=1), x_pairs[pair]))
    return y + jnp.concatenate(ys, axis=1)


def _ssd_kernel(xs_ref, b_ref, c_ref, z_ref, dtr_ref, cxs_ref, cbm_ref, cdtr_ref,
                cwx_ref, cwb_ref, cwc_ref, bx_ref, bb_ref, bc_ref,
                dtb_ref, alog_ref, dsk_ref, nw_ref, tri_ref, sel_ref,
                y_ref,
                xs_scr, xp_scr, bt_scr, cc_scr, cb_scr, y_scr, hf_scr, hb_scr,
                tab_scr, bsel_scr, dt_scr):
    dt_bias = dtb_ref[0]
    a_neg = -jnp.exp(alog_ref[0])
    cwx, cwb, cwc = cwx_ref[...], cwb_ref[...], cwc_ref[...]
    bx, bb, bc = bx_ref[...], bb_ref[...], bc_ref[...]

    hf_scr[...] = jnp.zeros(hf_scr.shape, F32)
    hb_scr[...] = jnp.zeros(hb_scr.shape, F32)
    ctx_tiles = []
    for c in range(N_CTX_CHUNKS):
        xs = _conv_silu_tile(cxs_ref, c, N_CTX_CHUNKS, cwx, bx)
        bt = _conv_silu_tile(cbm_ref, c, N_CTX_CHUNKS, cwb, bb).T
        ctx_tiles.append(([_split_heads(xs[:, p * LANES:(p + 1) * LANES]) for p in range(2)], bt))
    for c in range(N_CTX_CHUNKS):
        _, _, w, dec = _decay_rows(cdtr_ref[0, 0, c], dt_bias, a_neg, tri_ref[0], True)
        _state_update(ctx_tiles[c][0], ctx_tiles[c][1], w, dec, 0, hf_scr)
    for c in reversed(range(N_CTX_CHUNKS)):
        _, _, w, dec = _decay_rows(cdtr_ref[0, 0, c], dt_bias, a_neg, tri_ref[1], False)
        _state_update(ctx_tiles[c][0], ctx_tiles[c][1], w, dec, 4, hb_scr)

    _decay_tables(dtr_ref, dt_bias, a_neg, tri_ref, tab_scr, bsel_scr, dt_scr)

    def prepare(c):
        r0 = pl.multiple_of(c * CHUNK, CHUNK)
        xs = _conv_silu_tile(xs_ref, c, N_CHUNKS, cwx, bx)
        bt = _conv_silu_tile(b_ref, c, N_CHUNKS, cwb, bb).T.astype(BF16)
        cc = _conv_silu_tile(c_ref, c, N_CHUNKS, cwc, bc).astype(BF16)
        cb = _dot(cc, bt)
        x_pairs = [jnp.concatenate(_split_heads(xs[:, p * LANES:(p + 1) * LANES]), axis=0)
                   for p in range(2)]
        xs_scr[pl.ds(r0, CHUNK), :] = xs
        xp_scr[c, 0] = x_pairs[0]
        xp_scr[c, 1] = x_pairs[1]
        bt_scr[c] = bt
        cc_scr[c] = cc
        cb_scr[c] = cb
        return xs, x_pairs, bt, cc, cb

    def recall(c):
        r0 = pl.multiple_of(c * CHUNK, CHUNK)
        return (xs_scr[pl.ds(r0, CHUNK), :], [xp_scr[c, 0], xp_scr[c, 1]], bt_scr[c], cc_scr[c],
                cb_scr[c])

    def scan(c, tiles, causal):
        xs, x_pairs, bt, cc, cb = tiles
        d = 0 if causal else 1
        bsel = bsel_scr[c, :, d * SSD_REP * CHUNK:(d + 1) * SSD_REP * CHUNK]
        return _chunk_dir(xs, x_pairs, bt, cc, cb, tab_scr[c], bsel, dt_scr[c], sel_ref[d],
                          d * SSD_REP, hf_scr if causal else hb_scr, causal)

    d_sum = dsk_ref[0, 0:1, :] + dsk_ref[0, 1:2, :]
    nw = nw_ref[...]

    def finish(c, y_dir, xs):
        r0 = pl.multiple_of(c * CHUNK, CHUNK)
        y = y_scr[pl.ds(r0, CHUNK), :] + y_dir + xs * d_sum
        gated = y * _silu(z_ref[0, pl.ds(r0, CHUNK), :].astype(F32))
        ms = jnp.mean(gated * gated, axis=-1, keepdims=True)
        y_ref[0, pl.ds(r0, CHUNK), :] = (gated * lax.rsqrt(ms + EPS) * nw).astype(y_ref.dtype)

    def visits(i):
        fwd = [i * SCAN_UNROLL + u for u in range(SCAN_UNROLL)]
        return [(c, True) for c in fwd] + [(N_CHUNKS - 1 - c, False) for c in fwd]

    def first_half(i, carry):
        for c, causal in visits(i):
            r0 = pl.multiple_of(c * CHUNK, CHUNK)
            y_scr[pl.ds(r0, CHUNK), :] = scan(c, prepare(c), causal)
        return carry

    def second_half(i, carry):
        for c, causal in visits(i):
            tiles = recall(c)
            finish(c, scan(c, tiles, causal), tiles[0])
        return carry

    half = N_CHUNKS // 2 // SCAN_UNROLL
    lax.fori_loop(0, half, first_half, 0)
    lax.fori_loop(half, 2 * half, second_half, 0)


def _ssd(proj, dtr, cproj, cdtr, conv_w, conv_b, dtb, alog, dsk, norm_w, tri):
    gx = COL_X // GROUP_W
    gz = COL_Z // GROUP_W
    gb = COL_B // SSD_STATE
    gc = COL_C // SSD_STATE
    in_specs = [
        pl.BlockSpec((1, SEQ, GROUP_W), lambda b, g: (b, 0, gx + g)),
        pl.BlockSpec((1, SEQ, SSD_STATE), lambda b, g: (b, 0, gb + g)),
        pl.BlockSpec((1, SEQ, SSD_STATE), lambda b, g: (b, 0, gc + g)),
        pl.BlockSpec((1, SEQ, GROUP_W), lambda b, g: (b, 0, gz + g)),
        pl.BlockSpec((1, 1, N_CHUNKS, 8, CHUNK), lambda b, g: (g, b, 0, 0, 0)),
        pl.BlockSpec((1, CTX_LEN, GROUP_W), lambda b, g: (b, 0, gx + g)),
        pl.BlockSpec((1, CTX_LEN, SSD_STATE), lambda b, g: (b, 0, gb + g)),
        pl.BlockSpec((1, 1, N_CTX_CHUNKS, 8, CHUNK), lambda b, g: (g, b, 0, 0, 0)),
        pl.BlockSpec((3, GROUP_W), lambda b, g: (0, gx + g)),
        pl.BlockSpec((3, SSD_STATE), lambda b, g: (0, gb + g)),
        pl.BlockSpec((3, SSD_STATE), lambda b, g: (0, gc + g)),
        pl.BlockSpec((1, GROUP_W), lambda b, g: (0, gx + g)),
        pl.BlockSpec((1, SSD_STATE), lambda b, g: (0, gb + g)),
        pl.BlockSpec((1, SSD_STATE), lambda b, g: (0, gc + g)),
        pl.BlockSpec((1, 8, 1), lambda b, g: (g, 0, 0)),
        pl.BlockSpec((1, 8, 1), lambda b, g: (g, 0, 0)),
        pl.BlockSpec((1, 2, GROUP_W), lambda b, g: (g, 0, 0)),
        pl.BlockSpec((1, GROUP_W), lambda b, g: (0, g)),
        pl.BlockSpec((2, CHUNK, CHUNK), lambda b, g: (0, 0, 0)),
        pl.BlockSpec((2, CHUNK, 3 * GROUP_W), lambda b, g: (0, 0, 0)),
    ]
    return pl.pallas_call(
        _ssd_kernel,
        grid=(BATCH, SSD_GROUPS),
        in_specs=in_specs,
        out_specs=pl.BlockSpec((1, SEQ, GROUP_W), lambda b, g: (b, 0, g)),
        out_shape=jax.ShapeDtypeStruct((BATCH, SEQ, D_MODEL), BF16),
        scratch_shapes=[pltpu.VMEM((SEQ, GROUP_W), F32),
                        pltpu.VMEM((N_CHUNKS, 2, 2 * CHUNK, LANES), BF16),
                        pltpu.VMEM((N_CHUNKS, CHUNK, CHUNK), BF16),
                        pltpu.VMEM((N_CHUNKS, CHUNK, CHUNK), BF16),
                        pltpu.VMEM((N_CHUNKS, CHUNK, CHUNK), F32),
                        pltpu.VMEM((SEQ, GROUP_W), F32),
                        pltpu.VMEM((SSD_STATE, GROUP_W), F32),
                        pltpu.VMEM((SSD_STATE, GROUP_W), F32),
                        pltpu.VMEM((N_CHUNKS, CHUNK, CHUNK), BF16),
                        pltpu.VMEM((N_CHUNKS, SEG_ROWS, 8 * CHUNK), BF16),
                        pltpu.VMEM((N_CHUNKS, 8, CHUNK), F32)],
        compiler_params=_params(("arbitrary", "arbitrary")),
        name="ssd",
    )(proj, proj, proj, proj, dtr, cproj, cproj, cdtr,
      conv_w, conv_w, conv_w, conv_b, conv_b, conv_b, dtb, alog, dsk, norm_w, tri, _spread_matrix())


def _spread_matrix():
    sel = np.zeros((2, CHUNK, 3 * GROUP_W), np.float32)
    for d in range(2):
        for grp, comp in enumerate((TAB_EXP, TAB_W, TAB_DEC)):
            for part in range(2):
                for h in range(SSD_REP):
                    lanes = slice(grp * GROUP_W + h * SSD_HEAD_DIM,
                                  grp * GROUP_W + (h + 1) * SSD_HEAD_DIM)
                    sel[d, (comp + part) * 8 + d * SSD_REP + h, lanes] = 1.0
    return jnp.asarray(sel, BF16)


def _pool_kernel(v_ref, win_ref, inv_ref, pw_ref, ps_ref, o_ref, p_scr):
    slab = win_ref.shape[1]
    for r in range(0, v_ref.shape[0], slab):
        v = v_ref[r:r + slab, :]
        p_scr[r:r + slab, :] = (_dot(win_ref[0], v) * inv_ref[0] - v.astype(F32)).astype(BF16)
    o_ref[...] = (_dot(p_scr[...], pw_ref[0]) * ps_ref[...]).astype(o_ref.dtype)


def _pool_constants(tm):
    t = np.arange(tm)
    col = t % GRID_W
    wins, invs = [], []
    for w in POOL_WINDOWS:
        start = np.clip(col - w // 2, 0, GRID_W)
        end = np.clip(col + w - w // 2, 0, GRID_W)
        same_row = (t[:, None] // GRID_W) == (t[None, :] // GRID_W)
        inside = (col[None, :] >= start[:, None]) & (col[None, :] < end[:, None])
        wins.append((same_row & inside).astype(np.float32))
        invs.append((1.0 / (end - start).astype(np.float32))[:, None])
    return jnp.asarray(np.stack(wins), BF16), jnp.asarray(np.stack(invs), F32)


def _pool(proj2d, pool_w, pool_scale, tm=2048, slab=128):
    win, inv = _pool_constants(slab)
    gv = COL_V // POOL_GW
    return pl.pallas_call(
        _pool_kernel,
        grid=(TOKENS // tm, len(POOL_WINDOWS)),
        in_specs=[pl.BlockSpec((tm, POOL_GW), lambda i, k: (i, gv + k)),
                  pl.BlockSpec((1, slab, slab), lambda i, k: (k, 0, 0)),
                  pl.BlockSpec((1, slab, 1), lambda i, k: (k, 0, 0)),
                  pl.BlockSpec((1, POOL_GW, POOL_GW), lambda i, k: (k, 0, 0)),
                  pl.BlockSpec((1, POOL_GW), lambda i, k: (0, k))],
        out_specs=pl.BlockSpec((tm, POOL_GW), lambda i, k: (i, k)),
        out_shape=jax.ShapeDtypeStruct((TOKENS, D_MODEL), BF16),
        scratch_shapes=[pltpu.VMEM((tm, POOL_GW), BF16)],
        compiler_params=_params(("arbitrary", "arbitrary")),
        name="pool",
    )(proj2d, win, inv, pool_w, pool_scale)


def _merge_kernel(ys_ref, yp_ref, ws_ref, wp_ref, gs_ref, gp_ref, o_ref):
    a = jax.nn.sigmoid(gs_ref[...].astype(F32)) * _dot(ys_ref[...], ws_ref[...])
    b = jax.nn.sigmoid(gp_ref[...].astype(F32)) * _dot(yp_ref[...], wp_ref[...])
    o_ref[...] = (a + b).astype(o_ref.dtype)


def _merge(y_ssd, y_pool, w_s, w_p, proj2d, tm=1024, tn=MERGE_TN):
    ggs = COL_GS // tn
    ggp = COL_GP // tn
    return pl.pallas_call(
        _merge_kernel,
        grid=(TOKENS // tm, D_MODEL // tn),
        in_specs=[pl.BlockSpec((tm, D_MODEL), lambda i, j: (i, 0)),
                  pl.BlockSpec((tm, D_MODEL), lambda i, j: (i, 0)),
                  pl.BlockSpec((D_MODEL, tn), lambda i, j: (0, j)),
                  pl.BlockSpec((D_MODEL, tn), lambda i, j: (0, j)),
                  pl.BlockSpec((tm, tn), lambda i, j: (i, ggs + j)),
                  pl.BlockSpec((tm, tn), lambda i, j: (i, ggp + j))],
        out_specs=pl.BlockSpec((tm, tn), lambda i, j: (i, j)),
        out_shape=jax.ShapeDtypeStruct((TOKENS, D_MODEL), BF16),
        compiler_params=_params(("arbitrary", "arbitrary")),
        name="merge",
    )(y_ssd, y_pool, w_s, w_p, proj2d, proj2d)


def _outproj_kernel(m_ref, wo_ref, x_ref, g1_ref, sh_ref, sc_ref, nw_ref, x1_ref, h2_ref):
    x1_ref[...] = x_ref[...] + g1_ref[0] * _dot(m_ref[...], wo_ref[...])
    _modulate_rows(x1_ref, nw_ref[...], sc_ref[0], sh_ref[0], h2_ref, x1_ref.shape[0], rc=128)


def _outproj(merged, w_o, x2d, g1, sh2, sc2, nw2, tm=512):
    tpb = SEQ // tm
    mod_spec = pl.BlockSpec((1, 1, D_MODEL), lambda i: (i // tpb, 0, 0))
    return pl.pallas_call(
        _outproj_kernel,
        grid=(TOKENS // tm,),
        in_specs=[pl.BlockSpec((tm, D_MODEL), lambda i: (i, 0)),
                  pl.BlockSpec((D_MODEL, D_MODEL), lambda i: (0, 0), pipeline_mode=pl.Buffered(1)),
                  pl.BlockSpec((tm, D_MODEL), lambda i: (i, 0)),
                  mod_spec, mod_spec, mod_spec,
                  pl.BlockSpec((1, D_MODEL), lambda i: (0, 0))],
        out_specs=[pl.BlockSpec((tm, D_MODEL), lambda i: (i, 0)),
                   pl.BlockSpec((tm, D_MODEL), lambda i: (i, 0))],
        out_shape=[jax.ShapeDtypeStruct((TOKENS, D_MODEL), F32),
                   jax.ShapeDtypeStruct((TOKENS, D_MODEL), BF16)],
        compiler_params=_params(("arbitrary",)),
        name="outproj",
    )(merged, w_o, x2d, g1, sh2, sc2, nw2)


def _glu_kernel(h_ref, hp_ref, hn_ref, wa_ref, wg_ref, fw_ref, fb_ref, u_ref,
                hext_scr, a_even, g_even, a_odd, g_odd, *, n_chunks, n_steps):
    s = pl.program_id(0)
    tm = h_ref.shape[0]
    cur = jnp.minimum(s, n_steps - 2)
    i = cur // n_chunks

    @pl.when(s == 0)
    def _():
        a_odd[...] = jnp.zeros(a_odd.shape, F32)
        g_odd[...] = jnp.zeros(g_odd.shape, F32)

    @pl.when(cur % n_chunks == 0)
    def _():
        has_prev = ((i * tm) % SEQ) != 0
        has_next = (((i + 1) * tm) % SEQ) != 0
        hext_scr[0:GRID_W, :] = jnp.where(has_prev, hp_ref[...], jnp.zeros_like(hp_ref[...]))
        hext_scr[GRID_W:GRID_W + tm, :] = h_ref[...]
        hext_scr[GRID_W + tm:, :] = jnp.where(has_next, hn_ref[...], jnp.zeros_like(hn_ref[...]))

    def step(wr_a, wr_g, rd_a, rd_g):
        fw = 0.5 * fw_ref[...]
        fb = 0.5 * fb_ref[...]
        for r in range(0, tm, GLU_ROWS):
            hv = (rd_g[r:r + GLU_ROWS, :] * fw[0:1, :] + fb
                  + rd_g[r + GRID_W:r + GRID_W + GLU_ROWS, :] * fw[1:2, :]
                  + rd_g[r + 2 * GRID_W:r + 2 * GRID_W + GLU_ROWS, :] * fw[2:3, :])
            gelu = hv * (1.0 + lax.erf(hv * np.float32(np.sqrt(2.0))))
            u_ref[r:r + GLU_ROWS, :] = (gelu * rd_a[r:r + GLU_ROWS, :]).astype(u_ref.dtype)
        wr_a[...] = _dot(hext_scr[GRID_W:GRID_W + tm, :], wa_ref[...])
        wr_g[...] = _dot(hext_scr[...], wg_ref[...])

    @pl.when(s % 2 == 0)
    def _():
        step(a_even, g_even, a_odd, g_odd)

    @pl.when(s % 2 == 1)
    def _():
        step(a_odd, g_odd, a_even, g_even)


def _glu(h2, w_up, ffn_conv_w, ffn_conv_b, tm=2048, tn=GLU_TN):
    n_chunks = D_FF // tn
    n_tiles = TOKENS // tm
    n_steps = n_tiles * n_chunks + 1
    hb = tm // GRID_W
    n_halo = TOKENS // GRID_W

    def cur(s):
        return jnp.minimum(s, n_steps - 2)

    def prev(s):
        return jnp.maximum(s - 1, 0)

    return pl.pallas_call(
        functools.partial(_glu_kernel, n_chunks=n_chunks, n_steps=n_steps),
        grid=(n_steps,),
        in_specs=[pl.BlockSpec((tm, D_MODEL), lambda s: (cur(s) // n_chunks, 0),
                               pipeline_mode=pl.Buffered(1)),
                  pl.BlockSpec((GRID_W, D_MODEL),
                               lambda s: (jnp.maximum((cur(s) // n_chunks) * hb - 1, 0), 0)),
                  pl.BlockSpec((GRID_W, D_MODEL),
                               lambda s: (jnp.minimum((cur(s) // n_chunks + 1) * hb, n_halo - 1), 0)),
                  pl.BlockSpec((D_MODEL, tn), lambda s: (0, cur(s) % n_chunks)),
                  pl.BlockSpec((D_MODEL, tn), lambda s: (0, n_chunks + cur(s) % n_chunks)),
                  pl.BlockSpec((3, tn), lambda s: (0, prev(s) % n_chunks)),
                  pl.BlockSpec((1, tn), lambda s: (0, prev(s) % n_chunks))],
        out_specs=pl.BlockSpec((tm, tn), lambda s: (prev(s) // n_chunks, prev(s) % n_chunks)),
        out_shape=jax.ShapeDtypeStruct((TOKENS, D_FF), BF16),
        scratch_shapes=[pltpu.VMEM((tm + 2 * GRID_W, D_MODEL), BF16),
                        pltpu.VMEM((tm, tn), F32), pltpu.VMEM((tm + 2 * GRID_W, tn), F32),
                        pltpu.VMEM((tm, tn), F32), pltpu.VMEM((tm + 2 * GRID_W, tn), F32)],
        compiler_params=_params(("arbitrary",)),
        name="glu",
    )(h2, h2, h2, w_up, w_up, ffn_conv_w, ffn_conv_b)


def _down_kernel(u_ref, wd_ref, x1_ref, g2_ref, nw_ref, o_ref):
    x2 = x1_ref[...] + g2_ref[0] * _dot(u_ref[...], wd_ref[...])
    ms = jnp.mean(x2 * x2, axis=-1, keepdims=True)
    o_ref[...] = x2 * lax.rsqrt(ms + EPS) * nw_ref[...]


def _down(u, w_down, x1, g2, fnw, tm=512):
    tpb = SEQ // tm
    return pl.pallas_call(
        _down_kernel,
        grid=(TOKENS // tm,),
        in_specs=[pl.BlockSpec((tm, D_FF), lambda i: (i, 0)),
                  pl.BlockSpec((D_FF, D_MODEL), lambda i: (0, 0), pipeline_mode=pl.Buffered(1)),
                  pl.BlockSpec((tm, D_MODEL), lambda i: (i, 0)),
                  pl.BlockSpec((1, 1, D_MODEL), lambda i: (i // tpb, 0, 0)),
                  pl.BlockSpec((1, D_MODEL), lambda i: (0, 0))],
        out_specs=pl.BlockSpec((tm, D_MODEL), lambda i: (i, 0)),
        out_shape=jax.ShapeDtypeStruct((TOKENS, D_MODEL), F32),
        compiler_params=_params(("arbitrary",)),
        name="down",
    )(u, w_down, x1, g2, fnw)


def _dt_rows(dt_raw, batch, n_chunks):
    d = dt_raw[:, :2 * SSD_HEADS].reshape(batch, n_chunks, CHUNK, 2, SSD_GROUPS, SSD_REP)
    d = jnp.transpose(d, (4, 0, 1, 3, 5, 2))
    return d.reshape(SSD_GROUPS, batch, n_chunks, 2 * SSD_REP, CHUNK)


def _group_rows(p):
    p = p.reshape(2, SSD_GROUPS, SSD_REP)
    return jnp.transpose(p, (1, 0, 2)).reshape(SSD_GROUPS, 2 * SSD_REP, 1)


def kernel(x, c, ctx, c_ctx, w_ada, b_ada, norm1_w, w_in, ssd_conv_w, ssd_conv_b, dt_bias, a_log,
           d_skip, ssd_norm_w, w_ssd_out, pool_w, pool_scale, w_pool_out, w_o, norm2_w, w_up,
           ffn_conv_w, ffn_conv_b, w_down, final_norm_w):
    assert w_in.shape[0] == 1, "single-layer block only"
    d = D_MODEL
    x2d = x.reshape(TOKENS, d)
    ctx2d = ctx.reshape(BATCH * CTX_LEN, d)

    cs = jnp.zeros((8, d), F32).at[:BATCH].set(c).at[BATCH].set(c_ctx)
    mod = _ada(cs, w_ada[0], b_ada)
    sh1, sc1, g1, sh2, sc2, g2 = [mod[:BATCH, k * d:(k + 1) * d].reshape(BATCH, 1, d)
                                  for k in range(6)]
    csh1 = mod[BATCH:BATCH + 1, 0:d].reshape(1, 1, d)
    csc1 = mod[BATCH:BATCH + 1, d:2 * d].reshape(1, 1, d)

    w_in0 = w_in[0]
    n_dt = 2 * SSD_HEADS
    w_dt = jnp.pad(w_in0[:, :n_dt], ((0, 0), (0, LANES - n_dt)))
    w_cat = w_in0[:, n_dt:].astype(BF16)
    w_s = w_ssd_out[0].astype(BF16)
    w_p = w_pool_out[0].astype(BF16)
    w_ob = w_o[0].astype(BF16)
    w_upb = w_up[0].astype(BF16)
    w_db = w_down[0].astype(BF16)
    pool_wb = pool_w[0].astype(BF16)

    proj, dt_raw = _inproj(x2d, sh1, sc1, norm1_w, w_cat, w_dt, n_cols=N_PROJ, rows_per_mod=SEQ)
    cproj, cdt_raw = _inproj(ctx2d, csh1, csc1, norm1_w, w_cat, w_dt, n_cols=N_CTX_PROJ,
                             rows_per_mod=BATCH * CTX_LEN)

    dtr = _dt_rows(dt_raw, BATCH, N_CHUNKS)
    cdtr = _dt_rows(cdt_raw, BATCH, N_CTX_CHUNKS)
    dsk = jnp.repeat(jnp.transpose(d_skip[0].reshape(2, SSD_GROUPS, SSD_REP), (1, 0, 2)),
                     SSD_HEAD_DIM, axis=-1)
    idx = np.arange(CHUNK)
    tri = jnp.asarray(np.stack([idx[:, None] <= idx[None, :], idx[:, None] >= idx[None, :]])
                      .astype(np.float32), BF16)

    y_ssd = _ssd(proj.reshape(BATCH, SEQ, N_PROJ), dtr,
                 cproj.reshape(BATCH, CTX_LEN, N_CTX_PROJ), cdtr,
                 ssd_conv_w[0], ssd_conv_b, _group_rows(dt_bias[0]), _group_rows(a_log[0]),
                 dsk, ssd_norm_w, tri)
    y_pool = _pool(proj, pool_wb, pool_scale)
    merged = _merge(y_ssd.reshape(TOKENS, d), y_pool, w_s, w_p, proj)
    x1, h2 = _outproj(merged, w_ob, x2d, g1, sh2, sc2, norm2_w)
    u = _glu(h2, w_upb, ffn_conv_w[0], ffn_conv_b)
    out = _down(u, w_db, x1, g2, final_norm_w.reshape(1, d))
    return out.reshape(BATCH, SEQ, d)
```

```python
import functools

import numpy as np
import jax
import jax.numpy as jnp
from jax import lax
from jax.experimental import pallas as pl
from jax.experimental.pallas import tpu as pltpu

F32 = jnp.float32
BF16 = jnp.bfloat16

D_MODEL = 2048
BATCH = 4
SEQ = 4096
TOKENS = BATCH * SEQ
GRID_W = 64
CTX_LEN = 256

SSD_HEAD_DIM = 64
SSD_HEADS = 32
SSD_GROUPS = 8
SSD_REP = 4
SSD_STATE = 128
CHUNK = 128
GROUP_W = SSD_REP * SSD_HEAD_DIM
N_CHUNKS = SEQ // CHUNK
N_CTX_CHUNKS = CTX_LEN // CHUNK
SCAN_UNROLL = 16
GLU_ROWS = 64
INPROJ_TN, MERGE_TN, GLU_TN = 1536, 512, 512

POOL_WINDOWS = (2, 4, 8, 16)
POOL_GW = 512
D_FF = 5632
EPS = 1e-6

COL_X = 0
COL_B = 2048
COL_C = 3072
COL_Z = 4096
COL_V = 6144
COL_GS = 8192
COL_GP = 10240
N_PROJ = 12288
N_CTX_PROJ = 3072

LANES = 128
VMEM_LIMIT = 56 * 1024 * 1024


def _dot(a, b):
    return jnp.dot(a, b, preferred_element_type=F32)


def _silu(x):
    return x * jax.nn.sigmoid(x)


def _softplus(x):
    return jnp.maximum(x, 0.0) + jnp.log1p(jnp.exp(-jnp.abs(x)))


def _params(sem):
    return pltpu.CompilerParams(dimension_semantics=sem, vmem_limit_bytes=VMEM_LIMIT)


def _ada_kernel(c_ref, w_ref, b_ref, o_ref):
    s_hi, s_lo = _split_bf16(_silu(c_ref[...]), 2)
    lhs = jnp.concatenate([s_hi, s_lo], axis=0).astype(BF16)
    w = w_ref[...]
    w_hi = w.astype(BF16)
    w_lo = (w - w_hi.astype(F32)).astype(BF16)
    r = _dot(lhs, w_hi) + _dot(lhs, w_lo)
    o_ref[...] = r[0:8] + r[8:16] + b_ref[...]


def _ada(cs, w, b):
    tn = 1024
    n = w.shape[1]
    return pl.pallas_call(
        _ada_kernel,
        grid=(n // tn,),
        in_specs=[pl.BlockSpec((8, D_MODEL), lambda j: (0, 0)),
                  pl.BlockSpec((D_MODEL, tn), lambda j: (0, j)),
                  pl.BlockSpec((1, tn), lambda j: (0, j))],
        out_specs=pl.BlockSpec((8, tn), lambda j: (0, j)),
        out_shape=jax.ShapeDtypeStruct((8, n), F32),
        compiler_params=_params(("arbitrary",)),
        name="ada",
    )(cs, w, b)


def _modulate_rows(x_ref, nw, sc, sh, out_ref, rows, rc=256):
    gain = nw * (1.0 + sc)

    def body(k, carry):
        r = pl.multiple_of(k * rc, rc)
        x = x_ref[pl.ds(r, rc), :]
        inv = lax.rsqrt(jnp.mean(x * x, axis=-1, keepdims=True) + EPS)
        out_ref[pl.ds(r, rc), :] = (x_ref[pl.ds(r, rc), :] * inv * gain + sh).astype(out_ref.dtype)
        return carry
    lax.fori_loop(0, rows // rc, body, 0)


def _inproj_kernel(x_ref, sh_ref, sc_ref, nw_ref, w_ref, wdt_ref, o_ref, dt_ref, h_scr):
    @pl.when(pl.program_id(1) == 0)
    def _():
        _modulate_rows(x_ref, nw_ref[...], sc_ref[0], sh_ref[0], h_scr, h_scr.shape[0])
        dt_ref[...] = _dot(h_scr[...], wdt_ref[...].astype(BF16))

    o_ref[...] = _dot(h_scr[...], w_ref[...]).astype(o_ref.dtype)


def _inproj(x2d, sh, sc, nw, w_cat, w_dt, *, n_cols, rows_per_mod, tm=1024, tn=INPROJ_TN):
    m = x2d.shape[0]
    tpb = rows_per_mod // tm
    return pl.pallas_call(
        _inproj_kernel,
        grid=(m // tm, n_cols // tn),
        in_specs=[pl.BlockSpec((tm, D_MODEL), lambda i, j: (i, 0)),
                  pl.BlockSpec((1, 1, D_MODEL), lambda i, j: (i // tpb, 0, 0)),
                  pl.BlockSpec((1, 1, D_MODEL), lambda i, j: (i // tpb, 0, 0)),
                  pl.BlockSpec((1, D_MODEL), lambda i, j: (0, 0)),
                  pl.BlockSpec((D_MODEL, tn), lambda i, j: (0, j)),
                  pl.BlockSpec((D_MODEL, LANES), lambda i, j: (0, 0))],
        out_specs=[pl.BlockSpec((tm, tn), lambda i, j: (i, j)),
                   pl.BlockSpec((tm, LANES), lambda i, j: (i, 0))],
        out_shape=[jax.ShapeDtypeStruct((m, n_cols), BF16),
                   jax.ShapeDtypeStruct((m, LANES), F32)],
        scratch_shapes=[pltpu.VMEM((tm, D_MODEL), BF16)],
        compiler_params=_params(("arbitrary", "arbitrary")),
        name="inproj",
    )(x2d, sh, sc, nw, w_cat, w_dt)


def _conv_silu_tile(ref, c, n_chunks, w, bias):
    length = n_chunks * CHUNK
    if isinstance(c, int):
        r0, p0, n0 = c * CHUNK, max(c * CHUNK - 16, 0), min(c * CHUNK + CHUNK, length - 16)
    else:
        r0 = pl.multiple_of(c * CHUNK, CHUNK)
        p0 = pl.multiple_of(jnp.maximum(c * CHUNK - 16, 0), 16)
        n0 = pl.multiple_of(jnp.minimum(c * CHUNK + CHUNK, length - 16), 16)
    x = ref[0, pl.ds(r0, CHUNK), :].astype(F32)
    prev_row = ref[0, pl.ds(p0, 16), :].astype(F32)[15:16, :]
    prev_row = jnp.where(c > 0, prev_row, 0.0)
    next_row = ref[0, pl.ds(n0, 16), :].astype(F32)[0:1, :]
    next_row = jnp.where(c < n_chunks - 1, next_row, 0.0)
    rows = lax.broadcasted_iota(jnp.int32, (8, x.shape[1]), 0)
    xm = pltpu.roll(x, 1, axis=0)
    xm = jnp.concatenate([jnp.where(rows == 0, prev_row, xm[0:8]), xm[8:]], axis=0)
    xp = pltpu.roll(x, CHUNK - 1, axis=0)
    xp = jnp.concatenate([xp[:CHUNK - 8], jnp.where(rows == 7, next_row, xp[CHUNK - 8:])], axis=0)
    y = xm * w[0:1, :] + x * w[1:2, :] + xp * w[2:3, :] + bias
    return _silu(y)


def _decay_rows(dt_raw, dt_bias, a_neg, tri, causal):
    dt = _softplus(dt_raw + dt_bias)
    a = dt * a_neg
    hi = a.astype(BF16)
    lo = (a - hi.astype(F32)).astype(BF16)
    r = _dot(jnp.concatenate([hi, lo], axis=0), tri)
    acs = r[0:8] + r[8:16]
    tot = acs[:, CHUNK - 1:CHUNK] if causal else acs[:, 0:1]
    return acs, dt, jnp.exp(tot - acs) * dt, jnp.exp(tot)


def _split_bf16(x, parts):
    out = []
    for _ in range(parts - 1):
        hi = x.astype(BF16).astype(F32)
        out.append(hi)
        x = x - hi
    out.append(x.astype(BF16).astype(F32))
    return out


TAB_ACS, TAB_ONE, TAB_EXP, TAB_W, TAB_DEC, TAB_USED = 0, 3, 6, 8, 10, 12
SEG_ROWS = 48


def _decay_tables(dtr_ref, dt_bias, a_neg, tri_ref, a_scr, bsel_scr, dt_scr):
    n = N_CHUNKS * 8
    dt = _softplus(dtr_ref[0, 0] + dt_bias)
    a = (dt * a_neg).reshape(n, CHUNK)
    dt = dt.reshape(n, CHUNK)
    hi, lo = _split_bf16(a, 2)
    lhs = jnp.concatenate([hi, lo], axis=0).astype(BF16)
    pre = _dot(lhs, tri_ref[0])
    suf = _dot(lhs, tri_ref[1])
    row = lax.broadcasted_iota(jnp.int32, (n, CHUNK), 0)
    fwd = jnp.bitwise_and(row, 7) < SSD_REP
    acs = jnp.where(fwd, pre[:n] + pre[n:], suf[:n] + suf[n:])
    tot = jnp.where(fwd[:, 0:1], acs[:, CHUNK - 1:CHUNK], acs[:, 0:1])
    ones = jnp.ones((n, CHUNK), F32)
    comps_a = (_split_bf16(acs, 3) + [ones, ones, ones] + _split_bf16(jnp.exp(acs), 2)
               + _split_bf16(jnp.exp(tot - acs) * dt, 2)
               + _split_bf16(jnp.broadcast_to(jnp.exp(tot), acs.shape), 2))
    comps_b = [ones, ones, ones] + [-p for p in _split_bf16(acs, 3)]
    row48 = jnp.bitwise_and(lax.broadcasted_iota(jnp.int32, (SEG_ROWS, CHUNK), 0), 7)
    pad = jnp.zeros((CHUNK - 8 * TAB_USED, CHUNK), F32)
    for c in range(N_CHUNKS):
        sl = slice(c * 8, c * 8 + 8)
        a_scr[c] = jnp.concatenate([p[sl] for p in comps_a] + [pad], axis=0).T.astype(BF16)
        b_all = jnp.concatenate([p[sl] for p in comps_b], axis=0)
        for k in range(8):
            bsel_scr[c, :, k * CHUNK:(k + 1) * CHUNK] = jnp.where(row48 == k, b_all, 0.0).astype(BF16)
        dt_scr[c] = dt[sl]


def _head_lanes(dec, off):
    lane = lax.broadcasted_iota(jnp.int32, (1, LANES), 1)
    t0 = jnp.where(lane < SSD_HEAD_DIM, dec[off:off + 1, :], dec[off + 1:off + 2, :])
    t1 = jnp.where(lane < SSD_HEAD_DIM, dec[off + 2:off + 3, :], dec[off + 3:off + 4, :])
    return jnp.concatenate([t0, t1], axis=1)


def _split_heads(t):
    lane = lax.broadcasted_iota(jnp.int32, t.shape, 1)
    return (jnp.where(lane < SSD_HEAD_DIM, t, 0.0).astype(BF16),
            jnp.where(lane >= SSD_HEAD_DIM, t, 0.0).astype(BF16))


def _state_update(x_split, bt, w, dec, off, h_ref):
    h = h_ref[...]
    ss = []
    for pair in range(2):
        bw = [(bt * w[off + r:off + r + 1, :]).astype(BF16) for r in (2 * pair, 2 * pair + 1)]
        ss.append(_dot(jnp.concatenate(bw, axis=1), jnp.concatenate(x_split[pair], axis=0)))
    h_ref[...] = h * _head_lanes(dec, off) + jnp.concatenate(ss, axis=1)
    return h


def _chunk_dir(xs, x_pairs, bt, cc, cb, tab, bsel, dt, sel, off, h_ref, causal):
    l_io = lax.broadcasted_iota(jnp.int32, (CHUNK, CHUNK), 0)
    s_io = lax.broadcasted_iota(jnp.int32, (CHUNK, CHUNK), 1)
    mask = (l_io >= s_io) if causal else (l_io <= s_io)
    seg = _dot(tab, jnp.concatenate([bsel, jnp.zeros((CHUNK - SEG_ROWS, bsel.shape[1]), BF16)],
                                    axis=0))
    spread = _dot(tab, sel)
    e_exp, w_exp, dec = spread[:, 0:GROUP_W], spread[:, GROUP_W:2 * GROUP_W], spread[0:1, 2 * GROUP_W:]
    h = h_ref[...]
    h_ref[...] = h * dec + _dot(bt, (xs * w_exp).astype(BF16))
    y = _dot(cc, h.astype(BF16)) * e_exp
    ys = []
    for pair in range(2):
        ms = []
        for r in (2 * pair, 2 * pair + 1):
            decay = jnp.exp(jnp.where(mask, seg[:, r * CHUNK:(r + 1) * CHUNK], -jnp.inf))
            ms.append((cb * decay * dt[off + r:off + r + 1, :]).astype(BF16))
        ys.append(_dot(jnp.concatenate(ms, axis=1), x_pairs[pair]))
    return y + jnp.concatenate(ys, axis=1)


def _ssd_kernel(xs_ref, b_ref, c_ref, z_ref, dtr_ref, cxs_ref, cbm_ref, cdtr_ref,
                cwx_ref, cwb_ref, cwc_ref, bx_ref, bb_ref, bc_ref,
                dtb_ref, alog_ref, dsk_ref, nw_ref, tri_ref, sel_ref,
                y_ref,
                xs_scr, xp_scr, bt_scr, cc_scr, cb_scr, y_scr, hf_scr, hb_scr,
                tab_scr, bsel_scr, dt_scr):
    dt_bias = dtb_ref[0]
    a_neg = -jnp.exp(alog_ref[0])
    cwx, cwb, cwc = cwx_ref[...], cwb_ref[...], cwc_ref[...]
    bx, bb, bc = bx_ref[...], bb_ref[...], bc_ref[...]

    hf_scr[...] = jnp.zeros(hf_scr.shape, F32)
    hb_scr[...] = jnp.zeros(hb_scr.shape, F32)
    ctx_tiles = []
    for c in range(N_CTX_CHUNKS):
        xs = _conv_silu_tile(cxs_ref, c, N_CTX_CHUNKS, cwx, bx)
        bt = _conv_silu_tile(cbm_ref, c, N_CTX_CHUNKS, cwb, bb).T
        ctx_tiles.append(([_split_heads(xs[:, p * LANES:(p + 1) * LANES]) for p in range(2)], bt))
    for c in range(N_CTX_CHUNKS):
        _, _, w, dec = _decay_rows(cdtr_ref[0, 0, c], dt_bias, a_neg, tri_ref[0], True)
        _state_update(ctx_tiles[c][0], ctx_tiles[c][1], w, dec, 0, hf_scr)
    for c in reversed(range(N_CTX_CHUNKS)):
        _, _, w, dec = _decay_rows(cdtr_ref[0, 0, c], dt_bias, a_neg, tri_ref[1], False)
        _state_update(ctx_tiles[c][0], ctx_tiles[c][1], w, dec, 4, hb_scr)

    _decay_tables(dtr_ref, dt_bias, a_neg, tri_ref, tab_scr, bsel_scr, dt_scr)

    def prepare(c):
        r0 = pl.multiple_of(c * CHUNK, CHUNK)
        xs = _conv_silu_tile(xs_ref, c, N_CHUNKS, cwx, bx)
        bt = _conv_silu_tile(b_ref, c, N_CHUNKS, cwb, bb).T.astype(BF16)
        cc = _conv_silu_tile(c_ref, c, N_CHUNKS, cwc, bc).astype(BF16)
        cb = _dot(cc, bt)
        x_pairs = [jnp.concatenate(_split_heads(xs[:, p * LANES:(p + 1) * LANES]), axis=0)
                   for p in range(2)]
        xs_scr[pl.ds(r0, CHUNK), :] = xs
        xp_scr[c, 0] = x_pairs[0]
        xp_scr[c, 1] = x_pairs[1]
        bt_scr[c] = bt
        cc_scr[c] = cc
        cb_scr[c] = cb
        return xs, x_pairs, bt, cc, cb

    def recall(c):
        r0 = pl.multiple_of(c * CHUNK, CHUNK)
        return (xs_scr[pl.ds(r0, CHUNK), :], [xp_scr[c, 0], xp_scr[c, 1]], bt_scr[c], cc_scr[c],
                cb_scr[c])

    def scan(c, tiles, causal):
        xs, x_pairs, bt, cc, cb = tiles
        d = 0 if causal else 1
        bsel = bsel_scr[c, :, d * SSD_REP * CHUNK:(d + 1) * SSD_REP * CHUNK]
        return _chunk_dir(xs, x_pairs, bt, cc, cb, tab_scr[c], bsel, dt_scr[c], sel_ref[d],
                          d * SSD_REP, hf_scr if causal else hb_scr, causal)

    d_sum = dsk_ref[0, 0:1, :] + dsk_ref[0, 1:2, :]
    nw = nw_ref[...]

    def finish(c, y_dir, xs):
        r0 = pl.multiple_of(c * CHUNK, CHUNK)
        y = y_scr[pl.ds(r0, CHUNK), :] + y_dir + xs * d_sum
        gated = y * _silu(z_ref[0, pl.ds(r0, CHUNK), :].astype(F32))
        ms = jnp.mean(gated * gated, axis=-1, keepdims=True)
        y_ref[0, pl.ds(r0, CHUNK), :] = (gated * lax.rsqrt(ms + EPS) * nw).astype(y_ref.dtype)

    def visits(i):
        fwd = [i * SCAN_UNROLL + u for u in range(SCAN_UNROLL)]
        return [(c, True) for c in fwd] + [(N_CHUNKS - 1 - c, False) for c in fwd]

    def first_half(i, carry):
        for c, causal in visits(i):
            r0 = pl.multiple_of(c * CHUNK, CHUNK)
            y_scr[pl.ds(r0, CHUNK), :] = scan(c, prepare(c), causal)
        return carry

    def second_half(i, carry):
        for c, causal in visits(i):
            tiles = recall(c)
            finish(c, scan(c, tiles, causal), tiles[0])
        return carry

    half = N_CHUNKS // 2 // SCAN_UNROLL
    lax.fori_loop(0, half, first_half, 0)
    lax.fori_loop(half, 2 * half, second_half, 0)


def _ssd(proj, dtr, cproj, cdtr, conv_w, conv_b, dtb, alog, dsk, norm_w, tri):
    gx = COL_X // GROUP_W
    gz = COL_Z // GROUP_W
    gb = COL_B // SSD_STATE
    gc = COL_C // SSD_STATE
    in_specs = [
        pl.BlockSpec((1, SEQ, GROUP_W), lambda b, g: (b, 0, gx + g)),
        pl.BlockSpec((1, SEQ, SSD_STATE), lambda b, g: (b, 0, gb + g)),
        pl.BlockSpec((1, SEQ, SSD_STATE), lambda b, g: (b, 0, gc + g)),
        pl.BlockSpec((1, SEQ, GROUP_W), lambda b, g: (b, 0, gz + g)),
        pl.BlockSpec((1, 1, N_CHUNKS, 8, CHUNK), lambda b, g: (g, b, 0, 0, 0)),
        pl.BlockSpec((1, CTX_LEN, GROUP_W), lambda b, g: (b, 0, gx + g)),
        pl.BlockSpec((1, CTX_LEN, SSD_STATE), lambda b, g: (b, 0, gb + g)),
        pl.BlockSpec((1, 1, N_CTX_CHUNKS, 8, CHUNK), lambda b, g: (g, b, 0, 0, 0)),
        pl.BlockSpec((3, GROUP_W), lambda b, g: (0, gx + g)),
        pl.BlockSpec((3, SSD_STATE), lambda b, g: (0, gb + g)),
        pl.BlockSpec((3, SSD_STATE), lambda b, g: (0, gc + g)),
        pl.BlockSpec((1, GROUP_W), lambda b, g: (0, gx + g)),
        pl.BlockSpec((1, SSD_STATE), lambda b, g: (0, gb + g)),
        pl.BlockSpec((1, SSD_STATE), lambda b, g: (0, gc + g)),
        pl.BlockSpec((1, 8, 1), lambda b, g: (g, 0, 0)),
        pl.BlockSpec((1, 8, 1), lambda b, g: (g, 0, 0)),
        pl.BlockSpec((1, 2, GROUP_W), lambda b, g: (g, 0, 0)),
        pl.BlockSpec((1, GROUP_W), lambda b, g: (0, g)),
        pl.BlockSpec((2, CHUNK, CHUNK), lambda b, g: (0, 0, 0)),
        pl.BlockSpec((2, CHUNK, 3 * GROUP_W), lambda b, g: (0, 0, 0)),
    ]
    return pl.pallas_call(
        _ssd_kernel,
        grid=(BATCH, SSD_GROUPS),
        in_specs=in_specs,
        out_specs=pl.BlockSpec((1, SEQ, GROUP_W), lambda b, g: (b, 0, g)),
        out_shape=jax.ShapeDtypeStruct((BATCH, SEQ, D_MODEL), BF16),
        scratch_shapes=[pltpu.VMEM((SEQ, GROUP_W), F32),
                        pltpu.VMEM((N_CHUNKS, 2, 2 * CHUNK, LANES), BF16),
                        pltpu.VMEM((N_CHUNKS, CHUNK, CHUNK), BF16),
                        pltpu.VMEM((N_CHUNKS, CHUNK, CHUNK), BF16),
                        pltpu.VMEM((N_CHUNKS, CHUNK, CHUNK), F32),
                        pltpu.VMEM((SEQ, GROUP_W), F32),
                        pltpu.VMEM((SSD_STATE, GROUP_W), F32),
                        pltpu.VMEM((SSD_STATE, GROUP_W), F32),
                        pltpu.VMEM((N_CHUNKS, CHUNK, CHUNK), BF16),
                        pltpu.VMEM((N_CHUNKS, SEG_ROWS, 8 * CHUNK), BF16),
                        pltpu.VMEM((N_CHUNKS, 8, CHUNK), F32)],
        compiler_params=_params(("arbitrary", "arbitrary")),
        name="ssd",
    )(proj, proj, proj, proj, dtr, cproj, cproj, cdtr,
      conv_w, conv_w, conv_w, conv_b, conv_b, conv_b, dtb, alog, dsk, norm_w, tri, _spread_matrix())


def _spread_matrix():
    sel = np.zeros((2, CHUNK, 3 * GROUP_W), np.float32)
    for d in range(2):
        for grp, comp in enumerate((TAB_EXP, TAB_W, TAB_DEC)):
            for part in range(2):
                for h in range(SSD_REP):
                    lanes = slice(grp * GROUP_W + h * SSD_HEAD_DIM,
                                  grp * GROUP_W + (h + 1) * SSD_HEAD_DIM)
                    sel[d, (comp + part) * 8 + d * SSD_REP + h, lanes] = 1.0
    return jnp.asarray(sel, BF16)


def _pool_kernel(v_ref, win_ref, inv_ref, pw_ref, ps_ref, o_ref, p_scr):
    slab = win_ref.shape[1]
    for r in range(0, v_ref.shape[0], slab):
        v = v_ref[r:r + slab, :]
        p_scr[r:r + slab, :] = (_dot(win_ref[0], v) * inv_ref[0] - v.astype(F32)).astype(BF16)
    o_ref[...] = (_dot(p_scr[...], pw_ref[0]) * ps_ref[...]).astype(o_ref.dtype)


def _pool_constants(tm):
    t = np.arange(tm)
    col = t % GRID_W
    wins, invs = [], []
    for w in POOL_WINDOWS:
        start = np.clip(col - w // 2, 0, GRID_W)
        end = np.clip(col + w - w // 2, 0, GRID_W)
        same_row = (t[:, None] // GRID_W) == (t[None, :] // GRID_W)
        inside = (col[None, :] >= start[:, None]) & (col[None, :] < end[:, None])
        wins.append((same_row & inside).astype(np.float32))
        invs.append((1.0 / (end - start).astype(np.float32))[:, None])
    return jnp.asarray(np.stack(wins), BF16), jnp.asarray(np.stack(invs), F32)


def _pool(proj2d, pool_w, pool_scale, tm=2048, slab=128):
    win, inv = _pool_constants(slab)
    gv = COL_V // POOL_GW
    return pl.pallas_call(
        _pool_kernel,
        grid=(TOKENS // tm, len(POOL_WINDOWS)),
        in_specs=[pl.BlockSpec((tm, POOL_GW), lambda i, k: (i, gv + k)),
                  pl.BlockSpec((1, slab, slab), lambda i, k: (k, 0, 0)),
                  pl.BlockSpec((1, slab, 1), lambda i, k: (k, 0, 0)),
                  pl.BlockSpec((1, POOL_GW, POOL_GW), lambda i, k: (k, 0, 0)),
                  pl.BlockSpec((1, POOL_GW), lambda i, k: (0, k))],
        out_specs=pl.BlockSpec((tm, POOL_GW), lambda i, k: (i, k)),
        out_shape=jax.ShapeDtypeStruct((TOKENS, D_MODEL), BF16),
        scratch_shapes=[pltpu.VMEM((tm, POOL_GW), BF16)],
        compiler_params=_params(("arbitrary", "arbitrary")),
        name="pool",
    )(proj2d, win, inv, pool_w, pool_scale)


def _merge_kernel(ys_ref, yp_ref, ws_ref, wp_ref, gs_ref, gp_ref, o_ref):
    a = jax.nn.sigmoid(gs_ref[...].astype(F32)) * _dot(ys_ref[...], ws_ref[...])
    b = jax.nn.sigmoid(gp_ref[...].astype(F32)) * _dot(yp_ref[...], wp_ref[...])
    o_ref[...] = (a + b).astype(o_ref.dtype)


def _merge(y_ssd, y_pool, w_s, w_p, proj2d, tm=1024, tn=MERGE_TN):
    ggs = COL_GS // tn
    ggp = COL_GP // tn
    return pl.pallas_call(
        _merge_kernel,
        grid=(TOKENS // tm, D_MODEL // tn),
        in_specs=[pl.BlockSpec((tm, D_MODEL), lambda i, j: (i, 0)),
                  pl.BlockSpec((tm, D_MODEL), lambda i, j: (i, 0)),
                  pl.BlockSpec((D_MODEL, tn), lambda i, j: (0, j)),
                  pl.BlockSpec((D_MODEL, tn), lambda i, j: (0, j)),
                  pl.BlockSpec((tm, tn), lambda i, j: (i, ggs + j)),
                  pl.BlockSpec((tm, tn), lambda i, j: (i, ggp + j))],
        out_specs=pl.BlockSpec((tm, tn), lambda i, j: (i, j)),
        out_shape=jax.ShapeDtypeStruct((TOKENS, D_MODEL), BF16),
        compiler_params=_params(("arbitrary", "arbitrary")),
        name="merge",
    )(y_ssd, y_pool, w_s, w_p, proj2d, proj2d)


def _outproj_kernel(m_ref, wo_ref, x_ref, g1_ref, sh_ref, sc_ref, nw_ref, x1_ref, h2_ref):
    x1_ref[...] = x_ref[...] + g1_ref[0] * _dot(m_ref[...], wo_ref[...])
    _modulate_rows(x1_ref, nw_ref[...], sc_ref[0], sh_ref[0], h2_ref, x1_ref.shape[0], rc=128)


def _outproj(merged, w_o, x2d, g1, sh2, sc2, nw2, tm=512):
    tpb = SEQ // tm
    mod_spec = pl.BlockSpec((1, 1, D_MODEL), lambda i: (i // tpb, 0, 0))
    return pl.pallas_call(
        _outproj_kernel,
        grid=(TOKENS // tm,),
        in_specs=[pl.BlockSpec((tm, D_MODEL), lambda i: (i, 0)),
                  pl.BlockSpec((D_MODEL, D_MODEL), lambda i: (0, 0), pipeline_mode=pl.Buffered(1)),
                  pl.BlockSpec((tm, D_MODEL), lambda i: (i, 0)),
                  mod_spec, mod_spec, mod_spec,
                  pl.BlockSpec((1, D_MODEL), lambda i: (0, 0))],
        out_specs=[pl.BlockSpec((tm, D_MODEL), lambda i: (i, 0)),
                   pl.BlockSpec((tm, D_MODEL), lambda i: (i, 0))],
        out_shape=[jax.ShapeDtypeStruct((TOKENS, D_MODEL), F32),
                   jax.ShapeDtypeStruct((TOKENS, D_MODEL), BF16)],
        compiler_params=_params(("arbitrary",)),
        name="outproj",
    )(merged, w_o, x2d, g1, sh2, sc2, nw2)


def _glu_kernel(h_ref, hp_ref, hn_ref, wa_ref, wg_ref, fw_ref, fb_ref, u_ref,
                hext_scr, a_even, g_even, a_odd, g_odd, *, n_chunks, n_steps):
    s = pl.program_id(0)
    tm = h_ref.shape[0]
    cur = jnp.minimum(s, n_steps - 2)
    i = cur // n_chunks

    @pl.when(s == 0)
    def _():
        a_odd[...] = jnp.zeros(a_odd.shape, F32)
        g_odd[...] = jnp.zeros(g_odd.shape, F32)

    @pl.when(cur % n_chunks == 0)
    def _():
        has_prev = ((i * tm) % SEQ) != 0
        has_next = (((i + 1) * tm) % SEQ) != 0
        hext_scr[0:GRID_W, :] = jnp.where(has_prev, hp_ref[...], jnp.zeros_like(hp_ref[...]))
        hext_scr[GRID_W:GRID_W + tm, :] = h_ref[...]
        hext_scr[GRID_W + tm:, :] = jnp.where(has_next, hn_ref[...], jnp.zeros_like(hn_ref[...]))

    def step(wr_a, wr_g, rd_a, rd_g):
        fw = 0.5 * fw_ref[...]
        fb = 0.5 * fb_ref[...]
        for r in range(0, tm, GLU_ROWS):
            hv = (rd_g[r:r + GLU_ROWS, :] * fw[0:1, :] + fb
                  + rd_g[r + GRID_W:r + GRID_W + GLU_ROWS, :] * fw[1:2, :]
                  + rd_g[r + 2 * GRID_W:r + 2 * GRID_W + GLU_ROWS, :] * fw[2:3, :])
            gelu = hv * (1.0 + lax.erf(hv * np.float32(np.sqrt(2.0))))
            u_ref[r:r + GLU_ROWS, :] = (gelu * rd_a[r:r + GLU_ROWS, :]).astype(u_ref.dtype)
        wr_a[...] = _dot(hext_scr[GRID_W:GRID_W + tm, :], wa_ref[...])
        wr_g[...] = _dot(hext_scr[...], wg_ref[...])

    @pl.when(s % 2 == 0)
    def _():
        step(a_even, g_even, a_odd, g_odd)

    @pl.when(s % 2 == 1)
    def _():
        step(a_odd, g_odd, a_even, g_even)


def _glu(h2, w_up, ffn_conv_w, ffn_conv_b, tm=2048, tn=GLU_TN):
    n_chunks = D_FF // tn
    n_tiles = TOKENS // tm
    n_steps = n_tiles * n_chunks + 1
    hb = tm // GRID_W
    n_halo = TOKENS // GRID_W

    def cur(s):
        return jnp.minimum(s, n_steps - 2)

    def prev(s):
        return jnp.maximum(s - 1, 0)

    return pl.pallas_call(
        functools.partial(_glu_kernel, n_chunks=n_chunks, n_steps=n_steps),
        grid=(n_steps,),
        in_specs=[pl.BlockSpec((tm, D_MODEL), lambda s: (cur(s) // n_chunks, 0),
                               pipeline_mode=pl.Buffered(1)),
                  pl.BlockSpec((GRID_W, D_MODEL),
                               lambda s: (jnp.maximum((cur(s) // n_chunks) * hb - 1, 0), 0)),
                  pl.BlockSpec((GRID_W, D_MODEL),
                               lambda s: (jnp.minimum((cur(s) // n_chunks + 1) * hb, n_halo - 1), 0)),
                  pl.BlockSpec((D_MODEL, tn), lambda s: (0, cur(s) % n_chunks)),
                  pl.BlockSpec((D_MODEL, tn), lambda s: (0, n_chunks + cur(s) % n_chunks)),
                  pl.BlockSpec((3, tn), lambda s: (0, prev(s) % n_chunks)),
                  pl.BlockSpec((1, tn), lambda s: (0, prev(s) % n_chunks))],
        out_specs=pl.BlockSpec((tm, tn), lambda s: (prev(s) // n_chunks, prev(s) % n_chunks)),
        out_shape=jax.ShapeDtypeStruct((TOKENS, D_FF), BF16),
        scratch_shapes=[pltpu.VMEM((tm + 2 * GRID_W, D_MODEL), BF16),
                        pltpu.VMEM((tm, tn), F32), pltpu.VMEM((tm + 2 * GRID_W, tn), F32),
                        pltpu.VMEM((tm, tn), F32), pltpu.VMEM((tm + 2 * GRID_W, tn), F32)],
        compiler_params=_params(("arbitrary",)),
        name="glu",
    )(h2, h2, h2, w_up, w_up, ffn_conv_w, ffn_conv_b)


def _down_kernel(u_ref, wd_ref, x1_ref, g2_ref, nw_ref, o_ref):
    x2 = x1_ref[...] + g2_ref[0] * _dot(u_ref[...], wd_ref[...])
    ms = jnp.mean(x2 * x2, axis=-1, keepdims=True)
    o_ref[...] = x2 * lax.rsqrt(ms + EPS) * nw_ref[...]


def _down(u, w_down, x1, g2, fnw, tm=512):
    tpb = SEQ // tm
    return pl.pallas_call(
        _down_kernel,
        grid=(TOKENS // tm,),
        in_specs=[pl.BlockSpec((tm, D_FF), lambda i: (i, 0)),
                  pl.BlockSpec((D_FF, D_MODEL), lambda i: (0, 0), pipeline_mode=pl.Buffered(1)),
                  pl.BlockSpec((tm, D_MODEL), lambda i: (i, 0)),
                  pl.BlockSpec((1, 1, D_MODEL), lambda i: (i // tpb, 0, 0)),
                  pl.BlockSpec((1, D_MODEL), lambda i: (0, 0))],
        out_specs=pl.BlockSpec((tm, D_MODEL), lambda i: (i, 0)),
        out_shape=jax.ShapeDtypeStruct((TOKENS, D_MODEL), F32),
        compiler_params=_params(("arbitrary",)),
        name="down",
    )(u, w_down, x1, g2, fnw)


def _dt_rows(dt_raw, batch, n_chunks):
    d = dt_raw[:, :2 * SSD_HEADS].reshape(batch, n_chunks, CHUNK, 2, SSD_GROUPS, SSD_REP)
    d = jnp.transpose(d, (4, 0, 1, 3, 5, 2))
    return d.reshape(SSD_GROUPS, batch, n_chunks, 2 * SSD_REP, CHUNK)


def _group_rows(p):
    p = p.reshape(2, SSD_GROUPS, SSD_REP)
    return jnp.transpose(p, (1, 0, 2)).reshape(SSD_GROUPS, 2 * SSD_REP, 1)


def kernel(x, c, ctx, c_ctx, w_ada, b_ada, norm1_w, w_in, ssd_conv_w, ssd_conv_b, dt_bias, a_log,
           d_skip, ssd_norm_w, w_ssd_out, pool_w, pool_scale, w_pool_out, w_o, norm2_w, w_up,
           ffn_conv_w, ffn_conv_b, w_down, final_norm_w):
    assert w_in.shape[0] == 1, "single-layer block only"
    d = D_MODEL
    x2d = x.reshape(TOKENS, d)
    ctx2d = ctx.reshape(BATCH * CTX_LEN, d)

    cs = jnp.zeros((8, d), F32).at[:BATCH].set(c).at[BATCH].set(c_ctx)
    mod = _ada(cs, w_ada[0], b_ada)
    sh1, sc1, g1, sh2, sc2, g2 = [mod[:BATCH, k * d:(k + 1) * d].reshape(BATCH, 1, d)
                                  for k in range(6)]
    csh1 = mod[BATCH:BATCH + 1, 0:d].reshape(1, 1, d)
    csc1 = mod[BATCH:BATCH + 1, d:2 * d].reshape(1, 1, d)

    w_in0 = w_in[0]
    n_dt = 2 * SSD_HEADS
    w_dt = jnp.pad(w_in0[:, :n_dt], ((0, 0), (0, LANES - n_dt)))
    w_cat = w_in0[:, n_dt:].astype(BF16)
    w_s = w_ssd_out[0].astype(BF16)
    w_p = w_pool_out[0].astype(BF16)
    w_ob = w_o[0].astype(BF16)
    w_upb = w_up[0].astype(BF16)
    w_db = w_down[0].astype(BF16)
    pool_wb = pool_w[0].astype(BF16)

    proj, dt_raw = _inproj(x2d, sh1, sc1, norm1_w, w_cat, w_dt, n_cols=N_PROJ, rows_per_mod=SEQ)
    cproj, cdt_raw = _inproj(ctx2d, csh1, csc1, norm1_w, w_cat, w_dt, n_cols=N_CTX_PROJ,
                             rows_per_mod=BATCH * CTX_LEN)

    dtr = _dt_rows(dt_raw, BATCH, N_CHUNKS)
    cdtr = _dt_rows(cdt_raw, BATCH, N_CTX_CHUNKS)
    dsk = jnp.repeat(jnp.transpose(d_skip[0].reshape(2, SSD_GROUPS, SSD_REP), (1, 0, 2)),
                     SSD_HEAD_DIM, axis=-1)
    idx = np.arange(CHUNK)
    tri = jnp.asarray(np.stack([idx[:, None] <= idx[None, :], idx[:, None] >= idx[None, :]])
                      .astype(np.float32), BF16)

    y_ssd = _ssd(proj.reshape(BATCH, SEQ, N_PROJ), dtr,
                 cproj.reshape(BATCH, CTX_LEN, N_CTX_PROJ), cdtr,
                 ssd_conv_w[0], ssd_conv_b, _group_rows(dt_bias[0]), _group_rows(a_log[0]),
                 dsk, ssd_norm_w, tri)
    y_pool = _pool(proj, pool_wb, pool_scale)
    merged = _merge(y_ssd.reshape(TOKENS, d), y_pool, w_s, w_p, proj)
    x1, h2 = _outproj(merged, w_ob, x2d, g1, sh2, sc2, norm2_w)
    u = _glu(h2, w_upb, ffn_conv_w[0], ffn_conv_b)
    out = _down(u, w_db, x1, g2, final_norm_w.reshape(1, d))
    return out.reshape(BATCH, SEQ, d)
```

```python
import functools

import numpy as np
import jax
import jax.numpy as jnp
from jax import lax
from jax.experimental import pallas as pl
from jax.experimental.pallas import tpu as pltpu

F32 = jnp.float32
BF16 = jnp.bfloat16

D_MODEL = 2048
BATCH = 4
SEQ = 4096
TOKENS = BATCH * SEQ
GRID_W = 64
CTX_LEN = 256

SSD_HEAD_DIM = 64
SSD_HEADS = 32
SSD_GROUPS = 8
SSD_REP = 4
SSD_STATE = 128
CHUNK = 128
GROUP_W = SSD_REP * SSD_HEAD_DIM
N_CHUNKS = SEQ // CHUNK
N_CTX_CHUNKS = CTX_LEN // CHUNK
SCAN_UNROLL = 16
GLU_ROWS = 64
INPROJ_TN, MERGE_TN, GLU_TN = 1536, 512, 512

POOL_WINDOWS = (2, 4, 8, 16)
POOL_GW = 512
D_FF = 5632
EPS = 1e-6

COL_X = 0
COL_B = 2048
COL_C = 3072
COL_Z = 4096
COL_V = 6144
COL_GS = 8192
COL_GP = 10240
N_PROJ = 12288
N_CTX_PROJ = 3072

LANES = 128
VMEM_LIMIT = 56 * 1024 * 1024


def _dot(a, b):
    return jnp.dot(a, b, preferred_element_type=F32)


def _silu(x):
    return x * jax.nn.sigmoid(x)


def _softplus(x):
    return jnp.maximum(x, 0.0) + jnp.log1p(jnp.exp(-jnp.abs(x)))


def _params(sem):
    return pltpu.CompilerParams(dimension_semantics=sem, vmem_limit_bytes=VMEM_LIMIT)


def _shift_cast_kernel(a_ref, b_ref, o_ref, *, shift):
    tn = o_ref.shape[1]
    for r in range(0, a_ref.shape[0], 256):
        x = jnp.concatenate([a_ref[r:r + 256, :], b_ref[r:r + 256, :]], axis=1)
        o_ref[r:r + 256, :] = x[:, shift:shift + tn].astype(o_ref.dtype)


def _drop_leading_cols(w, shift, tn=1024):
    k, n = w.shape
    n_out = n - shift
    per = tn // LANES
    return pl.pallas_call(
        functools.partial(_shift_cast_kernel, shift=shift),
        grid=(n_out // tn,),
        in_specs=[pl.BlockSpec((k, tn), lambda j: (0, j)),
                  pl.BlockSpec((k, LANES), lambda j: (0, (j + 1) * per))],
        out_specs=pl.BlockSpec((k, tn), lambda j: (0, j)),
        out_shape=jax.ShapeDtypeStruct((k, n_out), BF16),
        compiler_params=_params(("arbitrary",)),
        name="wprep",
    )(w, w)


def _ada_kernel(c_ref, w_ref, b_ref, o_ref):
    s_hi, s_lo = _split_bf16(_silu(c_ref[...]), 2)
    lhs = jnp.concatenate([s_hi, s_lo], axis=0).astype(BF16)
    w = w_ref[...]
    w_hi = w.astype(BF16)
    w_lo = (w - w_hi.astype(F32)).astype(BF16)
    r = _dot(lhs, w_hi) + _dot(lhs, w_lo)
    o_ref[...] = r[0:8] + r[8:16] + b_ref[...]


def _ada(cs, w, b):
    tn = 1024
    n = w.shape[1]
    return pl.pallas_call(
        _ada_kernel,
        grid=(n // tn,),
        in_specs=[pl.BlockSpec((8, D_MODEL), lambda j: (0, 0)),
                  pl.BlockSpec((D_MODEL, tn), lambda j: (0, j)),
                  pl.BlockSpec((1, tn), lambda j: (0, j))],
        out_specs=pl.BlockSpec((8, tn), lambda j: (0, j)),
        out_shape=jax.ShapeDtypeStruct((8, n), F32),
        compiler_params=_params(("arbitrary",)),
        name="ada",
    )(cs, w, b)


def _modulate_rows(x_ref, nw, sc, sh, out_ref, rows, rc=256):
    gain = nw * (1.0 + sc)

    def body(k, carry):
        r = pl.multiple_of(k * rc, rc)
        x = x_ref[pl.ds(r, rc), :]
        inv = lax.rsqrt(jnp.mean(x * x, axis=-1, keepdims=True) + EPS)
        out_ref[pl.ds(r, rc), :] = (x_ref[pl.ds(r, rc), :] * inv * gain + sh).astype(out_ref.dtype)
        return carry
    lax.fori_loop(0, rows // rc, body, 0)


def _inproj_kernel(x_ref, sh_ref, sc_ref, nw_ref, w_ref, wdt_ref, o_ref, dt_ref, h_scr):
    @pl.when(pl.program_id(1) == 0)
    def _():
        _modulate_rows(x_ref, nw_ref[...], sc_ref[0], sh_ref[0], h_scr, h_scr.shape[0])
        dt_ref[...] = _dot(h_scr[...], wdt_ref[...].astype(BF16))

    o_ref[...] = _dot(h_scr[...], w_ref[...]).astype(o_ref.dtype)


def _inproj(x2d, sh, sc, nw, w_cat, w_dt, *, n_cols, rows_per_mod, tm=1024, tn=INPROJ_TN):
    m = x2d.shape[0]
    tpb = rows_per_mod // tm
    return pl.pallas_call(
        _inproj_kernel,
        grid=(m // tm, n_cols // tn),
        in_specs=[pl.BlockSpec((tm, D_MODEL), lambda i, j: (i, 0)),
                  pl.BlockSpec((1, 1, D_MODEL), lambda i, j: (i // tpb, 0, 0)),
                  pl.BlockSpec((1, 1, D_MODEL), lambda i, j: (i // tpb, 0, 0)),
                  pl.BlockSpec((1, D_MODEL), lambda i, j: (0, 0)),
                  pl.BlockSpec((D_MODEL, tn), lambda i, j: (0, j)),
                  pl.BlockSpec((D_MODEL, LANES), lambda i, j: (0, 0))],
        out_specs=[pl.BlockSpec((tm, tn), lambda i, j: (i, j)),
                   pl.BlockSpec((tm, LANES), lambda i, j: (i, 0))],
        out_shape=[jax.ShapeDtypeStruct((m, n_cols), BF16),
                   jax.ShapeDtypeStruct((m, LANES), F32)],
        scratch_shapes=[pltpu.VMEM((tm, D_MODEL), BF16)],
        compiler_params=_params(("arbitrary", "arbitrary")),
        name="inproj",
    )(x2d, sh, sc, nw, w_cat, w_dt)


def _conv_silu_tile(ref, c, n_chunks, w, bias):
    length = n_chunks * CHUNK
    if isinstance(c, int):
        r0, p0, n0 = c * CHUNK, max(c * CHUNK - 16, 0), min(c * CHUNK + CHUNK, length - 16)
    else:
        r0 = pl.multiple_of(c * CHUNK, CHUNK)
        p0 = pl.multiple_of(jnp.maximum(c * CHUNK - 16, 0), 16)
        n0 = pl.multiple_of(jnp.minimum(c * CHUNK + CHUNK, length - 16), 16)
    x = ref[0, pl.ds(r0, CHUNK), :].astype(F32)
    prev_row = ref[0, pl.ds(p0, 16), :].astype(F32)[15:16, :]
    prev_row = jnp.where(c > 0, prev_row, 0.0)
    next_row = ref[0, pl.ds(n0, 16), :].astype(F32)[0:1, :]
    next_row = jnp.where(c < n_chunks - 1, next_row, 0.0)
    rows = lax.broadcasted_iota(jnp.int32, (8, x.shape[1]), 0)
    xm = pltpu.roll(x, 1, axis=0)
    xm = jnp.concatenate([jnp.where(rows == 0, prev_row, xm[0:8]), xm[8:]], axis=0)
    xp = pltpu.roll(x, CHUNK - 1, axis=0)
    xp = jnp.concatenate([xp[:CHUNK - 8], jnp.where(rows == 7, next_row, xp[CHUNK - 8:])], axis=0)
    y = xm * w[0:1, :] + x * w[1:2, :] + xp * w[2:3, :] + bias
    return _silu(y)


def _decay_rows(dt_raw, dt_bias, a_neg, tri, causal):
    dt = _softplus(dt_raw + dt_bias)
    a = dt * a_neg
    hi = a.astype(BF16)
    lo = (a - hi.astype(F32)).astype(BF16)
    r = _dot(jnp.concatenate([hi, lo], axis=0), tri)
    acs = r[0:8] + r[8:16]
    tot = acs[:, CHUNK - 1:CHUNK] if causal else acs[:, 0:1]
    return acs, dt, jnp.exp(tot - acs) * dt, jnp.exp(tot)


def _split_bf16(x, parts):
    out = []
    for _ in range(parts - 1):
        hi = x.astype(BF16).astype(F32)
        out.append(hi)
        x = x - hi
    out.append(x.astype(BF16).astype(F32))
    return out


TAB_ACS, TAB_ONE, TAB_EXP, TAB_W, TAB_DEC, TAB_USED = 0, 3, 6, 8, 10, 12
SEG_ROWS = 48


def _decay_tables(dtr_ref, dt_bias, a_neg, tri_ref, a_scr, bsel_scr, dt_scr):
    n = N_CHUNKS * 8
    dt = _softplus(dtr_ref[0, 0] + dt_bias)
    a = (dt * a_neg).reshape(n, CHUNK)
    dt = dt.reshape(n, CHUNK)
    hi, lo = _split_bf16(a, 2)
    lhs = jnp.concatenate([hi, lo], axis=0).astype(BF16)
    pre = _dot(lhs, tri_ref[0])
    suf = _dot(lhs, tri_ref[1])
    row = lax.broadcasted_iota(jnp.int32, (n, CHUNK), 0)
    fwd = jnp.bitwise_and(row, 7) < SSD_REP
    acs = jnp.where(fwd, pre[:n] + pre[n:], suf[:n] + suf[n:])
    tot = jnp.where(fwd[:, 0:1], acs[:, CHUNK - 1:CHUNK], acs[:, 0:1])
    ones = jnp.ones((n, CHUNK), F32)
    comps_a = (_split_bf16(acs, 3) + [ones, ones, ones] + _split_bf16(jnp.exp(acs), 2)
               + _split_bf16(jnp.exp(tot - acs) * dt, 2)
               + _split_bf16(jnp.broadcast_to(jnp.exp(tot), acs.shape), 2))
    comps_b = [ones, ones, ones] + [-p for p in _split_bf16(acs, 3)]
    row48 = jnp.bitwise_and(lax.broadcasted_iota(jnp.int32, (SEG_ROWS, CHUNK), 0), 7)
    pad = jnp.zeros((CHUNK - 8 * TAB_USED, CHUNK), F32)
    for c in range(N_CHUNKS):
        sl = slice(c * 8, c * 8 + 8)
        a_scr[c] = jnp.concatenate([p[sl] for p in comps_a] + [pad], axis=0).T.astype(BF16)
        b_all = jnp.concatenate([p[sl] for p in comps_b], axis=0)
        for k in range(8):
            bsel_scr[c, :, k * CHUNK:(k + 1) * CHUNK] = jnp.where(row48 == k, b_all, 0.0).astype(BF16)
        dt_scr[c] = dt[sl]


def _head_lanes(dec, off):
    lane = lax.broadcasted_iota(jnp.int32, (1, LANES), 1)
    t0 = jnp.where(lane < SSD_HEAD_DIM, dec[off:off + 1, :], dec[off + 1:off + 2, :])
    t1 = jnp.where(lane < SSD_HEAD_DIM, dec[off + 2:off + 3, :], dec[off + 3:off + 4, :])
    return jnp.concatenate([t0, t1], axis=1)


def _split_heads(t):
    lane = lax.broadcasted_iota(jnp.int32, t.shape, 1)
    return (jnp.where(lane < SSD_HEAD_DIM, t, 0.0).astype(BF16),
            jnp.where(lane >= SSD_HEAD_DIM, t, 0.0).astype(BF16))


def _state_update(x_split, bt, w, dec, off, h_ref):
    h = h_ref[...]
    ss = []
    for pair in range(2):
        bw = [(bt * w[off + r:off + r + 1, :]).astype(BF16) for r in (2 * pair, 2 * pair + 1)]
        ss.append(_dot(jnp.concatenate(bw, axis=1), jnp.concatenate(x_split[pair], axis=0)))
    h_ref[...] = h * _head_lanes(dec, off) + jnp.concatenate(ss, axis=1)
    return h


def _chunk_dir(xs, x_pairs, bt, cc, cb, tab, bsel, dt, sel, off, h_ref, causal):
    l_io = lax.broadcasted_iota(jnp.int32, (CHUNK, CHUNK), 0)
    s_io = lax.broadcasted_iota(jnp.int32, (CHUNK, CHUNK), 1)
    mask = (l_io >= s_io) if causal else (l_io <= s_io)
    seg = _dot(tab, jnp.concatenate([bsel, jnp.zeros((CHUNK - SEG_ROWS, bsel.shape[1]), BF16)],
                                    axis=0))
    spread = _dot(tab, sel)
    e_exp, w_exp, dec = spread[:, 0:GROUP_W], spread[:, GROUP_W:2 * GROUP_W], spread[0:1, 2 * GROUP_W:]
    h = h_ref[...]
    h_ref[...] = h * dec + _dot(bt, (xs * w_exp).astype(BF16))
    y = _dot(cc, h.astype(BF16)) * e_exp
    ys = []
    for pair in range(2):
        ms = []
        for r in (2 * pair, 2 * pair + 1):
            decay = jnp.exp(jnp.where(mask, seg[:, r * CHUNK:(r + 1) * CHUNK], -jnp.inf))
            ms.append((cb * decay * dt[off + r:off + r + 1, :]).astype(BF16))
        ys.append(_dot(jnp.concatenate(ms, axis=1), x_pairs[pair]))
    return y + jnp.concatenate(ys, axis=1)


def _ssd_kernel(xs_ref, b_ref, c_ref, z_ref, dtr_ref, cxs_ref, cbm_ref, cdtr_ref,
                cwx_ref, cwb_ref, cwc_ref, bx_ref, bb_ref, bc_ref,
                dtb_ref, alog_ref, dsk_ref, nw_ref, tri_ref, sel_ref,
                y_ref,
                xs_scr, xp_scr, bt_scr, cc_scr, cb_scr, y_scr, hf_scr, hb_scr,
                tab_scr, bsel_scr, dt_scr):
    dt_bias = dtb_ref[0]
    a_neg = -jnp.exp(alog_ref[0])
    cwx, cwb, cwc = cwx_ref[...], cwb_ref[...], cwc_ref[...]
    bx, bb, bc = bx_ref[...], bb_ref[...], bc_ref[...]

    hf_scr[...] = jnp.zeros(hf_scr.shape, F32)
    hb_scr[...] = jnp.zeros(hb_scr.shape, F32)
    ctx_tiles = []
    for c in range(N_CTX_CHUNKS):
        xs = _conv_silu_tile(cxs_ref, c, N_CTX_CHUNKS, cwx, bx)
        bt = _conv_silu_tile(cbm_ref, c, N_CTX_CHUNKS, cwb, bb).T
        ctx_tiles.append(([_split_heads(xs[:, p * LANES:(p + 1) * LANES]) for p in range(2)], bt))
    for c in range(N_CTX_CHUNKS):
        _, _, w, dec = _decay_rows(cdtr_ref[0, 0, c], dt_bias, a_neg, tri_ref[0], True)
        _state_update(ctx_tiles[c][0], ctx_tiles[c][1], w, dec, 0, hf_scr)
    for c in reversed(range(N_CTX_CHUNKS)):
        _, _, w, dec = _decay_rows(cdtr_ref[0, 0, c], dt_bias, a_neg, tri_ref[1], False)
        _state_update(ctx_tiles[c][0], ctx_tiles[c][1], w, dec, 4, hb_scr)

    _decay_tables(dtr_ref, dt_bias, a_neg, tri_ref, tab_scr, bsel_scr, dt_scr)

    def prepare(c):
        r0 = pl.multiple_of(c * CHUNK, CHUNK)
        xs = _conv_silu_tile(xs_ref, c, N_CHUNKS, cwx, bx)
        bt = _conv_silu_tile(b_ref, c, N_CHUNKS, cwb, bb).T.astype(BF16)
        cc = _conv_silu_tile(c_ref, c, N_CHUNKS, cwc, bc).astype(BF16)
        cb = _dot(cc, bt)
        x_pairs = [jnp.concatenate(_split_heads(xs[:, p * LANES:(p + 1) * LANES]), axis=0)
                   for p in range(2)]
        xs_scr[pl.ds(r0, CHUNK), :] = xs
        xp_scr[c, 0] = x_pairs[0]
        xp_scr[c, 1] = x_pairs[1]
        bt_scr[c] = bt
        cc_scr[c] = cc
        cb_scr[c] = cb
        return xs, x_pairs, bt, cc, cb

    def recall(c):
        r0 = pl.multiple_of(c * CHUNK, CHUNK)
        return (xs_scr[pl.ds(r0, CHUNK), :], [xp_scr[c, 0], xp_scr[c, 1]], bt_scr[c], cc_scr[c],
                cb_scr[c])

    def scan(c, tiles, causal):
        xs, x_pairs, bt, cc, cb = tiles
        d = 0 if causal else 1
        bsel = bsel_scr[c, :, d * SSD_REP * CHUNK:(d + 1) * SSD_REP * CHUNK]
        return _chunk_dir(xs, x_pairs, bt, cc, cb, tab_scr[c], bsel, dt_scr[c], sel_ref[d],
                          d * SSD_REP, hf_scr if causal else hb_scr, causal)

    d_sum = dsk_ref[0, 0:1, :] + dsk_ref[0, 1:2, :]
    nw = nw_ref[...]

    def finish(c, y_dir, xs):
        r0 = pl.multiple_of(c * CHUNK, CHUNK)
        y = y_scr[pl.ds(r0, CHUNK), :] + y_dir + xs * d_sum
        gated = y * _silu(z_ref[0, pl.ds(r0, CHUNK), :].astype(F32))
        ms = jnp.mean(gated * gated, axis=-1, keepdims=True)
        y_ref[0, pl.ds(r0, CHUNK), :] = (gated * lax.rsqrt(ms + EPS) * nw).astype(y_ref.dtype)

    def visits(i):
        fwd = [i * SCAN_UNROLL + u for u in range(SCAN_UNROLL)]
        return [(c, True) for c in fwd] + [(N_CHUNKS - 1 - c, False) for c in fwd]

    def first_half(i, carry):
        for c, causal in visits(i):
            r0 = pl.multiple_of(c * CHUNK, CHUNK)
            y_scr[pl.ds(r0, CHUNK), :] = scan(c, prepare(c), causal)
        return carry

    def second_half(i, carry):
        for c, causal in visits(i):
            tiles = recall(c)
            finish(c, scan(c, tiles, causal), tiles[0])
        return carry

    half = N_CHUNKS // 2 // SCAN_UNROLL
    lax.fori_loop(0, half, first_half, 0)
    lax.fori_loop(half, 2 * half, second_half, 0)


def _ssd(proj, dtr, cproj, cdtr, conv_w, conv_b, dtb, alog, dsk, norm_w, tri):
    gx = COL_X // GROUP_W
    gz = COL_Z // GROUP_W
    gb = COL_B // SSD_STATE
    gc = COL_C // SSD_STATE
    in_specs = [
        pl.BlockSpec((1, SEQ, GROUP_W), lambda b, g: (b, 0, gx + g)),
        pl.BlockSpec((1, SEQ, SSD_STATE), lambda b, g: (b, 0, gb + g)),
        pl.BlockSpec((1, SEQ, SSD_STATE), lambda b, g: (b, 0, gc + g)),
        pl.BlockSpec((1, SEQ, GROUP_W), lambda b, g: (b, 0, gz + g)),
        pl.BlockSpec((1, 1, N_CHUNKS, 8, CHUNK), lambda b, g: (g, b, 0, 0, 0)),
        pl.BlockSpec((1, CTX_LEN, GROUP_W), lambda b, g: (b, 0, gx + g)),
        pl.BlockSpec((1, CTX_LEN, SSD_STATE), lambda b, g: (b, 0, gb + g)),
        pl.BlockSpec((1, 1, N_CTX_CHUNKS, 8, CHUNK), lambda b, g: (g, b, 0, 0, 0)),
        pl.BlockSpec((3, GROUP_W), lambda b, g: (0, gx + g)),
        pl.BlockSpec((3, SSD_STATE), lambda b, g: (0, gb + g)),
        pl.BlockSpec((3, SSD_STATE), lambda b, g: (0, gc + g)),
        pl.BlockSpec((1, GROUP_W), lambda b, g: (0, gx + g)),
        pl.BlockSpec((1, SSD_STATE), lambda b, g: (0, gb + g)),
        pl.BlockSpec((1, SSD_STATE), lambda b, g: (0, gc + g)),
        pl.BlockSpec((1, 8, 1), lambda b, g: (g, 0, 0)),
        pl.BlockSpec((1, 8, 1), lambda b, g: (g, 0, 0)),
        pl.BlockSpec((1, 2, GROUP_W), lambda b, g: (g, 0, 0)),
        pl.BlockSpec((1, GROUP_W), lambda b, g: (0, g)),
        pl.BlockSpec((2, CHUNK, CHUNK), lambda b, g: (0, 0, 0)),
        pl.BlockSpec((2, CHUNK, 3 * GROUP_W), lambda b, g: (0, 0, 0)),
    ]
    return pl.pallas_call(
        _ssd_kernel,
        grid=(BATCH, SSD_GROUPS),
        in_specs=in_specs,
        out_specs=pl.BlockSpec((1, SEQ, GROUP_W), lambda b, g: (b, 0, g)),
        out_shape=jax.ShapeDtypeStruct((BATCH, SEQ, D_MODEL), BF16),
        scratch_shapes=[pltpu.VMEM((SEQ, GROUP_W), F32),
                        pltpu.VMEM((N_CHUNKS, 2, 2 * CHUNK, LANES), BF16),
                        pltpu.VMEM((N_CHUNKS, CHUNK, CHUNK), BF16),
                        pltpu.VMEM((N_CHUNKS, CHUNK, CHUNK), BF16),
                        pltpu.VMEM((N_CHUNKS, CHUNK, CHUNK), F32),
                        pltpu.VMEM((SEQ, GROUP_W), F32),
                        pltpu.VMEM((SSD_STATE, GROUP_W), F32),
                        pltpu.VMEM((SSD_STATE, GROUP_W), F32),
                        pltpu.VMEM((N_CHUNKS, CHUNK, CHUNK), BF16),
                        pltpu.VMEM((N_CHUNKS, SEG_ROWS, 8 * CHUNK), BF16),
                        pltpu.VMEM((N_CHUNKS, 8, CHUNK), F32)],
        compiler_params=_params(("arbitrary", "arbitrary")),
        name="ssd",
    )(proj, proj, proj, proj, dtr, cproj, cproj, cdtr,
      conv_w, conv_w, conv_w, conv_b, conv_b, conv_b, dtb, alog, dsk, norm_w, tri, _spread_matrix())


def _spread_matrix():
    sel = np.zeros((2, CHUNK, 3 * GROUP_W), np.float32)
    for d in range(2):
        for grp, comp in enumerate((TAB_EXP, TAB_W, TAB_DEC)):
            for part in range(2):
                for h in range(SSD_REP):
                    lanes = slice(grp * GROUP_W + h * SSD_HEAD_DIM,
                                  grp * GROUP_W + (h + 1) * SSD_HEAD_DIM)
                    sel[d, (comp + part) * 8 + d * SSD_REP + h, lanes] = 1.0
    return jnp.asarray(sel, BF16)


def _pool_kernel(v_ref, win_ref, inv_ref, pw_ref, ps_ref, o_ref, p_scr):
    slab = win_ref.shape[1]
    for r in range(0, v_ref.shape[0], slab):
        v = v_ref[r:r + slab, :]
        p_scr[r:r + slab, :] = (_dot(win_ref[0], v) * inv_ref[0] - v.astype(F32)).astype(BF16)
    o_ref[...] = (_dot(p_scr[...], pw_ref[0]) * ps_ref[...]).astype(o_ref.dtype)


def _pool_constants(tm):
    t = np.arange(tm)
    col = t % GRID_W
    wins, invs = [], []
    for w in POOL_WINDOWS:
        start = np.clip(col - w // 2, 0, GRID_W)
        end = np.clip(col + w - w // 2, 0, GRID_W)
        same_row = (t[:, None] // GRID_W) == (t[None, :] // GRID_W)
        inside = (col[None, :] >= start[:, None]) & (col[None, :] < end[:, None])
        wins.append((same_row & inside).astype(np.float32))
        invs.append((1.0 / (end - start).astype(np.float32))[:, None])
    return jnp.asarray(np.stack(wins), BF16), jnp.asarray(np.stack(invs), F32)


def _pool(proj2d, pool_w, pool_scale, tm=2048, slab=128):
    win, inv = _pool_constants(slab)
    gv = COL_V // POOL_GW
    return pl.pallas_call(
        _pool_kernel,
        grid=(TOKENS // tm, len(POOL_WINDOWS)),
        in_specs=[pl.BlockSpec((tm, POOL_GW), lambda i, k: (i, gv + k)),
                  pl.BlockSpec((1, slab, slab), lambda i, k: (k, 0, 0)),
                  pl.BlockSpec((1, slab, 1), lambda i, k: (k, 0, 0)),
                  pl.BlockSpec((1, POOL_GW, POOL_GW), lambda i, k: (k, 0, 0)),
                  pl.BlockSpec((1, POOL_GW), lambda i, k: (0, k))],
        out_specs=pl.BlockSpec((tm, POOL_GW), lambda i, k: (i, k)),
        out_shape=jax.ShapeDtypeStruct((TOKENS, D_MODEL), BF16),
        scratch_shapes=[pltpu.VMEM((tm, POOL_GW), BF16)],
        compiler_params=_params(("arbitrary", "arbitrary")),
        name="pool",
    )(proj2d, win, inv, pool_w, pool_scale)


def _merge_kernel(ys_ref, yp_ref, ws_ref, wp_ref, gs_ref, gp_ref, o_ref):
    a = jax.nn.sigmoid(gs_ref[...].astype(F32)) * _dot(ys_ref[...], ws_ref[...])
    b = jax.nn.sigmoid(gp_ref[...].astype(F32)) * _dot(yp_ref[...], wp_ref[...])
    o_ref[...] = (a + b).astype(o_ref.dtype)


def _merge(y_ssd, y_pool, w_s, w_p, proj2d, tm=1024, tn=MERGE_TN):
    ggs = COL_GS // tn
    ggp = COL_GP // tn
    return pl.pallas_call(
        _merge_kernel,
        grid=(TOKENS // tm, D_MODEL // tn),
        in_specs=[pl.BlockSpec((tm, D_MODEL), lambda i, j: (i, 0)),
                  pl.BlockSpec((tm, D_MODEL), lambda i, j: (i, 0)),
                  pl.BlockSpec((D_MODEL, tn), lambda i, j: (0, j)),
                  pl.BlockSpec((D_MODEL, tn), lambda i, j: (0, j)),
                  pl.BlockSpec((tm, tn), lambda i, j: (i, ggs + j)),
                  pl.BlockSpec((tm, tn), lambda i, j: (i, ggp + j))],
        out_specs=pl.BlockSpec((tm, tn), lambda i, j: (i, j)),
        out_shape=jax.ShapeDtypeStruct((TOKENS, D_MODEL), BF16),
        compiler_params=_params(("arbitrary", "arbitrary")),
        name="merge",
    )(y_ssd, y_pool, w_s, w_p, proj2d, proj2d)


def _outproj_kernel(m_ref, wo_ref, x_ref, g1_ref, sh_ref, sc_ref, nw_ref, x1_ref, h2_ref):
    x1_ref[...] = x_ref[...] + g1_ref[0] * _dot(m_ref[...], wo_ref[...])
    _modulate_rows(x1_ref, nw_ref[...], sc_ref[0], sh_ref[0], h2_ref, x1_ref.shape[0], rc=128)


def _outproj(merged, w_o, x2d, g1, sh2, sc2, nw2, tm=512):
    tpb = SEQ // tm
    mod_spec = pl.BlockSpec((1, 1, D_MODEL), lambda i: (i // tpb, 0, 0))
    return pl.pallas_call(
        _outproj_kernel,
        grid=(TOKENS // tm,),
        in_specs=[pl.BlockSpec((tm, D_MODEL), lambda i: (i, 0)),
                  pl.BlockSpec((D_MODEL, D_MODEL), lambda i: (0, 0), pipeline_mode=pl.Buffered(1)),
                  pl.BlockSpec((tm, D_MODEL), lambda i: (i, 0)),
                  mod_spec, mod_spec, mod_spec,
                  pl.BlockSpec((1, D_MODEL), lambda i: (0, 0))],
        out_specs=[pl.BlockSpec((tm, D_MODEL), lambda i: (i, 0)),
                   pl.BlockSpec((tm, D_MODEL), lambda i: (i, 0))],
        out_shape=[jax.ShapeDtypeStruct((TOKENS, D_MODEL), F32),
                   jax.ShapeDtypeStruct((TOKENS, D_MODEL), BF16)],
        compiler_params=_params(("arbitrary",)),
        name="outproj",
    )(merged, w_o, x2d, g1, sh2, sc2, nw2)


def _glu_kernel(h_ref, hp_ref, hn_ref, wa_ref, wg_ref, fw_ref, fb_ref, u_ref,
                hext_scr, a_even, g_even, a_odd, g_odd, *, n_chunks, n_steps):
    s = pl.program_id(0)
    tm = h_ref.shape[0]
    cur = jnp.minimum(s, n_steps - 2)
    i = cur // n_chunks

    @pl.when(s == 0)
    def _():
        a_odd[...] = jnp.zeros(a_odd.shape, F32)
        g_odd[...] = jnp.zeros(g_odd.shape, F32)

    @pl.when(cur % n_chunks == 0)
    def _():
        has_prev = ((i * tm) % SEQ) != 0
        has_next = (((i + 1) * tm) % SEQ) != 0
        hext_scr[0:GRID_W, :] = jnp.where(has_prev, hp_ref[...], jnp.zeros_like(hp_ref[...]))
        hext_scr[GRID_W:GRID_W + tm, :] = h_ref[...]
        hext_scr[GRID_W + tm:, :] = jnp.where(has_next, hn_ref[...], jnp.zeros_like(hn_ref[...]))

    def step(wr_a, wr_g, rd_a, rd_g):
        fw = 0.5 * fw_ref[...]
        fb = 0.5 * fb_ref[...]
        for r in range(0, tm, GLU_ROWS):
            hv = (rd_g[r:r + GLU_ROWS, :] * fw[0:1, :] + fb
                  + rd_g[r + GRID_W:r + GRID_W + GLU_ROWS, :] * fw[1:2, :]
                  + rd_g[r + 2 * GRID_W:r + 2 * GRID_W + GLU_ROWS, :] * fw[2:3, :])
            gelu = hv * (1.0 + lax.erf(hv * np.float32(np.sqrt(2.0))))
            u_ref[r:r + GLU_ROWS, :] = (gelu * rd_a[r:r + GLU_ROWS, :]).astype(u_ref.dtype)
        wr_a[...] = _dot(hext_scr[GRID_W:GRID_W + tm, :], wa_ref[...])
        wr_g[...] = _dot(hext_scr[...], wg_ref[...])

    @pl.when(s % 2 == 0)
    def _():
        step(a_even, g_even, a_odd, g_odd)

    @pl.when(s % 2 == 1)
    def _():
        step(a_odd, g_odd, a_even, g_even)


def _glu(h2, w_up, ffn_conv_w, ffn_conv_b, tm=2048, tn=GLU_TN):
    n_chunks = D_FF // tn
    n_tiles = TOKENS // tm
    n_steps = n_tiles * n_chunks + 1
    hb = tm // GRID_W
    n_halo = TOKENS // GRID_W

    def cur(s):
        return jnp.minimum(s, n_steps - 2)

    def prev(s):
        return jnp.maximum(s - 1, 0)

    return pl.pallas_call(
        functools.partial(_glu_kernel, n_chunks=n_chunks, n_steps=n_steps),
        grid=(n_steps,),
        in_specs=[pl.BlockSpec((tm, D_MODEL), lambda s: (cur(s) // n_chunks, 0),
                               pipeline_mode=pl.Buffered(1)),
                  pl.BlockSpec((GRID_W, D_MODEL),
                               lambda s: (jnp.maximum((cur(s) // n_chunks) * hb - 1, 0), 0)),
                  pl.BlockSpec((GRID_W, D_MODEL),
                               lambda s: (jnp.minimum((cur(s) // n_chunks + 1) * hb, n_halo - 1), 0)),
                  pl.BlockSpec((D_MODEL, tn), lambda s: (0, cur(s) % n_chunks)),
                  pl.BlockSpec((D_MODEL, tn), lambda s: (0, n_chunks + cur(s) % n_chunks)),
                  pl.BlockSpec((3, tn), lambda s: (0, prev(s) % n_chunks)),
                  pl.BlockSpec((1, tn), lambda s: (0, prev(s) % n_chunks))],
        out_specs=pl.BlockSpec((tm, tn), lambda s: (prev(s) // n_chunks, prev(s) % n_chunks)),
        out_shape=jax.ShapeDtypeStruct((TOKENS, D_FF), BF16),
        scratch_shapes=[pltpu.VMEM((tm + 2 * GRID_W, D_MODEL), BF16),
                        pltpu.VMEM((tm, tn), F32), pltpu.VMEM((tm + 2 * GRID_W, tn), F32),
                        pltpu.VMEM((tm, tn), F32), pltpu.VMEM((tm + 2 * GRID_W, tn), F32)],
        compiler_params=_params(("arbitrary",)),
        name="glu",
    )(h2, h2, h2, w_up, w_up, ffn_conv_w, ffn_conv_b)


def _down_kernel(u_ref, wd_ref, x1_ref, g2_ref, nw_ref, o_ref):
    x2 = x1_ref[...] + g2_ref[0] * _dot(u_ref[...], wd_ref[...])
    ms = jnp.mean(x2 * x2, axis=-1, keepdims=True)
    o_ref[...] = x2 * lax.rsqrt(ms + EPS) * nw_ref[...]


def _down(u, w_down, x1, g2, fnw, tm=512):
    tpb = SEQ // tm
    return pl.pallas_call(
        _down_kernel,
        grid=(TOKENS // tm,),
        in_specs=[pl.BlockSpec((tm, D_FF), lambda i: (i, 0)),
                  pl.BlockSpec((D_FF, D_MODEL), lambda i: (0, 0), pipeline_mode=pl.Buffered(1)),
                  pl.BlockSpec((tm, D_MODEL), lambda i: (i, 0)),
                  pl.BlockSpec((1, 1, D_MODEL), lambda i: (i // tpb, 0, 0)),
                  pl.BlockSpec((1, D_MODEL), lambda i: (0, 0))],
        out_specs=pl.BlockSpec((tm, D_MODEL), lambda i: (i, 0)),
        out_shape=jax.ShapeDtypeStruct((TOKENS, D_MODEL), F32),
        compiler_params=_params(("arbitrary",)),
        name="down",
    )(u, w_down, x1, g2, fnw)


def _dt_rows(dt_raw, batch, n_chunks):
    d = dt_raw[:, :2 * SSD_HEADS].reshape(batch, n_chunks, CHUNK, 2, SSD_GROUPS, SSD_REP)
    d = jnp.transpose(d, (4, 0, 1, 3, 5, 2))
    return d.reshape(SSD_GROUPS, batch, n_chunks, 2 * SSD_REP, CHUNK)


def _group_rows(p):
    p = p.reshape(2, SSD_GROUPS, SSD_REP)
    return jnp.transpose(p, (1, 0, 2)).reshape(SSD_GROUPS, 2 * SSD_REP, 1)


def kernel(x, c, ctx, c_ctx, w_ada, b_ada, norm1_w, w_in, ssd_conv_w, ssd_conv_b, dt_bias, a_log,
           d_skip, ssd_norm_w, w_ssd_out, pool_w, pool_scale, w_pool_out, w_o, norm2_w, w_up,
           ffn_conv_w, ffn_conv_b, w_down, final_norm_w):
    assert w_in.shape[0] == 1, "single-layer block only"
    d = D_MODEL
    x2d = x.reshape(TOKENS, d)
    ctx2d = ctx.reshape(BATCH * CTX_LEN, d)

    cs = jnp.zeros((8, d), F32).at[:BATCH].set(c).at[BATCH].set(c_ctx)
    mod = _ada(cs, w_ada[0], b_ada)
    sh1, sc1, g1, sh2, sc2, g2 = [mod[:BATCH, k * d:(k + 1) * d].reshape(BATCH, 1, d)
                                  for k in range(6)]
    csh1 = mod[BATCH:BATCH + 1, 0:d].reshape(1, 1, d)
    csc1 = mod[BATCH:BATCH + 1, d:2 * d].reshape(1, 1, d)

    w_in0 = w_in[0]
    n_dt = 2 * SSD_HEADS
    w_dt = jnp.pad(w_in0[:, :n_dt], ((0, 0), (0, LANES - n_dt)))
    w_cat = _drop_leading_cols(w_in0, n_dt)
    w_s = w_ssd_out[0].astype(BF16)
    w_p = w_pool_out[0].astype(BF16)
    w_ob = w_o[0].astype(BF16)
    w_upb = w_up[0].astype(BF16)
    w_db = w_down[0].astype(BF16)
    pool_wb = pool_w[0].astype(BF16)

    proj, dt_raw = _inproj(x2d, sh1, sc1, norm1_w, w_cat, w_dt, n_cols=N_PROJ, rows_per_mod=SEQ)
    cproj, cdt_raw = _inproj(ctx2d, csh1, csc1, norm1_w, w_cat, w_dt, n_cols=N_CTX_PROJ,
                             rows_per_mod=BATCH * CTX_LEN)

    dtr = _dt_rows(dt_raw, BATCH, N_CHUNKS)
    cdtr = _dt_rows(cdt_raw, BATCH, N_CTX_CHUNKS)
    dsk = jnp.repeat(jnp.transpose(d_skip[0].reshape(2, SSD_GROUPS, SSD_REP), (1, 0, 2)),
                     SSD_HEAD_DIM, axis=-1)
    idx = np.arange(CHUNK)
    tri = jnp.asarray(np.stack([idx[:, None] <= idx[None, :], idx[:, None] >= idx[None, :]])
                      .astype(np.float32), BF16)

    y_ssd = _ssd(proj.reshape(BATCH, SEQ, N_PROJ), dtr,
                 cproj.reshape(BATCH, CTX_LEN, N_CTX_PROJ), cdtr,
                 ssd_conv_w[0], ssd_conv_b, _group_rows(dt_bias[0]), _group_rows(a_log[0]),
                 dsk, ssd_norm_w, tri)
    y_pool = _pool(proj, pool_wb, pool_scale)
    merged = _merge(y_ssd.reshape(TOKENS, d), y_pool, w_s, w_p, proj)
    x1, h2 = _outproj(merged, w_ob, x2d, g1, sh2, sc2, norm2_w)
    u = _glu(h2, w_upb, ffn_conv_w[0], ffn_conv_b)
    out = _down(u, w_db, x1, g2, final_norm_w.reshape(1, d))
    return out.reshape(BATCH, SEQ, d)
```

```python
import functools

import numpy as np
import jax
import jax.numpy as jnp
from jax import lax
from jax.experimental import pallas as pl
from jax.experimental.pallas import tpu as pltpu

F32 = jnp.float32
BF16 = jnp.bfloat16

D_MODEL = 2048
BATCH = 4
SEQ = 4096
TOKENS = BATCH * SEQ
GRID_W = 64
CTX_LEN = 256

SSD_HEAD_DIM = 64
SSD_HEADS = 32
SSD_GROUPS = 8
SSD_REP = 4
SSD_STATE = 128
CHUNK = 128
GROUP_W = SSD_REP * SSD_HEAD_DIM
N_CHUNKS = SEQ // CHUNK
N_CTX_CHUNKS = CTX_LEN // CHUNK
SCAN_UNROLL = 16
GLU_ROWS = 64
INPROJ_TN, MERGE_TN, GLU_TN = 1536, 512, 512

POOL_WINDOWS = (2, 4, 8, 16)
POOL_GW = 512
D_FF = 5632
EPS = 1e-6

COL_X = 0
COL_B = 2048
COL_C = 3072
COL_Z = 4096
COL_V = 6144
COL_GS = 8192
COL_GP = 10240
N_PROJ = 12288
N_CTX_PROJ = 3072

LANES = 128
VMEM_LIMIT = 56 * 1024 * 1024


def _dot(a, b):
    return jnp.dot(a, b, preferred_element_type=F32)


def _silu(x):
    return x * jax.nn.sigmoid(x)


def _softplus(x):
    return jnp.maximum(x, 0.0) + jnp.log1p(jnp.exp(-jnp.abs(x)))


def _params(sem):
    return pltpu.CompilerParams(dimension_semantics=sem, vmem_limit_bytes=VMEM_LIMIT)


def _shift_cast_kernel(a_ref, b_ref, head_ref, o_ref, head_o_ref, *, shift):
    tn = o_ref.shape[1]

    @pl.when(pl.program_id(0) == 0)
    def _():
        head_o_ref[...] = head_ref[...].T

    x = jnp.concatenate([a_ref[shift:, :], b_ref[...]], axis=0)
    for r in range(0, tn, LANES):
        o_ref[:, r:r + LANES] = x[r:r + LANES, :].T.astype(o_ref.dtype)


def _drop_leading_cols(w_t, shift, tn=1024):
    n, k = w_t.shape
    n_out = n - shift
    return pl.pallas_call(
        functools.partial(_shift_cast_kernel, shift=shift),
        grid=(n_out // tn,),
        in_specs=[pl.BlockSpec((tn, k), lambda j: (j, 0)),
                  pl.BlockSpec((shift, k), lambda j: ((j + 1) * (tn // shift), 0)),
                  pl.BlockSpec((LANES, k), lambda j: (0, 0))],
        out_specs=[pl.BlockSpec((k, tn), lambda j: (0, j)),
                   pl.BlockSpec((k, LANES), lambda j: (0, 0))],
        out_shape=[jax.ShapeDtypeStruct((k, n_out), BF16),
                   jax.ShapeDtypeStruct((k, LANES), F32)],
        compiler_params=_params(("arbitrary",)),
        name="wprep",
    )(w_t, w_t, w_t)


def _ada_kernel(c_ref, w_ref, b_ref, o_ref):
    s_hi, s_lo = _split_bf16(_silu(c_ref[...]), 2)
    lhs = jnp.concatenate([s_hi, s_lo], axis=0).astype(BF16)
    w = w_ref[...]
    w_hi = w.astype(BF16)
    w_lo = (w - w_hi.astype(F32)).astype(BF16)
    r = _dot(lhs, w_hi) + _dot(lhs, w_lo)
    o_ref[...] = r[0:8] + r[8:16] + b_ref[...]


def _ada(cs, w, b):
    tn = 1024
    n = w.shape[1]
    return pl.pallas_call(
        _ada_kernel,
        grid=(n // tn,),
        in_specs=[pl.BlockSpec((8, D_MODEL), lambda j: (0, 0)),
                  pl.BlockSpec((D_MODEL, tn), lambda j: (0, j)),
                  pl.BlockSpec((1, tn), lambda j: (0, j))],
        out_specs=pl.BlockSpec((8, tn), lambda j: (0, j)),
        out_shape=jax.ShapeDtypeStruct((8, n), F32),
        compiler_params=_params(("arbitrary",)),
        name="ada",
    )(cs, w, b)


def _modulate_rows(x_ref, nw, sc, sh, out_ref, rows, rc=256):
    gain = nw * (1.0 + sc)

    def body(k, carry):
        r = pl.multiple_of(k * rc, rc)
        x = x_ref[pl.ds(r, rc), :]
        inv = lax.rsqrt(jnp.mean(x * x, axis=-1, keepdims=True) + EPS)
        out_ref[pl.ds(r, rc), :] = (x_ref[pl.ds(r, rc), :] * inv * gain + sh).astype(out_ref.dtype)
        return carry
    lax.fori_loop(0, rows // rc, body, 0)


def _inproj_kernel(x_ref, sh_ref, sc_ref, nw_ref, w_ref, wdt_ref, o_ref, dt_ref, h_scr):
    @pl.when(pl.program_id(1) == 0)
    def _():
        _modulate_rows(x_ref, nw_ref[...], sc_ref[0], sh_ref[0], h_scr, h_scr.shape[0])
        dt_ref[...] = _dot(h_scr[...], wdt_ref[...].astype(BF16))

    o_ref[...] = _dot(h_scr[...], w_ref[...]).astype(o_ref.dtype)


def _inproj(x2d, sh, sc, nw, w_cat, w_dt, *, n_cols, rows_per_mod, tm=1024, tn=INPROJ_TN):
    m = x2d.shape[0]
    tpb = rows_per_mod // tm
    return pl.pallas_call(
        _inproj_kernel,
        grid=(m // tm, n_cols // tn),
        in_specs=[pl.BlockSpec((tm, D_MODEL), lambda i, j: (i, 0)),
                  pl.BlockSpec((1, 1, D_MODEL), lambda i, j: (i // tpb, 0, 0)),
                  pl.BlockSpec((1, 1, D_MODEL), lambda i, j: (i // tpb, 0, 0)),
                  pl.BlockSpec((1, D_MODEL), lambda i, j: (0, 0)),
                  pl.BlockSpec((D_MODEL, tn), lambda i, j: (0, j)),
                  pl.BlockSpec((D_MODEL, LANES), lambda i, j: (0, 0))],
        out_specs=[pl.BlockSpec((tm, tn), lambda i, j: (i, j)),
                   pl.BlockSpec((tm, LANES), lambda i, j: (i, 0))],
        out_shape=[jax.ShapeDtypeStruct((m, n_cols), BF16),
                   jax.ShapeDtypeStruct((m, LANES), F32)],
        scratch_shapes=[pltpu.VMEM((tm, D_MODEL), BF16)],
        compiler_params=_params(("arbitrary", "arbitrary")),
        name="inproj",
    )(x2d, sh, sc, nw, w_cat, w_dt)


def _conv_silu_tile(ref, c, n_chunks, w, bias):
    length = n_chunks * CHUNK
    if isinstance(c, int):
        r0, p0, n0 = c * CHUNK, max(c * CHUNK - 16, 0), min(c * CHUNK + CHUNK, length - 16)
    else:
        r0 = pl.multiple_of(c * CHUNK, CHUNK)
        p0 = pl.multiple_of(jnp.maximum(c * CHUNK - 16, 0), 16)
        n0 = pl.multiple_of(jnp.minimum(c * CHUNK + CHUNK, length - 16), 16)
    x = ref[0, pl.ds(r0, CHUNK), :].astype(F32)
    prev_row = ref[0, pl.ds(p0, 16), :].astype(F32)[15:16, :]
    prev_row = jnp.where(c > 0, prev_row, 0.0)
    next_row = ref[0, pl.ds(n0, 16), :].astype(F32)[0:1, :]
    next_row = jnp.where(c < n_chunks - 1, next_row, 0.0)
    rows = lax.broadcasted_iota(jnp.int32, (8, x.shape[1]), 0)
    xm = pltpu.roll(x, 1, axis=0)
    xm = jnp.concatenate([jnp.where(rows == 0, prev_row, xm[0:8]), xm[8:]], axis=0)
    xp = pltpu.roll(x, CHUNK - 1, axis=0)
    xp = jnp.concatenate([xp[:CHUNK - 8], jnp.where(rows == 7, next_row, xp[CHUNK - 8:])], axis=0)
    y = xm * w[0:1, :] + x * w[1:2, :] + xp * w[2:3, :] + bias
    return _silu(y)


def _decay_rows(dt_raw, dt_bias, a_neg, tri, causal):
    dt = _softplus(dt_raw + dt_bias)
    a = dt * a_neg
    hi = a.astype(BF16)
    lo = (a - hi.astype(F32)).astype(BF16)
    r = _dot(jnp.concatenate([hi, lo], axis=0), tri)
    acs = r[0:8] + r[8:16]
    tot = acs[:, CHUNK - 1:CHUNK] if causal else acs[:, 0:1]
    return acs, dt, jnp.exp(tot - acs) * dt, jnp.exp(tot)


def _split_bf16(x, parts):
    out = []
    for _ in range(parts - 1):
        hi = x.astype(BF16).astype(F32)
        out.append(hi)
        x = x - hi
    out.append(x.astype(BF16).astype(F32))
    return out


TAB_ACS, TAB_ONE, TAB_EXP, TAB_W, TAB_DEC, TAB_USED = 0, 3, 6, 8, 10, 12
SEG_ROWS = 48


def _decay_tables(dtr_ref, dt_bias, a_neg, tri_ref, a_scr, bsel_scr, dt_scr):
    n = N_CHUNKS * 8
    dt = _softplus(dtr_ref[0, 0] + dt_bias)
    a = (dt * a_neg).reshape(n, CHUNK)
    dt = dt.reshape(n, CHUNK)
    hi, lo = _split_bf16(a, 2)
    lhs = jnp.concatenate([hi, lo], axis=0).astype(BF16)
    pre = _dot(lhs, tri_ref[0])
    suf = _dot(lhs, tri_ref[1])
    row = lax.broadcasted_iota(jnp.int32, (n, CHUNK), 0)
    fwd = jnp.bitwise_and(row, 7) < SSD_REP
    acs = jnp.where(fwd, pre[:n] + pre[n:], suf[:n] + suf[n:])
    tot = jnp.where(fwd[:, 0:1], acs[:, CHUNK - 1:CHUNK], acs[:, 0:1])
    ones = jnp.ones((n, CHUNK), F32)
    comps_a = (_split_bf16(acs, 3) + [ones, ones, ones] + _split_bf16(jnp.exp(acs), 2)
               + _split_bf16(jnp.exp(tot - acs) * dt, 2)
               + _split_bf16(jnp.broadcast_to(jnp.exp(tot), acs.shape), 2))
    comps_b = [ones, ones, ones] + [-p for p in _split_bf16(acs, 3)]
    row48 = jnp.bitwise_and(lax.broadcasted_iota(jnp.int32, (SEG_ROWS, CHUNK), 0), 7)
    pad = jnp.zeros((CHUNK - 8 * TAB_USED, CHUNK), F32)
    for c in range(N_CHUNKS):
        sl = slice(c * 8, c * 8 + 8)
        a_scr[c] = jnp.concatenate([p[sl] for p in comps_a] + [pad], axis=0).T.astype(BF16)
        b_all = jnp.concatenate([p[sl] for p in comps_b], axis=0)
        for k in range(8):
            bsel_scr[c, :, k * CHUNK:(k + 1) * CHUNK] = jnp.where(row48 == k, b_all, 0.0).astype(BF16)
        dt_scr[c] = dt[sl]


def _head_lanes(dec, off):
    lane = lax.broadcasted_iota(jnp.int32, (1, LANES), 1)
    t0 = jnp.where(lane < SSD_HEAD_DIM, dec[off:off + 1, :], dec[off + 1:off + 2, :])
    t1 = jnp.where(lane < SSD_HEAD_DIM, dec[off + 2:off + 3, :], dec[off + 3:off + 4, :])
    return jnp.concatenate([t0, t1], axis=1)


def _split_heads(t):
    lane = lax.broadcasted_iota(jnp.int32, t.shape, 1)
    return (jnp.where(lane < SSD_HEAD_DIM, t, 0.0).astype(BF16),
            jnp.where(lane >= SSD_HEAD_DIM, t, 0.0).astype(BF16))


def _state_update(x_split, bt, w, dec, off, h_ref):
    h = h_ref[...]
    ss = []
    for pair in range(2):
        bw = [(bt * w[off + r:off + r + 1, :]).astype(BF16) for r in (2 * pair, 2 * pair + 1)]
        ss.append(_dot(jnp.concatenate(bw, axis=1), jnp.concatenate(x_split[pair], axis=0)))
    h_ref[...] = h * _head_lanes(dec, off) + jnp.concatenate(ss, axis=1)
    return h


def _chunk_dir(xs, x_pairs, bt, cc, cb, tab, bsel, dt, sel, off, h_ref, causal):
    l_io = lax.broadcasted_iota(jnp.int32, (CHUNK, CHUNK), 0)
    s_io = lax.broadcasted_iota(jnp.int32, (CHUNK, CHUNK), 1)
    mask = (l_io >= s_io) if causal else (l_io <= s_io)
    seg = _dot(tab, jnp.concatenate([bsel, jnp.zeros((CHUNK - SEG_ROWS, bsel.shape[1]), BF16)],
                                    axis=0))
    spread = _dot(tab, sel)
    e_exp, w_exp, dec = spread[:, 0:GROUP_W], spread[:, GROUP_W:2 * GROUP_W], spread[0:1, 2 * GROUP_W:]
    h = h_ref[...]
    h_ref[...] = h * dec + _dot(bt, (xs * w_exp).astype(BF16))
    y = _dot(cc, h.astype(BF16)) * e_exp
    ys = []
    for pair in range(2):
        ms = []
        for r in (2 * pair, 2 * pair + 1):
            decay = jnp.exp(jnp.where(mask, seg[:, r * CHUNK:(r + 1) * CHUNK], -jnp.inf))
            ms.append((cb * decay * dt[off + r:off + r + 1, :]).astype(BF16))
        ys.append(_dot(jnp.concatenate(ms, axis=1), x_pairs[pair]))
    return y + jnp.concatenate(ys, axis=1)


def _ssd_kernel(xs_ref, b_ref, c_ref, z_ref, dtr_ref, cxs_ref, cbm_ref, cdtr_ref,
                cwx_ref, cwb_ref, cwc_ref, bx_ref, bb_ref, bc_ref,
                dtb_ref, alog_ref, dsk_ref, nw_ref, tri_ref, sel_ref,
                y_ref,
                xs_scr, xp_scr, bt_scr, cc_scr, cb_scr, y_scr, hf_scr, hb_scr,
                tab_scr, bsel_scr, dt_scr):
    dt_bias = dtb_ref[0]
    a_neg = -jnp.exp(alog_ref[0])
    cwx, cwb, cwc = cwx_ref[...], cwb_ref[...], cwc_ref[...]
    bx, bb, bc = bx_ref[...], bb_ref[...], bc_ref[...]

    hf_scr[...] = jnp.zeros(hf_scr.shape, F32)
    hb_scr[...] = jnp.zeros(hb_scr.shape, F32)
    ctx_tiles = []
    for c in range(N_CTX_CHUNKS):
        xs = _conv_silu_tile(cxs_ref, c, N_CTX_CHUNKS, cwx, bx)
        bt = _conv_silu_tile(cbm_ref, c, N_CTX_CHUNKS, cwb, bb).T
        ctx_tiles.append(([_split_heads(xs[:, p * LANES:(p + 1) * LANES]) for p in range(2)], bt))
    for c in range(N_CTX_CHUNKS):
        _, _, w, dec = _decay_rows(cdtr_ref[0, 0, c], dt_bias, a_neg, tri_ref[0], True)
        _state_update(ctx_tiles[c][0], ctx_tiles[c][1], w, dec, 0, hf_scr)
    for c in reversed(range(N_CTX_CHUNKS)):
        _, _, w, dec = _decay_rows(cdtr_ref[0, 0, c], dt_bias, a_neg, tri_ref[1], False)
        _state_update(ctx_tiles[c][0], ctx_tiles[c][1], w, dec, 4, hb_scr)

    _decay_tables(dtr_ref, dt_bias, a_neg, tri_ref, tab_scr, bsel_scr, dt_scr)

    def prepare(c):
        r0 = pl.multiple_of(c * CHUNK, CHUNK)
        xs = _conv_silu_tile(xs_ref, c, N_CHUNKS, cwx, bx)
        bt = _conv_silu_tile(b_ref, c, N_CHUNKS, cwb, bb).T.astype(BF16)
        cc = _conv_silu_tile(c_ref, c, N_CHUNKS, cwc, bc).astype(BF16)
        cb = _dot(cc, bt)
        x_pairs = [jnp.concatenate(_split_heads(xs[:, p * LANES:(p + 1) * LANES]), axis=0)
                   for p in range(2)]
        xs_scr[pl.ds(r0, CHUNK), :] = xs
        xp_scr[c, 0] = x_pairs[0]
        xp_scr[c, 1] = x_pairs[1]
        bt_scr[c] = bt
        cc_scr[c] = cc
        cb_scr[c] = cb
        return xs, x_pairs, bt, cc, cb

    def recall(c):
        r0 = pl.multiple_of(c * CHUNK, CHUNK)
        return (xs_scr[pl.ds(r0, CHUNK), :], [xp_scr[c, 0], xp_scr[c, 1]], bt_scr[c], cc_scr[c],
                cb_scr[c])

    def scan(c, tiles, causal):
        xs, x_pairs, bt, cc, cb = tiles
        d = 0 if causal else 1
        bsel = bsel_scr[c, :, d * SSD_REP * CHUNK:(d + 1) * SSD_REP * CHUNK]
        return _chunk_dir(xs, x_pairs, bt, cc, cb, tab_scr[c], bsel, dt_scr[c], sel_ref[d],
                          d * SSD_REP, hf_scr if causal else hb_scr, causal)

    d_sum = dsk_ref[0, 0:1, :] + dsk_ref[0, 1:2, :]
    nw = nw_ref[...]

    def finish(c, y_dir, xs):
        r0 = pl.multiple_of(c * CHUNK, CHUNK)
        y = y_scr[pl.ds(r0, CHUNK), :] + y_dir + xs * d_sum
        gated = y * _silu(z_ref[0, pl.ds(r0, CHUNK), :].astype(F32))
        ms = jnp.mean(gated * gated, axis=-1, keepdims=True)
        y_ref[0, pl.ds(r0, CHUNK), :] = (gated * lax.rsqrt(ms + EPS) * nw).astype(y_ref.dtype)

    def visits(i):
        fwd = [i * SCAN_UNROLL + u for u in range(SCAN_UNROLL)]
        return [(c, True) for c in fwd] + [(N_CHUNKS - 1 - c, False) for c in fwd]

    def first_half(i, carry):
        for c, causal in visits(i):
            r0 = pl.multiple_of(c * CHUNK, CHUNK)
            y_scr[pl.ds(r0, CHUNK), :] = scan(c, prepare(c), causal)
        return carry

    def second_half(i, carry):
        for c, causal in visits(i):
            tiles = recall(c)
            finish(c, scan(c, tiles, causal), tiles[0])
        return carry

    half = N_CHUNKS // 2 // SCAN_UNROLL
    lax.fori_loop(0, half, first_half, 0)
    lax.fori_loop(half, 2 * half, second_half, 0)


def _ssd(proj, dtr, cproj, cdtr, conv_w, conv_b, dtb, alog, dsk, norm_w, tri):
    gx = COL_X // GROUP_W
    gz = COL_Z // GROUP_W
    gb = COL_B // SSD_STATE
    gc = COL_C // SSD_STATE
    in_specs = [
        pl.BlockSpec((1, SEQ, GROUP_W), lambda b, g: (b, 0, gx + g)),
        pl.BlockSpec((1, SEQ, SSD_STATE), lambda b, g: (b, 0, gb + g)),
        pl.BlockSpec((1, SEQ, SSD_STATE), lambda b, g: (b, 0, gc + g)),
        pl.BlockSpec((1, SEQ, GROUP_W), lambda b, g: (b, 0, gz + g)),
        pl.BlockSpec((1, 1, N_CHUNKS, 8, CHUNK), lambda b, g: (g, b, 0, 0, 0)),
        pl.BlockSpec((1, CTX_LEN, GROUP_W), lambda b, g: (b, 0, gx + g)),
        pl.BlockSpec((1, CTX_LEN, SSD_STATE), lambda b, g: (b, 0, gb + g)),
        pl.BlockSpec((1, 1, N_CTX_CHUNKS, 8, CHUNK), lambda b, g: (g, b, 0, 0, 0)),
        pl.BlockSpec((3, GROUP_W), lambda b, g: (0, gx + g)),
        pl.BlockSpec((3, SSD_STATE), lambda b, g: (0, gb + g)),
        pl.BlockSpec((3, SSD_STATE), lambda b, g: (0, gc + g)),
        pl.BlockSpec((1, GROUP_W), lambda b, g: (0, gx + g)),
        pl.BlockSpec((1, SSD_STATE), lambda b, g: (0, gb + g)),
        pl.BlockSpec((1, SSD_STATE), lambda b, g: (0, gc + g)),
        pl.BlockSpec((1, 8, 1), lambda b, g: (g, 0, 0)),
        pl.BlockSpec((1, 8, 1), lambda b, g: (g, 0, 0)),
        pl.BlockSpec((1, 2, GROUP_W), lambda b, g: (g, 0, 0)),
        pl.BlockSpec((1, GROUP_W), lambda b, g: (0, g)),
        pl.BlockSpec((2, CHUNK, CHUNK), lambda b, g: (0, 0, 0)),
        pl.BlockSpec((2, CHUNK, 3 * GROUP_W), lambda b, g: (0, 0, 0)),
    ]
    return pl.pallas_call(
        _ssd_kernel,
        grid=(BATCH, SSD_GROUPS),
        in_specs=in_specs,
        out_specs=pl.BlockSpec((1, SEQ, GROUP_W), lambda b, g: (b, 0, g)),
        out_shape=jax.ShapeDtypeStruct((BATCH, SEQ, D_MODEL), BF16),
        scratch_shapes=[pltpu.VMEM((SEQ, GROUP_W), F32),
                        pltpu.VMEM((N_CHUNKS, 2, 2 * CHUNK, LANES), BF16),
                        pltpu.VMEM((N_CHUNKS, CHUNK, CHUNK), BF16),
                        pltpu.VMEM((N_CHUNKS, CHUNK, CHUNK), BF16),
                        pltpu.VMEM((N_CHUNKS, CHUNK, CHUNK), F32),
                        pltpu.VMEM((SEQ, GROUP_W), F32),
                        pltpu.VMEM((SSD_STATE, GROUP_W), F32),
                        pltpu.VMEM((SSD_STATE, GROUP_W), F32),
                        pltpu.VMEM((N_CHUNKS, CHUNK, CHUNK), BF16),
                        pltpu.VMEM((N_CHUNKS, SEG_ROWS, 8 * CHUNK), BF16),
                        pltpu.VMEM((N_CHUNKS, 8, CHUNK), F32)],
        compiler_params=_params(("arbitrary", "arbitrary")),
        name="ssd",
    )(proj, proj, proj, proj, dtr, cproj, cproj, cdtr,
      conv_w, conv_w, conv_w, conv_b, conv_b, conv_b, dtb, alog, dsk, norm_w, tri, _spread_matrix())


def _spread_matrix():
    sel = np.zeros((2, CHUNK, 3 * GROUP_W), np.float32)
    for d in range(2):
        for grp, comp in enumerate((TAB_EXP, TAB_W, TAB_DEC)):
            for part in range(2):
                for h in range(SSD_REP):
                    lanes = slice(grp * GROUP_W + h * SSD_HEAD_DIM,
                                  grp * GROUP_W + (h + 1) * SSD_HEAD_DIM)
                    sel[d, (comp + part) * 8 + d * SSD_REP + h, lanes] = 1.0
    return jnp.asarray(sel, BF16)


def _pool_kernel(v_ref, win_ref, inv_ref, pw_ref, ps_ref, o_ref, p_scr):
    slab = win_ref.shape[1]
    for r in range(0, v_ref.shape[0], slab):
        v = v_ref[r:r + slab, :]
        p_scr[r:r + slab, :] = (_dot(win_ref[0], v) * inv_ref[0] - v.astype(F32)).astype(BF16)
    o_ref[...] = (_dot(p_scr[...], pw_ref[0]) * ps_ref[...]).astype(o_ref.dtype)


def _pool_constants(tm):
    t = np.arange(tm)
    col = t % GRID_W
    wins, invs = [], []
    for w in POOL_WINDOWS:
        start = np.clip(col - w // 2, 0, GRID_W)
        end = np.clip(col + w - w // 2, 0, GRID_W)
        same_row = (t[:, None] // GRID_W) == (t[None, :] // GRID_W)
        inside = (col[None, :] >= start[:, None]) & (col[None, :] < end[:, None])
        wins.append((same_row & inside).astype(np.float32))
        invs.append((1.0 / (end - start).astype(np.float32))[:, None])
    return jnp.asarray(np.stack(wins), BF16), jnp.asarray(np.stack(invs), F32)


def _pool(proj2d, pool_w, pool_scale, tm=2048, slab=128):
    win, inv = _pool_constants(slab)
    gv = COL_V // POOL_GW
    return pl.pallas_call(
        _pool_kernel,
        grid=(TOKENS // tm, len(POOL_WINDOWS)),
        in_specs=[pl.BlockSpec((tm, POOL_GW), lambda i, k: (i, gv + k)),
                  pl.BlockSpec((1, slab, slab), lambda i, k: (k, 0, 0)),
                  pl.BlockSpec((1, slab, 1), lambda i, k: (k, 0, 0)),
                  pl.BlockSpec((1, POOL_GW, POOL_GW), lambda i, k: (k, 0, 0)),
                  pl.BlockSpec((1, POOL_GW), lambda i, k: (0, k))],
        out_specs=pl.BlockSpec((tm, POOL_GW), lambda i, k: (i, k)),
        out_shape=jax.ShapeDtypeStruct((TOKENS, D_MODEL), BF16),
        scratch_shapes=[pltpu.VMEM((tm, POOL_GW), BF16)],
        compiler_params=_params(("arbitrary", "arbitrary")),
        name="pool",
    )(proj2d, win, inv, pool_w, pool_scale)


def _merge_kernel(ys_ref, yp_ref, ws_ref, wp_ref, gs_ref, gp_ref, o_ref):
    a = jax.nn.sigmoid(gs_ref[...].astype(F32)) * _dot(ys_ref[...], ws_ref[...])
    b = jax.nn.sigmoid(gp_ref[...].astype(F32)) * _dot(yp_ref[...], wp_ref[...])
    o_ref[...] = (a + b).astype(o_ref.dtype)


def _merge(y_ssd, y_pool, w_s, w_p, proj2d, tm=1024, tn=MERGE_TN):
    ggs = COL_GS // tn
    ggp = COL_GP // tn
    return pl.pallas_call(
        _merge_kernel,
        grid=(TOKENS // tm, D_MODEL // tn),
        in_specs=[pl.BlockSpec((tm, D_MODEL), lambda i, j: (i, 0)),
                  pl.BlockSpec((tm, D_MODEL), lambda i, j: (i, 0)),
                  pl.BlockSpec((D_MODEL, tn), lambda i, j: (0, j)),
                  pl.BlockSpec((D_MODEL, tn), lambda i, j: (0, j)),
                  pl.BlockSpec((tm, tn), lambda i, j: (i, ggs + j)),
                  pl.BlockSpec((tm, tn), lambda i, j: (i, ggp + j))],
        out_specs=pl.BlockSpec((tm, tn), lambda i, j: (i, j)),
        out_shape=jax.ShapeDtypeStruct((TOKENS, D_MODEL), BF16),
        compiler_params=_params(("arbitrary", "arbitrary")),
        name="merge",
    )(y_ssd, y_pool, w_s, w_p, proj2d, proj2d)


def _outproj_kernel(m_ref, wo_ref, x_ref, g1_ref, sh_ref, sc_ref, nw_ref, x1_ref, h2_ref):
    x1_ref[...] = x_ref[...] + g1_ref[0] * _dot(m_ref[...], wo_ref[...])
    _modulate_rows(x1_ref, nw_ref[...], sc_ref[0], sh_ref[0], h2_ref, x1_ref.shape[0], rc=128)


def _outproj(merged, w_o, x2d, g1, sh2, sc2, nw2, tm=512):
    tpb = SEQ // tm
    mod_spec = pl.BlockSpec((1, 1, D_MODEL), lambda i: (i // tpb, 0, 0))
    return pl.pallas_call(
        _outproj_kernel,
        grid=(TOKENS // tm,),
        in_specs=[pl.BlockSpec((tm, D_MODEL), lambda i: (i, 0)),
                  pl.BlockSpec((D_MODEL, D_MODEL), lambda i: (0, 0), pipeline_mode=pl.Buffered(1)),
                  pl.BlockSpec((tm, D_MODEL), lambda i: (i, 0)),
                  mod_spec, mod_spec, mod_spec,
                  pl.BlockSpec((1, D_MODEL), lambda i: (0, 0))],
        out_specs=[pl.BlockSpec((tm, D_MODEL), lambda i: (i, 0)),
                   pl.BlockSpec((tm, D_MODEL), lambda i: (i, 0))],
        out_shape=[jax.ShapeDtypeStruct((TOKENS, D_MODEL), F32),
                   jax.ShapeDtypeStruct((TOKENS, D_MODEL), BF16)],
        compiler_params=_params(("arbitrary",)),
        name="outproj",
    )(merged, w_o, x2d, g1, sh2, sc2, nw2)


def _glu_kernel(h_ref, hp_ref, hn_ref, wa_ref, wg_ref, fw_ref, fb_ref, u_ref,
                hext_scr, a_even, g_even, a_odd, g_odd, *, n_chunks, n_steps):
    s = pl.program_id(0)
    tm = h_ref.shape[0]
    cur = jnp.minimum(s, n_steps - 2)
    i = cur // n_chunks

    @pl.when(s == 0)
    def _():
        a_odd[...] = jnp.zeros(a_odd.shape, F32)
        g_odd[...] = jnp.zeros(g_odd.shape, F32)

    @pl.when(cur % n_chunks == 0)
    def _():
        has_prev = ((i * tm) % SEQ) != 0
        has_next = (((i + 1) * tm) % SEQ) != 0
        hext_scr[0:GRID_W, :] = jnp.where(has_prev, hp_ref[...], jnp.zeros_like(hp_ref[...]))
        hext_scr[GRID_W:GRID_W + tm, :] = h_ref[...]
        hext_scr[GRID_W + tm:, :] = jnp.where(has_next, hn_ref[...], jnp.zeros_like(hn_ref[...]))

    def step(wr_a, wr_g, rd_a, rd_g):
        fw = 0.5 * fw_ref[...]
        fb = 0.5 * fb_ref[...]
        for r in range(0, tm, GLU_ROWS):
            hv = (rd_g[r:r + GLU_ROWS, :] * fw[0:1, :] + fb
                  + rd_g[r + GRID_W:r + GRID_W + GLU_ROWS, :] * fw[1:2, :]
                  + rd_g[r + 2 * GRID_W:r + 2 * GRID_W + GLU_ROWS, :] * fw[2:3, :])
            gelu = hv * (1.0 + lax.erf(hv * np.float32(np.sqrt(2.0))))
            u_ref[r:r + GLU_ROWS, :] = (gelu * rd_a[r:r + GLU_ROWS, :]).astype(u_ref.dtype)
        wr_a[...] = _dot(hext_scr[GRID_W:GRID_W + tm, :], wa_ref[...])
        wr_g[...] = _dot(hext_scr[...], wg_ref[...])

    @pl.when(s % 2 == 0)
    def _():
        step(a_even, g_even, a_odd, g_odd)

    @pl.when(s % 2 == 1)
    def _():
        step(a_odd, g_odd, a_even, g_even)


def _glu(h2, w_up, ffn_conv_w, ffn_conv_b, tm=2048, tn=GLU_TN):
    n_chunks = D_FF // tn
    n_tiles = TOKENS // tm
    n_steps = n_tiles * n_chunks + 1
    hb = tm // GRID_W
    n_halo = TOKENS // GRID_W

    def cur(s):
        return jnp.minimum(s, n_steps - 2)

    def prev(s):
        return jnp.maximum(s - 1, 0)

    return pl.pallas_call(
        functools.partial(_glu_kernel, n_chunks=n_chunks, n_steps=n_steps),
        grid=(n_steps,),
        in_specs=[pl.BlockSpec((tm, D_MODEL), lambda s: (cur(s) // n_chunks, 0),
                               pipeline_mode=pl.Buffered(1)),
                  pl.BlockSpec((GRID_W, D_MODEL),
                               lambda s: (jnp.maximum((cur(s) // n_chunks) * hb - 1, 0), 0)),
                  pl.BlockSpec((GRID_W, D_MODEL),
                               lambda s: (jnp.minimum((cur(s) // n_chunks + 1) * hb, n_halo - 1), 0)),
                  pl.BlockSpec((D_MODEL, tn), lambda s: (0, cur(s) % n_chunks)),
                  pl.BlockSpec((D_MODEL, tn), lambda s: (0, n_chunks + cur(s) % n_chunks)),
                  pl.BlockSpec((3, tn), lambda s: (0, prev(s) % n_chunks)),
                  pl.BlockSpec((1, tn), lambda s: (0, prev(s) % n_chunks))],
        out_specs=pl.BlockSpec((tm, tn), lambda s: (prev(s) // n_chunks, prev(s) % n_chunks)),
        out_shape=jax.ShapeDtypeStruct((TOKENS, D_FF), BF16),
        scratch_shapes=[pltpu.VMEM((tm + 2 * GRID_W, D_MODEL), BF16),
                        pltpu.VMEM((tm, tn), F32), pltpu.VMEM((tm + 2 * GRID_W, tn), F32),
                        pltpu.VMEM((tm, tn), F32), pltpu.VMEM((tm + 2 * GRID_W, tn), F32)],
        compiler_params=_params(("arbitrary",)),
        name="glu",
    )(h2, h2, h2, w_up, w_up, ffn_conv_w, ffn_conv_b)


def _down_kernel(u_ref, wd_ref, x1_ref, g2_ref, nw_ref, o_ref):
    x2 = x1_ref[...] + g2_ref[0] * _dot(u_ref[...], wd_ref[...])
    ms = jnp.mean(x2 * x2, axis=-1, keepdims=True)
    o_ref[...] = x2 * lax.rsqrt(ms + EPS) * nw_ref[...]


def _down(u, w_down, x1, g2, fnw, tm=512):
    tpb = SEQ // tm
    return pl.pallas_call(
        _down_kernel,
        grid=(TOKENS // tm,),
        in_specs=[pl.BlockSpec((tm, D_FF), lambda i: (i, 0)),
                  pl.BlockSpec((D_FF, D_MODEL), lambda i: (0, 0), pipeline_mode=pl.Buffered(1)),
                  pl.BlockSpec((tm, D_MODEL), lambda i: (i, 0)),
                  pl.BlockSpec((1, 1, D_MODEL), lambda i: (i // tpb, 0, 0)),
                  pl.BlockSpec((1, D_MODEL), lambda i: (0, 0))],
        out_specs=pl.BlockSpec((tm, D_MODEL), lambda i: (i, 0)),
        out_shape=jax.ShapeDtypeStruct((TOKENS, D_MODEL), F32),
        compiler_params=_params(("arbitrary",)),
        name="down",
    )(u, w_down, x1, g2, fnw)


def _dt_rows(dt_raw, batch, n_chunks):
    d = dt_raw[:, :2 * SSD_HEADS].reshape(batch, n_chunks, CHUNK, 2, SSD_GROUPS, SSD_REP)
    d = jnp.transpose(d, (4, 0, 1, 3, 5, 2))
    return d.reshape(SSD_GROUPS, batch, n_chunks, 2 * SSD_REP, CHUNK)


def _group_rows(p):
    p = p.reshape(2, SSD_GROUPS, SSD_REP)
    return jnp.transpose(p, (1, 0, 2)).reshape(SSD_GROUPS, 2 * SSD_REP, 1)


def kernel(x, c, ctx, c_ctx, w_ada, b_ada, norm1_w, w_in, ssd_conv_w, ssd_conv_b, dt_bias, a_log,
           d_skip, ssd_norm_w, w_ssd_out, pool_w, pool_scale, w_pool_out, w_o, norm2_w, w_up,
           ffn_conv_w, ffn_conv_b, w_down, final_norm_w):
    assert w_in.shape[0] == 1, "single-layer block only"
    d = D_MODEL
    x2d = x.reshape(TOKENS, d)
    ctx2d = ctx.reshape(BATCH * CTX_LEN, d)

    cs = jnp.zeros((8, d), F32).at[:BATCH].set(c).at[BATCH].set(c_ctx)
    mod = _ada(cs, w_ada[0], b_ada)
    sh1, sc1, g1, sh2, sc2, g2 = [mod[:BATCH, k * d:(k + 1) * d].reshape(BATCH, 1, d)
                                  for k in range(6)]
    csh1 = mod[BATCH:BATCH + 1, 0:d].reshape(1, 1, d)
    csc1 = mod[BATCH:BATCH + 1, d:2 * d].reshape(1, 1, d)

    w_in0 = w_in[0]
    n_dt = 2 * SSD_HEADS
    w_cat, w_dt = _drop_leading_cols(jnp.transpose(w_in0), n_dt)
    w_s = w_ssd_out[0].astype(BF16)
    w_p = w_pool_out[0].astype(BF16)
    w_ob = w_o[0].astype(BF16)
    w_upb = w_up[0].astype(BF16)
    w_db = w_down[0].astype(BF16)
    pool_wb = pool_w[0].astype(BF16)

    proj, dt_raw = _inproj(x2d, sh1, sc1, norm1_w, w_cat, w_dt, n_cols=N_PROJ, rows_per_mod=SEQ)
    cproj, cdt_raw = _inproj(ctx2d, csh1, csc1, norm1_w, w_cat, w_dt, n_cols=N_CTX_PROJ,
                             rows_per_mod=BATCH * CTX_LEN)

    dtr = _dt_rows(dt_raw, BATCH, N_CHUNKS)
    cdtr = _dt_rows(cdt_raw, BATCH, N_CTX_CHUNKS)
    dsk = jnp.repeat(jnp.transpose(d_skip[0].reshape(2, SSD_GROUPS, SSD_REP), (1, 0, 2)),
                     SSD_HEAD_DIM, axis=-1)
    idx = np.arange(CHUNK)
    tri = jnp.asarray(np.stack([idx[:, None] <= idx[None, :], idx[:, None] >= idx[None, :]])
                      .astype(np.float32), BF16)

    y_ssd = _ssd(proj.reshape(BATCH, SEQ, N_PROJ), dtr,
                 cproj.reshape(BATCH, CTX_LEN, N_CTX_PROJ), cdtr,
                 ssd_conv_w[0], ssd_conv_b, _group_rows(dt_bias[0]), _group_rows(a_log[0]),
                 dsk, ssd_norm_w, tri)
    y_pool = _pool(proj, pool_wb, pool_scale)
    merged = _merge(y_ssd.reshape(TOKENS, d), y_pool, w_s, w_p, proj)
    x1, h2 = _outproj(merged, w_ob, x2d, g1, sh2, sc2, norm2_w)
    u = _glu(h2, w_upb, ffn_conv_w[0], ffn_conv_b)
    out = _down(u, w_db, x1, g2, final_norm_w.reshape(1, d))
    return out.reshape(BATCH, SEQ, d)
```

```python
import functools

import numpy as np
import jax
import jax.numpy as jnp
from jax import lax
from jax.experimental import pallas as pl
from jax.experimental.pallas import tpu as pltpu

F32 = jnp.float32
BF16 = jnp.bfloat16

D_MODEL = 2048
BATCH = 4
SEQ = 4096
TOKENS = BATCH * SEQ
GRID_W = 64
CTX_LEN = 256

SSD_HEAD_DIM = 64
SSD_HEADS = 32
SSD_GROUPS = 8
SSD_REP = 4
SSD_STATE = 128
CHUNK = 128
GROUP_W = SSD_REP * SSD_HEAD_DIM
N_CHUNKS = SEQ // CHUNK
N_CTX_CHUNKS = CTX_LEN // CHUNK
SCAN_UNROLL = 16
GLU_ROWS = 16
DOT_ROWS = 1024
INPROJ_TN, MERGE_TN, GLU_TN = 1536, 512, 512

POOL_WINDOWS = (2, 4, 8, 16)
POOL_GW = 512
D_FF = 5632
EPS = 1e-6

COL_X = 0
COL_B = 2048
COL_C = 3072
COL_Z = 4096
COL_V = 6144
COL_GS = 8192
COL_GP = 10240
N_PROJ = 12288
N_CTX_PROJ = 3072

LANES = 128
VMEM_LIMIT = 56 * 1024 * 1024


def _dot(a, b):
    return jnp.dot(a, b, preferred_element_type=F32)


def _silu(x):
    return x * jax.nn.sigmoid(x)


def _softplus(x):
    return jnp.maximum(x, 0.0) + jnp.log1p(jnp.exp(-jnp.abs(x)))


def _params(sem):
    return pltpu.CompilerParams(dimension_semantics=sem, vmem_limit_bytes=VMEM_LIMIT)


def _shift_cast_kernel(a_ref, b_ref, head_ref, o_ref, head_o_ref, *, shift):
    tn = o_ref.shape[1]

    @pl.when(pl.program_id(0) == 0)
    def _():
        head_o_ref[...] = head_ref[...].T

    x = jnp.concatenate([a_ref[shift:, :], b_ref[...]], axis=0)
    for r in range(0, tn, LANES):
        o_ref[:, r:r + LANES] = x[r:r + LANES, :].T.astype(o_ref.dtype)


def _drop_leading_cols(w_t, shift, tn=1024):
    n, k = w_t.shape
    n_out = n - shift
    return pl.pallas_call(
        functools.partial(_shift_cast_kernel, shift=shift),
        grid=(n_out // tn,),
        in_specs=[pl.BlockSpec((tn, k), lambda j: (j, 0)),
                  pl.BlockSpec((shift, k), lambda j: ((j + 1) * (tn // shift), 0)),
                  pl.BlockSpec((LANES, k), lambda j: (0, 0))],
        out_specs=[pl.BlockSpec((k, tn), lambda j: (0, j)),
                   pl.BlockSpec((k, LANES), lambda j: (0, 0))],
        out_shape=[jax.ShapeDtypeStruct((k, n_out), BF16),
                   jax.ShapeDtypeStruct((k, LANES), F32)],
        compiler_params=_params(("arbitrary",)),
        name="wprep",
    )(w_t, w_t, w_t)


def _ada_kernel(c_ref, w_ref, b_ref, o_ref):
    s_hi, s_lo = _split_bf16(_silu(c_ref[...]), 2)
    lhs = jnp.concatenate([s_hi, s_lo], axis=0).astype(BF16)
    w = w_ref[...]
    w_hi = w.astype(BF16)
    w_lo = (w - w_hi.astype(F32)).astype(BF16)
    r = _dot(lhs, w_hi) + _dot(lhs, w_lo)
    o_ref[...] = r[0:8] + r[8:16] + b_ref[...]


def _ada(cs, w, b):
    tn = 1024
    n = w.shape[1]
    return pl.pallas_call(
        _ada_kernel,
        grid=(n // tn,),
        in_specs=[pl.BlockSpec((8, D_MODEL), lambda j: (0, 0)),
                  pl.BlockSpec((D_MODEL, tn), lambda j: (0, j)),
                  pl.BlockSpec((1, tn), lambda j: (0, j))],
        out_specs=pl.BlockSpec((8, tn), lambda j: (0, j)),
        out_shape=jax.ShapeDtypeStruct((8, n), F32),
        compiler_params=_params(("arbitrary",)),
        name="ada",
    )(cs, w, b)


def _modulate_rows(x_ref, nw, sc, sh, out_ref, rows, rc=256):
    gain = nw * (1.0 + sc)

    def body(k, carry):
        r = pl.multiple_of(k * rc, rc)
        x = x_ref[pl.ds(r, rc), :]
        inv = lax.rsqrt(jnp.mean(x * x, axis=-1, keepdims=True) + EPS)
        out_ref[pl.ds(r, rc), :] = (x_ref[pl.ds(r, rc), :] * inv * gain + sh).astype(out_ref.dtype)
        return carry
    lax.fori_loop(0, rows // rc, body, 0)


def _inproj_kernel(x_ref, sh_ref, sc_ref, nw_ref, w_ref, wdt_ref, o_ref, dt_ref, h_scr):
    @pl.when(pl.program_id(1) == 0)
    def _():
        _modulate_rows(x_ref, nw_ref[...], sc_ref[0], sh_ref[0], h_scr, h_scr.shape[0])
        dt_ref[...] = _dot(h_scr[...], wdt_ref[...].astype(BF16))

    o_ref[...] = _dot(h_scr[...], w_ref[...]).astype(o_ref.dtype)


def _inproj(x2d, sh, sc, nw, w_cat, w_dt, *, n_cols, rows_per_mod, tm=1024, tn=INPROJ_TN):
    m = x2d.shape[0]
    tpb = rows_per_mod // tm
    return pl.pallas_call(
        _inproj_kernel,
        grid=(m // tm, n_cols // tn),
        in_specs=[pl.BlockSpec((tm, D_MODEL), lambda i, j: (i, 0)),
                  pl.BlockSpec((1, 1, D_MODEL), lambda i, j: (i // tpb, 0, 0)),
                  pl.BlockSpec((1, 1, D_MODEL), lambda i, j: (i // tpb, 0, 0)),
                  pl.BlockSpec((1, D_MODEL), lambda i, j: (0, 0)),
                  pl.BlockSpec((D_MODEL, tn), lambda i, j: (0, j)),
                  pl.BlockSpec((D_MODEL, LANES), lambda i, j: (0, 0))],
        out_specs=[pl.BlockSpec((tm, tn), lambda i, j: (i, j)),
                   pl.BlockSpec((tm, LANES), lambda i, j: (i, 0))],
        out_shape=[jax.ShapeDtypeStruct((m, n_cols), BF16),
                   jax.ShapeDtypeStruct((m, LANES), F32)],
        scratch_shapes=[pltpu.VMEM((tm, D_MODEL), BF16)],
        compiler_params=_params(("arbitrary", "arbitrary")),
        name="inproj",
    )(x2d, sh, sc, nw, w_cat, w_dt)


def _conv_silu_tile(ref, c, n_chunks, w, bias):
    length = n_chunks * CHUNK
    if isinstance(c, int):
        r0, p0, n0 = c * CHUNK, max(c * CHUNK - 16, 0), min(c * CHUNK + CHUNK, length - 16)
    else:
        r0 = pl.multiple_of(c * CHUNK, CHUNK)
        p0 = pl.multiple_of(jnp.maximum(c * CHUNK - 16, 0), 16)
        n0 = pl.multiple_of(jnp.minimum(c * CHUNK + CHUNK, length - 16), 16)
    x = ref[0, pl.ds(r0, CHUNK), :].astype(F32)
    prev_row = ref[0, pl.ds(p0, 16), :].astype(F32)[15:16, :]
    prev_row = jnp.where(c > 0, prev_row, 0.0)
    next_row = ref[0, pl.ds(n0, 16), :].astype(F32)[0:1, :]
    next_row = jnp.where(c < n_chunks - 1, next_row, 0.0)
    rows = lax.broadcasted_iota(jnp.int32, (8, x.shape[1]), 0)
    xm = pltpu.roll(x, 1, axis=0)
    xm = jnp.concatenate([jnp.where(rows == 0, prev_row, xm[0:8]), xm[8:]], axis=0)
    xp = pltpu.roll(x, CHUNK - 1, axis=0)
    xp = jnp.concatenate([xp[:CHUNK - 8], jnp.where(rows == 7, next_row, xp[CHUNK - 8:])], axis=0)
    y = xm * w[0:1, :] + x * w[1:2, :] + xp * w[2:3, :] + bias
    return _silu(y)


def _split_bf16(x, parts):
    out = []
    for _ in range(parts - 1):
        hi = x.astype(BF16).astype(F32)
        out.append(hi)
        x = x - hi
    out.append(x.astype(BF16).astype(F32))
    return out


TAB_ACS, TAB_ONE, TAB_EXP, TAB_W, TAB_DEC, TAB_USED = 0, 3, 6, 8, 10, 12
SEG_ROWS = 48


def _decay_tables(dt_raw, dt_bias, a_neg, tri_ref, a_scr, bsel_scr=None, dt_scr=None):
    n_chunks = dt_raw.shape[0]
    n = n_chunks * 8
    dt = _softplus(dt_raw + dt_bias)
    a = (dt * a_neg).reshape(n, CHUNK)
    dt = dt.reshape(n, CHUNK)
    hi, lo = _split_bf16(a, 2)
    lhs = jnp.concatenate([hi, lo], axis=0).astype(BF16)
    pre = _dot(lhs, tri_ref[0])
    suf = _dot(lhs, tri_ref[1])
    row = lax.broadcasted_iota(jnp.int32, (n, CHUNK), 0)
    fwd = jnp.bitwise_and(row, 7) < SSD_REP
    acs = jnp.where(fwd, pre[:n] + pre[n:], suf[:n] + suf[n:])
    tot = jnp.where(fwd[:, 0:1], acs[:, CHUNK - 1:CHUNK], acs[:, 0:1])
    ones = jnp.ones((n, CHUNK), F32)
    comps_a = (_split_bf16(acs, 3) + [ones, ones, ones] + _split_bf16(jnp.exp(acs), 2)
               + _split_bf16(jnp.exp(tot - acs) * dt, 2)
               + _split_bf16(jnp.broadcast_to(jnp.exp(tot), acs.shape), 2))
    comps_b = [ones, ones, ones] + [-p for p in _split_bf16(acs, 3)]
    row48 = jnp.bitwise_and(lax.broadcasted_iota(jnp.int32, (SEG_ROWS, CHUNK), 0), 7)
    pad = jnp.zeros((CHUNK - 8 * TAB_USED, CHUNK), F32)
    for c in range(n_chunks):
        sl = slice(c * 8, c * 8 + 8)
        a_scr[c] = jnp.concatenate([p[sl] for p in comps_a] + [pad], axis=0).T.astype(BF16)
        if bsel_scr is None:
            continue
        b_all = jnp.concatenate([p[sl] for p in comps_b], axis=0)
        for k in range(8):
            bsel_scr[c, :, k * CHUNK:(k + 1) * CHUNK] = jnp.where(row48 == k, b_all, 0.0).astype(BF16)
        dt_scr[c] = dt[sl]


def _split_heads(t):
    lane = lax.broadcasted_iota(jnp.int32, t.shape, 1)
    return (jnp.where(lane < SSD_HEAD_DIM, t, 0.0).astype(BF16),
            jnp.where(lane >= SSD_HEAD_DIM, t, 0.0).astype(BF16))


def _chunk_dir(xs, x_pairs, bt, cc, cb, tab, bsel, dt, sel, off, h_ref, causal):
    l_io = lax.broadcasted_iota(jnp.int32, (CHUNK, CHUNK), 0)
    s_io = lax.broadcasted_iota(jnp.int32, (CHUNK, CHUNK), 1)
    mask = (l_io >= s_io) if causal else (l_io <= s_io)
    seg = _dot(tab, jnp.concatenate([bsel, jnp.zeros((CHUNK - SEG_ROWS, bsel.shape[1]), BF16)],
                                    axis=0))
    spread = _dot(tab, sel)
    e_exp, w_exp, dec = spread[:, 0:GROUP_W], spread[:, GROUP_W:2 * GROUP_W], spread[0:1, 2 * GROUP_W:]
    h = h_ref[...]
    h_ref[...] = h * dec + _dot(bt, (xs * w_exp).astype(BF16))
    y = _dot(cc, h.astype(BF16)) * e_exp
    ys = []
    for pair in range(2):
        ms = []
        for r in (2 * pair, 2 * pair + 1):
            decay = jnp.exp(jnp.where(mask, seg[:, r * CHUNK:(r + 1) * CHUNK], -jnp.inf))
            ms.append((cb * decay * dt[off + r:off + r + 1, :]).astype(BF16))
        ys.append(_dot(jnp.concatenate(ms, axis=1), x_pairs[pair]))
    return y + jnp.concatenate(ys, axis=1)


def _ssd_kernel(xs_ref, b_ref, c_ref, z_ref, dtr_ref, cxs_ref, cbm_ref, cdtr_ref,
                cwx_ref, cwb_ref, cwc_ref, bx_ref, bb_ref, bc_ref,
                dtb_ref, alog_ref, dsk_ref, nw_ref, tri_ref, sel_ref,
                y_ref,
                xs_scr, xp_scr, bt_scr, cc_scr, cb_scr, y_scr, hf_scr, hb_scr,
                tab_scr, bsel_scr, dt_scr, ctab_scr):
    dt_bias = dtb_ref[0]
    a_neg = -jnp.exp(alog_ref[0])
    cwx, cwb, cwc = cwx_ref[...], cwb_ref[...], cwc_ref[...]
    bx, bb, bc = bx_ref[...], bb_ref[...], bc_ref[...]

    hf_scr[...] = jnp.zeros(hf_scr.shape, F32)
    hb_scr[...] = jnp.zeros(hb_scr.shape, F32)
    _decay_tables(cdtr_ref[0, 0], dt_bias, a_neg, tri_ref, ctab_scr)
    ctx_tiles = []
    for c in range(N_CTX_CHUNKS):
        xs = _conv_silu_tile(cxs_ref, c, N_CTX_CHUNKS, cwx, bx)
        bt = _conv_silu_tile(cbm_ref, c, N_CTX_CHUNKS, cwb, bb).T.astype(BF16)
        ctx_tiles.append((xs, bt))
    for d, h_ref, order in ((0, hf_scr, range(N_CTX_CHUNKS)), (1, hb_scr, reversed(range(N_CTX_CHUNKS)))):
        for c in order:
            xs, bt = ctx_tiles[c]
            spread = _dot(ctab_scr[c], sel_ref[d])
            w_exp, dec = spread[:, GROUP_W:2 * GROUP_W], spread[0:1, 2 * GROUP_W:]
            h_ref[...] = h_ref[...] * dec + _dot(bt, (xs * w_exp).astype(BF16))

    _decay_tables(dtr_ref[0, 0], dt_bias, a_neg, tri_ref, tab_scr, bsel_scr, dt_scr)

    def prepare(c):
        r0 = pl.multiple_of(c * CHUNK, CHUNK)
        xs = _conv_silu_tile(xs_ref, c, N_CHUNKS, cwx, bx)
        bt = _conv_silu_tile(b_ref, c, N_CHUNKS, cwb, bb).T.astype(BF16)
        cc = _conv_silu_tile(c_ref, c, N_CHUNKS, cwc, bc).astype(BF16)
        cb = _dot(cc, bt)
        x_pairs = [jnp.concatenate(_split_heads(xs[:, p * LANES:(p + 1) * LANES]), axis=0)
                   for p in range(2)]
        xs_scr[pl.ds(r0, CHUNK), :] = xs
        xp_scr[c, 0] = x_pairs[0]
        xp_scr[c, 1] = x_pairs[1]
        bt_scr[c] = bt
        cc_scr[c] = cc
        cb_scr[c] = cb
        return xs, x_pairs, bt, cc, cb

    def recall(c):
        r0 = pl.multiple_of(c * CHUNK, CHUNK)
        return (xs_scr[pl.ds(r0, CHUNK), :], [xp_scr[c, 0], xp_scr[c, 1]], bt_scr[c], cc_scr[c],
                cb_scr[c])

    def scan(c, tiles, causal):
        xs, x_pairs, bt, cc, cb = tiles
        d = 0 if causal else 1
        bsel = bsel_scr[c, :, d * SSD_REP * CHUNK:(d + 1) * SSD_REP * CHUNK]
        return _chunk_dir(xs, x_pairs, bt, cc, cb, tab_scr[c], bsel, dt_scr[c], sel_ref[d],
                          d * SSD_REP, hf_scr if causal else hb_scr, causal)

    d_sum = dsk_ref[0, 0:1, :] + dsk_ref[0, 1:2, :]
    nw = nw_ref[...]

    def finish(c, y_dir, xs):
        r0 = pl.multiple_of(c * CHUNK, CHUNK)
        y = y_scr[pl.ds(r0, CHUNK), :] + y_dir + xs * d_sum
        gated = y * _silu(z_ref[0, pl.ds(r0, CHUNK), :].astype(F32))
        ms = jnp.mean(gated * gated, axis=-1, keepdims=True)
        y_ref[0, pl.ds(r0, CHUNK), :] = (gated * lax.rsqrt(ms + EPS) * nw).astype(y_ref.dtype)

    def visits(i):
        fwd = [i * SCAN_UNROLL + u for u in range(SCAN_UNROLL)]
        return [(c, True) for c in fwd] + [(N_CHUNKS - 1 - c, False) for c in fwd]

    def first_half(i, carry):
        for c, causal in visits(i):
            r0 = pl.multiple_of(c * CHUNK, CHUNK)
            y_scr[pl.ds(r0, CHUNK), :] = scan(c, prepare(c), causal)
        return carry

    def second_half(i, carry):
        for c, causal in visits(i):
            tiles = recall(c)
            finish(c, scan(c, tiles, causal), tiles[0])
        return carry

    half = N_CHUNKS // 2 // SCAN_UNROLL
    lax.fori_loop(0, half, first_half, 0)
    lax.fori_loop(half, 2 * half, second_half, 0)


def _ssd(proj, dtr, cproj, cdtr, conv_w, conv_b, dtb, alog, dsk, norm_w, tri):
    gx = COL_X // GROUP_W
    gz = COL_Z // GROUP_W
    gb = COL_B // SSD_STATE
    gc = COL_C // SSD_STATE
    in_specs = [
        pl.BlockSpec((1, SEQ, GROUP_W), lambda b, g: (b, 0, gx + g)),
        pl.BlockSpec((1, SEQ, SSD_STATE), lambda b, g: (b, 0, gb + g)),
        pl.BlockSpec((1, SEQ, SSD_STATE), lambda b, g: (b, 0, gc + g)),
        pl.BlockSpec((1, SEQ, GROUP_W), lambda b, g: (b, 0, gz + g)),
        pl.BlockSpec((1, 1, N_CHUNKS, 8, CHUNK), lambda b, g: (g, b, 0, 0, 0)),
        pl.BlockSpec((1, CTX_LEN, GROUP_W), lambda b, g: (b, 0, gx + g)),
        pl.BlockSpec((1, CTX_LEN, SSD_STATE), lambda b, g: (b, 0, gb + g)),
        pl.BlockSpec((1, 1, N_CTX_CHUNKS, 8, CHUNK), lambda b, g: (g, b, 0, 0, 0)),
        pl.BlockSpec((3, GROUP_W), lambda b, g: (0, gx + g)),
        pl.BlockSpec((3, SSD_STATE), lambda b, g: (0, gb + g)),
        pl.BlockSpec((3, SSD_STATE), lambda b, g: (0, gc + g)),
        pl.BlockSpec((1, GROUP_W), lambda b, g: (0, gx + g)),
        pl.BlockSpec((1, SSD_STATE), lambda b, g: (0, gb + g)),
        pl.BlockSpec((1, SSD_STATE), lambda b, g: (0, gc + g)),
        pl.BlockSpec((1, 8, 1), lambda b, g: (g, 0, 0)),
        pl.BlockSpec((1, 8, 1), lambda b, g: (g, 0, 0)),
        pl.BlockSpec((1, 2, GROUP_W), lambda b, g: (g, 0, 0)),
        pl.BlockSpec((1, GROUP_W), lambda b, g: (0, g)),
        pl.BlockSpec((2, CHUNK, CHUNK), lambda b, g: (0, 0, 0)),
        pl.BlockSpec((2, CHUNK, 3 * GROUP_W), lambda b, g: (0, 0, 0)),
    ]
    return pl.pallas_call(
        _ssd_kernel,
        grid=(BATCH, SSD_GROUPS),
        in_specs=in_specs,
        out_specs=pl.BlockSpec((1, SEQ, GROUP_W), lambda b, g: (b, 0, g)),
        out_shape=jax.ShapeDtypeStruct((BATCH, SEQ, D_MODEL), BF16),
        scratch_shapes=[pltpu.VMEM((SEQ, GROUP_W), F32),
                        pltpu.VMEM((N_CHUNKS, 2, 2 * CHUNK, LANES), BF16),
                        pltpu.VMEM((N_CHUNKS, CHUNK, CHUNK), BF16),
                        pltpu.VMEM((N_CHUNKS, CHUNK, CHUNK), BF16),
                        pltpu.VMEM((N_CHUNKS, CHUNK, CHUNK), F32),
                        pltpu.VMEM((SEQ, GROUP_W), F32),
                        pltpu.VMEM((SSD_STATE, GROUP_W), F32),
                        pltpu.VMEM((SSD_STATE, GROUP_W), F32),
                        pltpu.VMEM((N_CHUNKS, CHUNK, CHUNK), BF16),
                        pltpu.VMEM((N_CHUNKS, SEG_ROWS, 8 * CHUNK), BF16),
                        pltpu.VMEM((N_CHUNKS, 8, CHUNK), F32),
                        pltpu.VMEM((N_CTX_CHUNKS, CHUNK, CHUNK), BF16)],
        compiler_params=_params(("arbitrary", "arbitrary")),
        name="ssd",
    )(proj, proj, proj, proj, dtr, cproj, cproj, cdtr,
      conv_w, conv_w, conv_w, conv_b, conv_b, conv_b, dtb, alog, dsk, norm_w, tri, _spread_matrix())


def _spread_matrix():
    sel = np.zeros((2, CHUNK, 3 * GROUP_W), np.float32)
    for d in range(2):
        for grp, comp in enumerate((TAB_EXP, TAB_W, TAB_DEC)):
            for part in range(2):
                for h in range(SSD_REP):
                    lanes = slice(grp * GROUP_W + h * SSD_HEAD_DIM,
                                  grp * GROUP_W + (h + 1) * SSD_HEAD_DIM)
                    sel[d, (comp + part) * 8 + d * SSD_REP + h, lanes] = 1.0
    return jnp.asarray(sel, BF16)


def _pool_kernel(v_ref, win_ref, inv_ref, pw_ref, ps_ref, o_ref, p_scr):
    slab = win_ref.shape[1]
    for r in range(0, v_ref.shape[0], slab):
        v = v_ref[r:r + slab, :]
        p_scr[r:r + slab, :] = (_dot(win_ref[0], v) * inv_ref[0] - v.astype(F32)).astype(BF16)
    for r in range(0, v_ref.shape[0], DOT_ROWS):
        o_ref[r:r + DOT_ROWS, :] = (_dot(p_scr[r:r + DOT_ROWS, :], pw_ref[0])
                                    * ps_ref[...]).astype(o_ref.dtype)


def _pool_constants(tm):
    t = np.arange(tm)
    col = t % GRID_W
    wins, invs = [], []
    for w in POOL_WINDOWS:
        start = np.clip(col - w // 2, 0, GRID_W)
        end = np.clip(col + w - w // 2, 0, GRID_W)
        same_row = (t[:, None] // GRID_W) == (t[None, :] // GRID_W)
        inside = (col[None, :] >= start[:, None]) & (col[None, :] < end[:, None])
        wins.append((same_row & inside).astype(np.float32))
        invs.append((1.0 / (end - start).astype(np.float32))[:, None])
    return jnp.asarray(np.stack(wins), BF16), jnp.asarray(np.stack(invs), F32)


def _pool(proj2d, pool_w, pool_scale, tm=2048, slab=128):
    win, inv = _pool_constants(slab)
    gv = COL_V // POOL_GW
    return pl.pallas_call(
        _pool_kernel,
        grid=(TOKENS // tm, len(POOL_WINDOWS)),
        in_specs=[pl.BlockSpec((tm, POOL_GW), lambda i, k: (i, gv + k)),
                  pl.BlockSpec((1, slab, slab), lambda i, k: (k, 0, 0)),
                  pl.BlockSpec((1, slab, 1), lambda i, k: (k, 0, 0)),
                  pl.BlockSpec((1, POOL_GW, POOL_GW), lambda i, k: (k, 0, 0)),
                  pl.BlockSpec((1, POOL_GW), lambda i, k: (0, k))],
        out_specs=pl.BlockSpec((tm, POOL_GW), lambda i, k: (i, k)),
        out_shape=jax.ShapeDtypeStruct((TOKENS, D_MODEL), BF16),
        scratch_shapes=[pltpu.VMEM((tm, POOL_GW), BF16)],
        compiler_params=_params(("arbitrary", "arbitrary")),
        name="pool",
    )(proj2d, win, inv, pool_w, pool_scale)


def _merge_kernel(ys_ref, yp_ref, ws_ref, wp_ref, gs_ref, gp_ref, o_ref):
    a = jax.nn.sigmoid(gs_ref[...].astype(F32)) * _dot(ys_ref[...], ws_ref[...])
    b = jax.nn.sigmoid(gp_ref[...].astype(F32)) * _dot(yp_ref[...], wp_ref[...])
    o_ref[...] = (a + b).astype(o_ref.dtype)


def _merge(y_ssd, y_pool, w_s, w_p, proj2d, tm=1024, tn=MERGE_TN):
    ggs = COL_GS // tn
    ggp = COL_GP // tn
    return pl.pallas_call(
        _merge_kernel,
        grid=(TOKENS // tm, D_MODEL // tn),
        in_specs=[pl.BlockSpec((tm, D_MODEL), lambda i, j: (i, 0)),
                  pl.BlockSpec((tm, D_MODEL), lambda i, j: (i, 0)),
                  pl.BlockSpec((D_MODEL, tn), lambda i, j: (0, j)),
                  pl.BlockSpec((D_MODEL, tn), lambda i, j: (0, j)),
                  pl.BlockSpec((tm, tn), lambda i, j: (i, ggs + j)),
                  pl.BlockSpec((tm, tn), lambda i, j: (i, ggp + j))],
        out_specs=pl.BlockSpec((tm, tn), lambda i, j: (i, j)),
        out_shape=jax.ShapeDtypeStruct((TOKENS, D_MODEL), BF16),
        compiler_params=_params(("arbitrary", "arbitrary")),
        name="merge",
    )(y_ssd, y_pool, w_s, w_p, proj2d, proj2d)


def _outproj_kernel(m_ref, wo_ref, x_ref, g1_ref, sh_ref, sc_ref, nw_ref, x1_ref, h2_ref):
    x1_ref[...] = x_ref[...] + g1_ref[0] * _dot(m_ref[...], wo_ref[...])
    _modulate_rows(x1_ref, nw_ref[...], sc_ref[0], sh_ref[0], h2_ref, x1_ref.shape[0], rc=128)


def _outproj(merged, w_o, x2d, g1, sh2, sc2, nw2, tm=512):
    tpb = SEQ // tm
    mod_spec = pl.BlockSpec((1, 1, D_MODEL), lambda i: (i // tpb, 0, 0))
    return pl.pallas_call(
        _outproj_kernel,
        grid=(TOKENS // tm,),
        in_specs=[pl.BlockSpec((tm, D_MODEL), lambda i: (i, 0)),
                  pl.BlockSpec((D_MODEL, D_MODEL), lambda i: (0, 0), pipeline_mode=pl.Buffered(1)),
                  pl.BlockSpec((tm, D_MODEL), lambda i: (i, 0)),
                  mod_spec, mod_spec, mod_spec,
                  pl.BlockSpec((1, D_MODEL), lambda i: (0, 0))],
        out_specs=[pl.BlockSpec((tm, D_MODEL), lambda i: (i, 0)),
                   pl.BlockSpec((tm, D_MODEL), lambda i: (i, 0))],
        out_shape=[jax.ShapeDtypeStruct((TOKENS, D_MODEL), F32),
                   jax.ShapeDtypeStruct((TOKENS, D_MODEL), BF16)],
        compiler_params=_params(("arbitrary",)),
        name="outproj",
    )(merged, w_o, x2d, g1, sh2, sc2, nw2)


def _glu_kernel(h_ref, hp_ref, hn_ref, wa_ref, wg_ref, fw_ref, fb_ref, u_ref,
                hext_scr, a_even, g_even, a_odd, g_odd, *, n_chunks, n_steps):
    s = pl.program_id(0)
    tm = h_ref.shape[0]
    cur = jnp.minimum(s, n_steps - 2)
    i = cur // n_chunks

    @pl.when(s == 0)
    def _():
        a_odd[...] = jnp.zeros(a_odd.shape, F32)
        g_odd[...] = jnp.zeros(g_odd.shape, F32)

    @pl.when(cur % n_chunks == 0)
    def _():
        has_prev = ((i * tm) % SEQ) != 0
        has_next = (((i + 1) * tm) % SEQ) != 0
        hext_scr[0:GRID_W, :] = jnp.where(has_prev, hp_ref[...], jnp.zeros_like(hp_ref[...]))
        hext_scr[GRID_W:GRID_W + tm, :] = h_ref[...]
        hext_scr[GRID_W + tm:, :] = jnp.where(has_next, hn_ref[...], jnp.zeros_like(hn_ref[...]))

    def step(wr_a, wr_g, rd_a, rd_g):
        fw = 0.5 * fw_ref[...]
        fb = 0.5 * fb_ref[...]
        for r in range(0, tm, GLU_ROWS):
            hv = (rd_g[r:r + GLU_ROWS, :] * fw[0:1, :] + fb
                  + rd_g[r + GRID_W:r + GRID_W + GLU_ROWS, :] * fw[1:2, :]
                  + rd_g[r + 2 * GRID_W:r + 2 * GRID_W + GLU_ROWS, :] * fw[2:3, :])
            gelu = hv * (1.0 + lax.erf(hv * np.float32(np.sqrt(2.0))))
            u_ref[r:r + GLU_ROWS, :] = (gelu * rd_a[r:r + GLU_ROWS, :]).astype(u_ref.dtype)
        for r in range(0, tm, DOT_ROWS):
            wr_a[r:r + DOT_ROWS, :] = _dot(hext_scr[GRID_W + r:GRID_W + r + DOT_ROWS, :], wa_ref[...])
        for r in range(0, tm + 2 * GRID_W, DOT_ROWS):
            n = min(DOT_ROWS, tm + 2 * GRID_W - r)
            wr_g[r:r + n, :] = _dot(hext_scr[r:r + n, :], wg_ref[...])

    @pl.when(s % 2 == 0)
    def _():
        step(a_even, g_even, a_odd, g_odd)

    @pl.when(s % 2 == 1)
    def _():
        step(a_odd, g_odd, a_even, g_even)


def _glu(h2, w_up, ffn_conv_w, ffn_conv_b, tm=2048, tn=GLU_TN):
    n_chunks = D_FF // tn
    n_tiles = TOKENS // tm
    n_steps = n_tiles * n_chunks + 1
    hb = tm // GRID_W
    n_halo = TOKENS // GRID_W

    def cur(s):
        return jnp.minimum(s, n_steps - 2)

    def prev(s):
        return jnp.maximum(s - 1, 0)

    return pl.pallas_call(
        functools.partial(_glu_kernel, n_chunks=n_chunks, n_steps=n_steps),
        grid=(n_steps,),
        in_specs=[pl.BlockSpec((tm, D_MODEL), lambda s: (cur(s) // n_chunks, 0),
                               pipeline_mode=pl.Buffered(1)),
                  pl.BlockSpec((GRID_W, D_MODEL),
                               lambda s: (jnp.maximum((cur(s) // n_chunks) * hb - 1, 0), 0)),
                  pl.BlockSpec((GRID_W, D_MODEL),
                               lambda s: (jnp.minimum((cur(s) // n_chunks + 1) * hb, n_halo - 1), 0)),
                  pl.BlockSpec((D_MODEL, tn), lambda s: (0, cur(s) % n_chunks)),
                  pl.BlockSpec((D_MODEL, tn), lambda s: (0, n_chunks + cur(s) % n_chunks)),
                  pl.BlockSpec((3, tn), lambda s: (0, prev(s) % n_chunks)),
                  pl.BlockSpec((1, tn), lambda s: (0, prev(s) % n_chunks))],
        out_specs=pl.BlockSpec((tm, tn), lambda s: (prev(s) // n_chunks, prev(s) % n_chunks)),
        out_shape=jax.ShapeDtypeStruct((TOKENS, D_FF), BF16),
        scratch_shapes=[pltpu.VMEM((tm + 2 * GRID_W, D_MODEL), BF16),
                        pltpu.VMEM((tm, tn), F32), pltpu.VMEM((tm + 2 * GRID_W, tn), F32),
                        pltpu.VMEM((tm, tn), F32), pltpu.VMEM((tm + 2 * GRID_W, tn), F32)],
        compiler_params=_params(("arbitrary",)),
        name="glu",
    )(h2, h2, h2, w_up, w_up, ffn_conv_w, ffn_conv_b)


def _down_kernel(u_ref, wd_ref, x1_ref, g2_ref, nw_ref, o_ref):
    x2 = x1_ref[...] + g2_ref[0] * _dot(u_ref[...], wd_ref[...])
    ms = jnp.mean(x2 * x2, axis=-1, keepdims=True)
    o_ref[...] = x2 * lax.rsqrt(ms + EPS) * nw_ref[...]


def _down(u, w_down, x1, g2, fnw, tm=512):
    tpb = SEQ // tm
    return pl.pallas_call(
        _down_kernel,
        grid=(TOKENS // tm,),
        in_specs=[pl.BlockSpec((tm, D_FF), lambda i: (i, 0)),
                  pl.BlockSpec((D_FF, D_MODEL), lambda i: (0, 0), pipeline_mode=pl.Buffered(1)),
                  pl.BlockSpec((tm, D_MODEL), lambda i: (i, 0)),
                  pl.BlockSpec((1, 1, D_MODEL), lambda i: (i // tpb, 0, 0)),
                  pl.BlockSpec((1, D_MODEL), lambda i: (0, 0))],
        out_specs=pl.BlockSpec((tm, D_MODEL), lambda i: (i, 0)),
        out_shape=jax.ShapeDtypeStruct((TOKENS, D_MODEL), F32),
        compiler_params=_params(("arbitrary",)),
        name="down",
    )(u, w_down, x1, g2, fnw)


def _dt_rows(dt_raw, batch, n_chunks):
    d = dt_raw[:, :2 * SSD_HEADS].reshape(batch, n_chunks, CHUNK, 2, SSD_GROUPS, SSD_REP)
    d = jnp.transpose(d, (4, 0, 1, 3, 5, 2))
    return d.reshape(SSD_GROUPS, batch, n_chunks, 2 * SSD_REP, CHUNK)


def _group_rows(p):
    p = p.reshape(2, SSD_GROUPS, SSD_REP)
    return jnp.transpose(p, (1, 0, 2)).reshape(SSD_GROUPS, 2 * SSD_REP, 1)


def kernel(x, c, ctx, c_ctx, w_ada, b_ada, norm1_w, w_in, ssd_conv_w, ssd_conv_b, dt_bias, a_log,
           d_skip, ssd_norm_w, w_ssd_out, pool_w, pool_scale, w_pool_out, w_o, norm2_w, w_up,
           ffn_conv_w, ffn_conv_b, w_down, final_norm_w):
    assert w_in.shape[0] == 1, "single-layer block only"
    d = D_MODEL
    x2d = x.reshape(TOKENS, d)
    ctx2d = ctx.reshape(BATCH * CTX_LEN, d)

    cs = jnp.zeros((8, d), F32).at[:BATCH].set(c).at[BATCH].set(c_ctx)
    mod = _ada(cs, w_ada[0], b_ada)
    sh1, sc1, g1, sh2, sc2, g2 = [mod[:BATCH, k * d:(k + 1) * d].reshape(BATCH, 1, d)
                                  for k in range(6)]
    csh1 = mod[BATCH:BATCH + 1, 0:d].reshape(1, 1, d)
    csc1 = mod[BATCH:BATCH + 1, d:2 * d].reshape(1, 1, d)

    w_in0 = w_in[0]
    n_dt = 2 * SSD_HEADS
    w_cat, w_dt = _drop_leading_cols(jnp.transpose(w_in0), n_dt)
    w_s = w_ssd_out[0].astype(BF16)
    w_p = w_pool_out[0].astype(BF16)
    w_ob = w_o[0].astype(BF16)
    w_upb = w_up[0].astype(BF16)
    w_db = w_down[0].astype(BF16)
    pool_wb = pool_w[0].astype(BF16)

    proj, dt_raw = _inproj(x2d, sh1, sc1, norm1_w, w_cat, w_dt, n_cols=N_PROJ, rows_per_mod=SEQ)
    cproj, cdt_raw = _inproj(ctx2d, csh1, csc1, norm1_w, w_cat, w_dt, n_cols=N_CTX_PROJ,
                             rows_per_mod=BATCH * CTX_LEN)

    dtr = _dt_rows(dt_raw, BATCH, N_CHUNKS)
    cdtr = _dt_rows(cdt_raw, BATCH, N_CTX_CHUNKS)
    dsk = jnp.repeat(jnp.transpose(d_skip[0].reshape(2, SSD_GROUPS, SSD_REP), (1, 0, 2)),
                     SSD_HEAD_DIM, axis=-1)
    idx = np.arange(CHUNK)
    tri = jnp.asarray(np.stack([idx[:, None] <= idx[None, :], idx[:, None] >= idx[None, :]])
                      .astype(np.float32), BF16)

    y_ssd = _ssd(proj.reshape(BATCH, SEQ, N_PROJ), dtr,
                 cproj.reshape(BATCH, CTX_LEN, N_CTX_PROJ), cdtr,
                 ssd_conv_w[0], ssd_conv_b, _group_rows(dt_bias[0]), _group_rows(a_log[0]),
                 dsk, ssd_norm_w, tri)
    y_pool = _pool(proj, pool_wb, pool_scale)
    merged = _merge(y_ssd.reshape(TOKENS, d), y_pool, w_s, w_p, proj)
    x1, h2 = _outproj(merged, w_ob, x2d, g1, sh2, sc2, norm2_w)
    u = _glu(h2, w_upb, ffn_conv_w[0], ffn_conv_b)
    out = _down(u, w_db, x1, g2, final_norm_w.reshape(1, d))
    return out.reshape(BATCH, SEQ, d)
```

```python
import functools

import numpy as np
import jax
import jax.numpy as jnp
from jax import lax
from jax.experimental import pallas as pl
from jax.experimental.pallas import tpu as pltpu

F32 = jnp.float32
BF16 = jnp.bfloat16

D_MODEL = 2048
BATCH = 4
SEQ = 4096
TOKENS = BATCH * SEQ
GRID_W = 64
CTX_LEN = 256

SSD_HEAD_DIM = 64
SSD_HEADS = 32
SSD_GROUPS = 8
SSD_REP = 4
SSD_STATE = 128
CHUNK = 128
GROUP_W = SSD_REP * SSD_HEAD_DIM
N_CHUNKS = SEQ // CHUNK
N_CTX_CHUNKS = CTX_LEN // CHUNK
SCAN_UNROLL = 16
GLU_ROWS = 16
DOT_ROWS = 1024
INPROJ_TN, MERGE_TN, GLU_TN = 1536, 2048, 512

POOL_WINDOWS = (2, 4, 8, 16)
POOL_GW = 512
D_FF = 5632
EPS = 1e-6

COL_X = 0
COL_B = 2048
COL_C = 3072
COL_Z = 4096
COL_V = 6144
COL_GS = 8192
COL_GP = 10240
N_PROJ = 12288
N_CTX_PROJ = 3072

LANES = 128
VMEM_LIMIT = 56 * 1024 * 1024


def _dot(a, b):
    return jnp.dot(a, b, preferred_element_type=F32)


def _silu(x):
    return x * jax.nn.sigmoid(x)


def _silu_of_twice(h):
    return h + h * jnp.tanh(h)


def _softplus(x):
    return jnp.maximum(x, 0.0) + jnp.log1p(jnp.exp(-jnp.abs(x)))


def _params(sem):
    return pltpu.CompilerParams(dimension_semantics=sem, vmem_limit_bytes=VMEM_LIMIT)


def _shift_cast_kernel(a_ref, b_ref, head_ref, o_ref, head_o_ref, *, shift):
    tn = o_ref.shape[1]

    @pl.when(pl.program_id(0) == 0)
    def _():
        head_o_ref[...] = head_ref[...].T

    x = jnp.concatenate([a_ref[shift:, :], b_ref[...]], axis=0)
    for r in range(0, tn, LANES):
        o_ref[:, r:r + LANES] = x[r:r + LANES, :].T.astype(o_ref.dtype)


def _drop_leading_cols(w_t, shift, tn=1024):
    n, k = w_t.shape
    n_out = n - shift
    return pl.pallas_call(
        functools.partial(_shift_cast_kernel, shift=shift),
        grid=(n_out // tn,),
        in_specs=[pl.BlockSpec((tn, k), lambda j: (j, 0)),
                  pl.BlockSpec((shift, k), lambda j: ((j + 1) * (tn // shift), 0)),
                  pl.BlockSpec((LANES, k), lambda j: (0, 0))],
        out_specs=[pl.BlockSpec((k, tn), lambda j: (0, j)),
                   pl.BlockSpec((k, LANES), lambda j: (0, 0))],
        out_shape=[jax.ShapeDtypeStruct((k, n_out), BF16),
                   jax.ShapeDtypeStruct((k, LANES), F32)],
        compiler_params=_params(("arbitrary",)),
        name="wprep",
    )(w_t, w_t, w_t)


def _ada_kernel(c_ref, w_ref, b_ref, o_ref):
    s_hi, s_lo = _split_bf16(_silu(c_ref[...]), 2)
    lhs = jnp.concatenate([s_hi, s_lo], axis=0).astype(BF16)
    w = w_ref[...]
    w_hi = w.astype(BF16)
    w_lo = (w - w_hi.astype(F32)).astype(BF16)
    r = _dot(lhs, w_hi) + _dot(lhs, w_lo)
    o_ref[...] = r[0:8] + r[8:16] + b_ref[...]


def _ada(cs, w, b):
    tn = 1024
    n = w.shape[1]
    return pl.pallas_call(
        _ada_kernel,
        grid=(n // tn,),
        in_specs=[pl.BlockSpec((8, D_MODEL), lambda j: (0, 0)),
                  pl.BlockSpec((D_MODEL, tn), lambda j: (0, j)),
                  pl.BlockSpec((1, tn), lambda j: (0, j))],
        out_specs=pl.BlockSpec((8, tn), lambda j: (0, j)),
        out_shape=jax.ShapeDtypeStruct((8, n), F32),
        compiler_params=_params(("arbitrary",)),
        name="ada",
    )(cs, w, b)


def _modulate_rows(x_ref, nw, sc, sh, out_ref, rows, rc=256):
    gain = nw * (1.0 + sc)

    def body(k, carry):
        r = pl.multiple_of(k * rc, rc)
        x = x_ref[pl.ds(r, rc), :]
        inv = lax.rsqrt(jnp.mean(x * x, axis=-1, keepdims=True) + EPS)
        out_ref[pl.ds(r, rc), :] = (x_ref[pl.ds(r, rc), :] * inv * gain + sh).astype(out_ref.dtype)
        return carry
    lax.fori_loop(0, rows // rc, body, 0)


def _inproj_kernel(x_ref, sh_ref, sc_ref, nw_ref, w_ref, wdt_ref, o_ref, dt_ref, h_scr):
    @pl.when(pl.program_id(1) == 0)
    def _():
        _modulate_rows(x_ref, nw_ref[...], sc_ref[0], sh_ref[0], h_scr, h_scr.shape[0])
        dt_ref[...] = _dot(h_scr[...], wdt_ref[...].astype(BF16))

    o_ref[...] = _dot(h_scr[...], w_ref[...]).astype(o_ref.dtype)


def _inproj(x2d, sh, sc, nw, w_cat, w_dt, *, n_cols, rows_per_mod, tm=1024, tn=INPROJ_TN):
    m = x2d.shape[0]
    tpb = rows_per_mod // tm
    return pl.pallas_call(
        _inproj_kernel,
        grid=(m // tm, n_cols // tn),
        in_specs=[pl.BlockSpec((tm, D_MODEL), lambda i, j: (i, 0)),
                  pl.BlockSpec((1, 1, D_MODEL), lambda i, j: (i // tpb, 0, 0)),
                  pl.BlockSpec((1, 1, D_MODEL), lambda i, j: (i // tpb, 0, 0)),
                  pl.BlockSpec((1, D_MODEL), lambda i, j: (0, 0)),
                  pl.BlockSpec((D_MODEL, tn), lambda i, j: (0, j)),
                  pl.BlockSpec((D_MODEL, LANES), lambda i, j: (0, 0))],
        out_specs=[pl.BlockSpec((tm, tn), lambda i, j: (i, j)),
                   pl.BlockSpec((tm, LANES), lambda i, j: (i, 0))],
        out_shape=[jax.ShapeDtypeStruct((m, n_cols), BF16),
                   jax.ShapeDtypeStruct((m, LANES), F32)],
        scratch_shapes=[pltpu.VMEM((tm, D_MODEL), BF16)],
        compiler_params=_params(("arbitrary", "arbitrary")),
        name="inproj",
    )(x2d, sh, sc, nw, w_cat, w_dt)


def _conv_silu_tile(ref, c, n_chunks, w, bias):
    length = n_chunks * CHUNK
    if isinstance(c, int):
        r0, p0, n0 = c * CHUNK, max(c * CHUNK - 16, 0), min(c * CHUNK + CHUNK, length - 16)
    else:
        r0 = pl.multiple_of(c * CHUNK, CHUNK)
        p0 = pl.multiple_of(jnp.maximum(c * CHUNK - 16, 0), 16)
        n0 = pl.multiple_of(jnp.minimum(c * CHUNK + CHUNK, length - 16), 16)
    x = ref[0, pl.ds(r0, CHUNK), :].astype(F32)
    prev_row = ref[0, pl.ds(p0, 16), :].astype(F32)[15:16, :]
    prev_row = jnp.where(c > 0, prev_row, 0.0)
    next_row = ref[0, pl.ds(n0, 16), :].astype(F32)[0:1, :]
    next_row = jnp.where(c < n_chunks - 1, next_row, 0.0)
    rows = lax.broadcasted_iota(jnp.int32, (8, x.shape[1]), 0)
    xm = pltpu.roll(x, 1, axis=0)
    xm = jnp.concatenate([jnp.where(rows == 0, prev_row, xm[0:8]), xm[8:]], axis=0)
    xp = pltpu.roll(x, CHUNK - 1, axis=0)
    xp = jnp.concatenate([xp[:CHUNK - 8], jnp.where(rows == 7, next_row, xp[CHUNK - 8:])], axis=0)
    hw = 0.5 * w
    half_y = xm * hw[0:1, :] + x * hw[1:2, :] + xp * hw[2:3, :] + 0.5 * bias
    return _silu_of_twice(half_y)


def _split_bf16(x, parts):
    out = []
    for _ in range(parts - 1):
        hi = x.astype(BF16).astype(F32)
        out.append(hi)
        x = x - hi
    out.append(x.astype(BF16).astype(F32))
    return out


TAB_ACS, TAB_ONE, TAB_EXP, TAB_W, TAB_DEC, TAB_USED = 0, 3, 6, 8, 10, 12
SEG_ROWS = 48


def _decay_tables(dt_raw, dt_bias, a_neg, tri_ref, a_scr, bsel_scr=None, dt_scr=None):
    n_chunks = dt_raw.shape[0]
    n = n_chunks * 8
    dt = _softplus(dt_raw + dt_bias)
    a = (dt * a_neg).reshape(n, CHUNK)
    dt = dt.reshape(n, CHUNK)
    hi, lo = _split_bf16(a, 2)
    lhs = jnp.concatenate([hi, lo], axis=0).astype(BF16)
    pre = _dot(lhs, tri_ref[0])
    suf = _dot(lhs, tri_ref[1])
    row = lax.broadcasted_iota(jnp.int32, (n, CHUNK), 0)
    fwd = jnp.bitwise_and(row, 7) < SSD_REP
    acs = jnp.where(fwd, pre[:n] + pre[n:], suf[:n] + suf[n:])
    tot = jnp.where(fwd[:, 0:1], acs[:, CHUNK - 1:CHUNK], acs[:, 0:1])
    ones = jnp.ones((n, CHUNK), F32)
    comps_a = (_split_bf16(acs, 3) + [ones, ones, ones] + _split_bf16(jnp.exp(acs), 2)
               + _split_bf16(jnp.exp(tot - acs) * dt, 2)
               + _split_bf16(jnp.broadcast_to(jnp.exp(tot), acs.shape), 2))
    comps_b = [ones, ones, ones] + [-p for p in _split_bf16(acs, 3)]
    row48 = jnp.bitwise_and(lax.broadcasted_iota(jnp.int32, (SEG_ROWS, CHUNK), 0), 7)
    pad = jnp.zeros((CHUNK - 8 * TAB_USED, CHUNK), F32)
    for c in range(n_chunks):
        sl = slice(c * 8, c * 8 + 8)
        a_scr[c] = jnp.concatenate([p[sl] for p in comps_a] + [pad], axis=0).T.astype(BF16)
        if bsel_scr is None:
            continue
        b_all = jnp.concatenate([p[sl] for p in comps_b], axis=0)
        for k in range(8):
            bsel_scr[c, :, k * CHUNK:(k + 1) * CHUNK] = jnp.where(row48 == k, b_all, 0.0).astype(BF16)
        dt_scr[c] = dt[sl]


def _split_heads(t):
    lane = lax.broadcasted_iota(jnp.int32, t.shape, 1)
    return (jnp.where(lane < SSD_HEAD_DIM, t, 0.0).astype(BF16),
            jnp.where(lane >= SSD_HEAD_DIM, t, 0.0).astype(BF16))


def _chunk_dir(xs, x_pairs, bt, cc, cb, tab, bsel, dt, sel, off, h_ref, causal):
    l_io = lax.broadcasted_iota(jnp.int32, (CHUNK, CHUNK), 0)
    s_io = lax.broadcasted_iota(jnp.int32, (CHUNK, CHUNK), 1)
    mask = (l_io >= s_io) if causal else (l_io <= s_io)
    seg = _dot(tab, jnp.concatenate([bsel, jnp.zeros((CHUNK - SEG_ROWS, bsel.shape[1]), BF16)],
                                    axis=0))
    spread = _dot(tab, sel)
    e_exp, w_exp, dec = spread[:, 0:GROUP_W], spread[:, GROUP_W:2 * GROUP_W], spread[0:1, 2 * GROUP_W:]
    h = h_ref[...]
    h_ref[...] = h * dec + _dot(bt, (xs * w_exp).astype(BF16))
    y = _dot(cc, h.astype(BF16)) * e_exp
    ys = []
    for pair in range(2):
        ms = []
        for r in (2 * pair, 2 * pair + 1):
            decay = jnp.exp(jnp.where(mask, seg[:, r * CHUNK:(r + 1) * CHUNK], -jnp.inf))
            ms.append((cb * decay * dt[off + r:off + r + 1, :]).astype(BF16))
        ys.append(_dot(jnp.concatenate(ms, axis=1), x_pairs[pair]))
    return y + jnp.concatenate(ys, axis=1)


def _ssd_kernel(xs_ref, b_ref, c_ref, z_ref, dtr_ref, cxs_ref, cbm_ref, cdtr_ref,
                cwx_ref, cwb_ref, cwc_ref, bx_ref, bb_ref, bc_ref,
                dtb_ref, alog_ref, dsk_ref, nw_ref, tri_ref, sel_ref,
                y_ref,
                xs_scr, xp_scr, bt_scr, cc_scr, cb_scr, y_scr, hf_scr, hb_scr,
                tab_scr, bsel_scr, dt_scr, ctab_scr):
    dt_bias = dtb_ref[0]
    a_neg = -jnp.exp(alog_ref[0])
    cwx, cwb, cwc = cwx_ref[...], cwb_ref[...], cwc_ref[...]
    bx, bb, bc = bx_ref[...], bb_ref[...], bc_ref[...]

    hf_scr[...] = jnp.zeros(hf_scr.shape, F32)
    hb_scr[...] = jnp.zeros(hb_scr.shape, F32)
    _decay_tables(cdtr_ref[0, 0], dt_bias, a_neg, tri_ref, ctab_scr)
    ctx_tiles = []
    for c in range(N_CTX_CHUNKS):
        xs = _conv_silu_tile(cxs_ref, c, N_CTX_CHUNKS, cwx, bx)
        bt = _conv_silu_tile(cbm_ref, c, N_CTX_CHUNKS, cwb, bb).T.astype(BF16)
        ctx_tiles.append((xs, bt))
    for d, h_ref, order in ((0, hf_scr, range(N_CTX_CHUNKS)), (1, hb_scr, reversed(range(N_CTX_CHUNKS)))):
        for c in order:
            xs, bt = ctx_tiles[c]
            spread = _dot(ctab_scr[c], sel_ref[d])
            w_exp, dec = spread[:, GROUP_W:2 * GROUP_W], spread[0:1, 2 * GROUP_W:]
            h_ref[...] = h_ref[...] * dec + _dot(bt, (xs * w_exp).astype(BF16))

    _decay_tables(dtr_ref[0, 0], dt_bias, a_neg, tri_ref, tab_scr, bsel_scr, dt_scr)

    def prepare(c):
        r0 = pl.multiple_of(c * CHUNK, CHUNK)
        xs = _conv_silu_tile(xs_ref, c, N_CHUNKS, cwx, bx)
        bt = _conv_silu_tile(b_ref, c, N_CHUNKS, cwb, bb).T.astype(BF16)
        cc = _conv_silu_tile(c_ref, c, N_CHUNKS, cwc, bc).astype(BF16)
        cb = _dot(cc, bt)
        x_pairs = [jnp.concatenate(_split_heads(xs[:, p * LANES:(p + 1) * LANES]), axis=0)
                   for p in range(2)]
        xs_scr[pl.ds(r0, CHUNK), :] = xs
        xp_scr[c, 0] = x_pairs[0]
        xp_scr[c, 1] = x_pairs[1]
        bt_scr[c] = bt
        cc_scr[c] = cc
        cb_scr[c] = cb
        return xs, x_pairs, bt, cc, cb

    def recall(c):
        r0 = pl.multiple_of(c * CHUNK, CHUNK)
        return (xs_scr[pl.ds(r0, CHUNK), :], [xp_scr[c, 0], xp_scr[c, 1]], bt_scr[c], cc_scr[c],
                cb_scr[c])

    def scan(c, tiles, causal):
        xs, x_pairs, bt, cc, cb = tiles
        d = 0 if causal else 1
        bsel = bsel_scr[c, :, d * SSD_REP * CHUNK:(d + 1) * SSD_REP * CHUNK]
        return _chunk_dir(xs, x_pairs, bt, cc, cb, tab_scr[c], bsel, dt_scr[c], sel_ref[d],
                          d * SSD_REP, hf_scr if causal else hb_scr, causal)

    d_sum = dsk_ref[0, 0:1, :] + dsk_ref[0, 1:2, :]
    nw = nw_ref[...]

    def finish(c, y_dir, xs):
        r0 = pl.multiple_of(c * CHUNK, CHUNK)
        y = y_scr[pl.ds(r0, CHUNK), :] + y_dir + xs * d_sum
        gated = y * _silu_of_twice(0.5 * z_ref[0, pl.ds(r0, CHUNK), :].astype(F32))
        ms = jnp.mean(gated * gated, axis=-1, keepdims=True)
        y_ref[0, pl.ds(r0, CHUNK), :] = (gated * lax.rsqrt(ms + EPS) * nw).astype(y_ref.dtype)

    def visits(i):
        fwd = [i * SCAN_UNROLL + u for u in range(SCAN_UNROLL)]
        return [(c, True) for c in fwd] + [(N_CHUNKS - 1 - c, False) for c in fwd]

    def first_half(i, carry):
        for c, causal in visits(i):
            r0 = pl.multiple_of(c * CHUNK, CHUNK)
            y_scr[pl.ds(r0, CHUNK), :] = scan(c, prepare(c), causal)
        return carry

    def second_half(i, carry):
        for c, causal in visits(i):
            tiles = recall(c)
            finish(c, scan(c, tiles, causal), tiles[0])
        return carry

    half = N_CHUNKS // 2 // SCAN_UNROLL
    lax.fori_loop(0, half, first_half, 0)
    lax.fori_loop(half, 2 * half, second_half, 0)


def _ssd(proj, dtr, cproj, cdtr, conv_w, conv_b, dtb, alog, dsk, norm_w, tri):
    gx = COL_X // GROUP_W
    gz = COL_Z // GROUP_W
    gb = COL_B // SSD_STATE
    gc = COL_C // SSD_STATE
    in_specs = [
        pl.BlockSpec((1, SEQ, GROUP_W), lambda b, g: (b, 0, gx + g)),
        pl.BlockSpec((1, SEQ, SSD_STATE), lambda b, g: (b, 0, gb + g)),
        pl.BlockSpec((1, SEQ, SSD_STATE), lambda b, g: (b, 0, gc + g)),
        pl.BlockSpec((1, SEQ, GROUP_W), lambda b, g: (b, 0, gz + g)),
        pl.BlockSpec((1, 1, N_CHUNKS, 8, CHUNK), lambda b, g: (g, b, 0, 0, 0)),
        pl.BlockSpec((1, CTX_LEN, GROUP_W), lambda b, g: (b, 0, gx + g)),
        pl.BlockSpec((1, CTX_LEN, SSD_STATE), lambda b, g: (b, 0, gb + g)),
        pl.BlockSpec((1, 1, N_CTX_CHUNKS, 8, CHUNK), lambda b, g: (g, b, 0, 0, 0)),
        pl.BlockSpec((3, GROUP_W), lambda b, g: (0, gx + g)),
        pl.BlockSpec((3, SSD_STATE), lambda b, g: (0, gb + g)),
        pl.BlockSpec((3, SSD_STATE), lambda b, g: (0, gc + g)),
        pl.BlockSpec((1, GROUP_W), lambda b, g: (0, gx + g)),
        pl.BlockSpec((1, SSD_STATE), lambda b, g: (0, gb + g)),
        pl.BlockSpec((1, SSD_STATE), lambda b, g: (0, gc + g)),
        pl.BlockSpec((1, 8, 1), lambda b, g: (g, 0, 0)),
        pl.BlockSpec((1, 8, 1), lambda b, g: (g, 0, 0)),
        pl.BlockSpec((1, 2, GROUP_W), lambda b, g: (g, 0, 0)),
        pl.BlockSpec((1, GROUP_W), lambda b, g: (0, g)),
        pl.BlockSpec((2, CHUNK, CHUNK), lambda b, g: (0, 0, 0)),
        pl.BlockSpec((2, CHUNK, 3 * GROUP_W), lambda b, g: (0, 0, 0)),
    ]
    return pl.pallas_call(
        _ssd_kernel,
        grid=(BATCH, SSD_GROUPS),
        in_specs=in_specs,
        out_specs=pl.BlockSpec((1, SEQ, GROUP_W), lambda b, g: (b, 0, g)),
        out_shape=jax.ShapeDtypeStruct((BATCH, SEQ, D_MODEL), BF16),
        scratch_shapes=[pltpu.VMEM((SEQ, GROUP_W), F32),
                        pltpu.VMEM((N_CHUNKS, 2, 2 * CHUNK, LANES), BF16),
                        pltpu.VMEM((N_CHUNKS, CHUNK, CHUNK), BF16),
                        pltpu.VMEM((N_CHUNKS, CHUNK, CHUNK), BF16),
                        pltpu.VMEM((N_CHUNKS, CHUNK, CHUNK), F32),
                        pltpu.VMEM((SEQ, GROUP_W), F32),
                        pltpu.VMEM((SSD_STATE, GROUP_W), F32),
                        pltpu.VMEM((SSD_STATE, GROUP_W), F32),
                        pltpu.VMEM((N_CHUNKS, CHUNK, CHUNK), BF16),
                        pltpu.VMEM((N_CHUNKS, SEG_ROWS, 8 * CHUNK), BF16),
                        pltpu.VMEM((N_CHUNKS, 8, CHUNK), F32),
                        pltpu.VMEM((N_CTX_CHUNKS, CHUNK, CHUNK), BF16)],
        compiler_params=_params(("arbitrary", "arbitrary")),
        name="ssd",
    )(proj, proj, proj, proj, dtr, cproj, cproj, cdtr,
      conv_w, conv_w, conv_w, conv_b, conv_b, conv_b, dtb, alog, dsk, norm_w, tri, _spread_matrix())


def _spread_matrix():
    sel = np.zeros((2, CHUNK, 3 * GROUP_W), np.float32)
    for d in range(2):
        for grp, comp in enumerate((TAB_EXP, TAB_W, TAB_DEC)):
            for part in range(2):
                for h in range(SSD_REP):
                    lanes = slice(grp * GROUP_W + h * SSD_HEAD_DIM,
                                  grp * GROUP_W + (h + 1) * SSD_HEAD_DIM)
                    sel[d, (comp + part) * 8 + d * SSD_REP + h, lanes] = 1.0
    return jnp.asarray(sel, BF16)


def _pool_kernel(v_ref, win_ref, inv_ref, pw_ref, ps_ref, o_ref, p_scr):
    slab = win_ref.shape[1]
    for r in range(0, v_ref.shape[0], slab):
        v = v_ref[r:r + slab, :]
        p_scr[r:r + slab, :] = (_dot(win_ref[0], v) * inv_ref[0] - v.astype(F32)).astype(BF16)
    for r in range(0, v_ref.shape[0], DOT_ROWS):
        o_ref[r:r + DOT_ROWS, :] = (_dot(p_scr[r:r + DOT_ROWS, :], pw_ref[0])
                                    * ps_ref[...]).astype(o_ref.dtype)


def _pool_constants(tm):
    t = np.arange(tm)
    col = t % GRID_W
    wins, invs = [], []
    for w in POOL_WINDOWS:
        start = np.clip(col - w // 2, 0, GRID_W)
        end = np.clip(col + w - w // 2, 0, GRID_W)
        same_row = (t[:, None] // GRID_W) == (t[None, :] // GRID_W)
        inside = (col[None, :] >= start[:, None]) & (col[None, :] < end[:, None])
        wins.append((same_row & inside).astype(np.float32))
        invs.append((1.0 / (end - start).astype(np.float32))[:, None])
    return jnp.asarray(np.stack(wins), BF16), jnp.asarray(np.stack(invs), F32)


def _pool(proj2d, pool_w, pool_scale, tm=2048, slab=128):
    win, inv = _pool_constants(slab)
    gv = COL_V // POOL_GW
    return pl.pallas_call(
        _pool_kernel,
        grid=(TOKENS // tm, len(POOL_WINDOWS)),
        in_specs=[pl.BlockSpec((tm, POOL_GW), lambda i, k: (i, gv + k)),
                  pl.BlockSpec((1, slab, slab), lambda i, k: (k, 0, 0)),
                  pl.BlockSpec((1, slab, 1), lambda i, k: (k, 0, 0)),
                  pl.BlockSpec((1, POOL_GW, POOL_GW), lambda i, k: (k, 0, 0)),
                  pl.BlockSpec((1, POOL_GW), lambda i, k: (0, k))],
        out_specs=pl.BlockSpec((tm, POOL_GW), lambda i, k: (i, k)),
        out_shape=jax.ShapeDtypeStruct((TOKENS, D_MODEL), BF16),
        scratch_shapes=[pltpu.VMEM((tm, POOL_GW), BF16)],
        compiler_params=_params(("arbitrary", "arbitrary")),
        name="pool",
    )(proj2d, win, inv, pool_w, pool_scale)


def _merge_kernel(ys_ref, yp_ref, ws_ref, wp_ref, gs_ref, gp_ref, o_ref):
    a = jax.nn.sigmoid(gs_ref[...].astype(F32)) * _dot(ys_ref[...], ws_ref[...])
    b = jax.nn.sigmoid(gp_ref[...].astype(F32)) * _dot(yp_ref[...], wp_ref[...])
    o_ref[...] = (a + b).astype(o_ref.dtype)


def _merge(y_ssd, y_pool, w_s, w_p, proj2d, tm=512, tn=MERGE_TN):
    ggs = COL_GS // tn
    ggp = COL_GP // tn
    resident = pl.Buffered(1) if tn == D_MODEL else None
    return pl.pallas_call(
        _merge_kernel,
        grid=(TOKENS // tm, D_MODEL // tn),
        in_specs=[pl.BlockSpec((tm, D_MODEL), lambda i, j: (i, 0)),
                  pl.BlockSpec((tm, D_MODEL), lambda i, j: (i, 0)),
                  pl.BlockSpec((D_MODEL, tn), lambda i, j: (0, j), pipeline_mode=resident),
                  pl.BlockSpec((D_MODEL, tn), lambda i, j: (0, j), pipeline_mode=resident),
                  pl.BlockSpec((tm, tn), lambda i, j: (i, ggs + j)),
                  pl.BlockSpec((tm, tn), lambda i, j: (i, ggp + j))],
        out_specs=pl.BlockSpec((tm, tn), lambda i, j: (i, j)),
        out_shape=jax.ShapeDtypeStruct((TOKENS, D_MODEL), BF16),
        compiler_params=_params(("arbitrary", "arbitrary")),
        name="merge",
    )(y_ssd, y_pool, w_s, w_p, proj2d, proj2d)


def _outproj_kernel(m_ref, wo_ref, x_ref, g1_ref, sh_ref, sc_ref, nw_ref, x1_ref, h2_ref):
    x1_ref[...] = x_ref[...] + g1_ref[0] * _dot(m_ref[...], wo_ref[...])
    _modulate_rows(x1_ref, nw_ref[...], sc_ref[0], sh_ref[0], h2_ref, x1_ref.shape[0], rc=128)


def _outproj(merged, w_o, x2d, g1, sh2, sc2, nw2, tm=512):
    tpb = SEQ // tm
    mod_spec = pl.BlockSpec((1, 1, D_MODEL), lambda i: (i // tpb, 0, 0))
    return pl.pallas_call(
        _outproj_kernel,
        grid=(TOKENS // tm,),
        in_specs=[pl.BlockSpec((tm, D_MODEL), lambda i: (i, 0)),
                  pl.BlockSpec((D_MODEL, D_MODEL), lambda i: (0, 0), pipeline_mode=pl.Buffered(1)),
                  pl.BlockSpec((tm, D_MODEL), lambda i: (i, 0)),
                  mod_spec, mod_spec, mod_spec,
                  pl.BlockSpec((1, D_MODEL), lambda i: (0, 0))],
        out_specs=[pl.BlockSpec((tm, D_MODEL), lambda i: (i, 0)),
                   pl.BlockSpec((tm, D_MODEL), lambda i: (i, 0))],
        out_shape=[jax.ShapeDtypeStruct((TOKENS, D_MODEL), F32),
                   jax.ShapeDtypeStruct((TOKENS, D_MODEL), BF16)],
        compiler_params=_params(("arbitrary",)),
        name="outproj",
    )(merged, w_o, x2d, g1, sh2, sc2, nw2)


def _glu_kernel(h_ref, hp_ref, hn_ref, wa_ref, wg_ref, fw_ref, fb_ref, u_ref,
                hext_scr, a_even, g_even, a_odd, g_odd, *, n_chunks, n_steps):
    s = pl.program_id(0)
    tm = h_ref.shape[0]
    cur = jnp.minimum(s, n_steps - 2)
    i = cur // n_chunks

    @pl.when(s == 0)
    def _():
        a_odd[...] = jnp.zeros(a_odd.shape, F32)
        g_odd[...] = jnp.zeros(g_odd.shape, F32)

    @pl.when(cur % n_chunks == 0)
    def _():
        has_prev = ((i * tm) % SEQ) != 0
        has_next = (((i + 1) * tm) % SEQ) != 0
        hext_scr[0:GRID_W, :] = jnp.where(has_prev, hp_ref[...], jnp.zeros_like(hp_ref[...]))
        hext_scr[GRID_W:GRID_W + tm, :] = h_ref[...]
        hext_scr[GRID_W + tm:, :] = jnp.where(has_next, hn_ref[...], jnp.zeros_like(hn_ref[...]))

    def step(wr_a, wr_g, rd_a, rd_g):
        fw = 0.5 * fw_ref[...]
        fb = 0.5 * fb_ref[...]
        for r in range(0, tm, GLU_ROWS):
            hv = (rd_g[r:r + GLU_ROWS, :] * fw[0:1, :] + fb
                  + rd_g[r + GRID_W:r + GRID_W + GLU_ROWS, :] * fw[1:2, :]
                  + rd_g[r + 2 * GRID_W:r + 2 * GRID_W + GLU_ROWS, :] * fw[2:3, :])
            gelu = hv * (1.0 + lax.erf(hv * np.float32(np.sqrt(2.0))))
            u_ref[r:r + GLU_ROWS, :] = (gelu * rd_a[r:r + GLU_ROWS, :]).astype(u_ref.dtype)
        for r in range(0, tm, DOT_ROWS):
            wr_a[r:r + DOT_ROWS, :] = _dot(hext_scr[GRID_W + r:GRID_W + r + DOT_ROWS, :], wa_ref[...])
        for r in range(0, tm + 2 * GRID_W, DOT_ROWS):
            n = min(DOT_ROWS, tm + 2 * GRID_W - r)
            wr_g[r:r + n, :] = _dot(hext_scr[r:r + n, :], wg_ref[...])

    @pl.when(s % 2 == 0)
    def _():
        step(a_even, g_even, a_odd, g_odd)

    @pl.when(s % 2 == 1)
    def _():
        step(a_odd, g_odd, a_even, g_even)


def _glu(h2, w_up, ffn_conv_w, ffn_conv_b, tm=2048, tn=GLU_TN):
    n_chunks = D_FF // tn
    n_tiles = TOKENS // tm
    n_steps = n_tiles * n_chunks + 1
    hb = tm // GRID_W
    n_halo = TOKENS // GRID_W

    def cur(s):
        return jnp.minimum(s, n_steps - 2)

    def prev(s):
        return jnp.maximum(s - 1, 0)

    return pl.pallas_call(
        functools.partial(_glu_kernel, n_chunks=n_chunks, n_steps=n_steps),
        grid=(n_steps,),
        in_specs=[pl.BlockSpec((tm, D_MODEL), lambda s: (cur(s) // n_chunks, 0),
                               pipeline_mode=pl.Buffered(1)),
                  pl.BlockSpec((GRID_W, D_MODEL),
                               lambda s: (jnp.maximum((cur(s) // n_chunks) * hb - 1, 0), 0)),
                  pl.BlockSpec((GRID_W, D_MODEL),
                               lambda s: (jnp.minimum((cur(s) // n_chunks + 1) * hb, n_halo - 1), 0)),
                  pl.BlockSpec((D_MODEL, tn), lambda s: (0, cur(s) % n_chunks)),
                  pl.BlockSpec((D_MODEL, tn), lambda s: (0, n_chunks + cur(s) % n_chunks)),
                  pl.BlockSpec((3, tn), lambda s: (0, prev(s) % n_chunks)),
                  pl.BlockSpec((1, tn), lambda s: (0, prev(s) % n_chunks))],
        out_specs=pl.BlockSpec((tm, tn), lambda s: (prev(s) // n_chunks, prev(s) % n_chunks)),
        out_shape=jax.ShapeDtypeStruct((TOKENS, D_FF), BF16),
        scratch_shapes=[pltpu.VMEM((tm + 2 * GRID_W, D_MODEL), BF16),
                        pltpu.VMEM((tm, tn), F32), pltpu.VMEM((tm + 2 * GRID_W, tn), F32),
                        pltpu.VMEM((tm, tn), F32), pltpu.VMEM((tm + 2 * GRID_W, tn), F32)],
        compiler_params=_params(("arbitrary",)),
        name="glu",
    )(h2, h2, h2, w_up, w_up, ffn_conv_w, ffn_conv_b)


def _down_kernel(u_ref, wd_ref, x1_ref, g2_ref, nw_ref, o_ref):
    x2 = x1_ref[...] + g2_ref[0] * _dot(u_ref[...], wd_ref[...])
    ms = jnp.mean(x2 * x2, axis=-1, keepdims=True)
    o_ref[...] = x2 * lax.rsqrt(ms + EPS) * nw_ref[...]


def _down(u, w_down, x1, g2, fnw, tm=512):
    tpb = SEQ // tm
    return pl.pallas_call(
        _down_kernel,
        grid=(TOKENS // tm,),
        in_specs=[pl.BlockSpec((tm, D_FF), lambda i: (i, 0)),
                  pl.BlockSpec((D_FF, D_MODEL), lambda i: (0, 0), pipeline_mode=pl.Buffered(1)),
                  pl.BlockSpec((tm, D_MODEL), lambda i: (i, 0)),
                  pl.BlockSpec((1, 1, D_MODEL), lambda i: (i // tpb, 0, 0)),
                  pl.BlockSpec((1, D_MODEL), lambda i: (0, 0))],
        out_specs=pl.BlockSpec((tm, D_MODEL), lambda i: (i, 0)),
        out_shape=jax.ShapeDtypeStruct((TOKENS, D_MODEL), F32),
        compiler_params=_params(("arbitrary",)),
        name="down",
    )(u, w_down, x1, g2, fnw)


def _dt_rows(dt_raw, batch, n_chunks):
    d = dt_raw[:, :2 * SSD_HEADS].reshape(batch, n_chunks, CHUNK, 2, SSD_GROUPS, SSD_REP)
    d = jnp.transpose(d, (4, 0, 1, 3, 5, 2))
    return d.reshape(SSD_GROUPS, batch, n_chunks, 2 * SSD_REP, CHUNK)


def _group_rows(p):
    p = p.reshape(2, SSD_GROUPS, SSD_REP)
    return jnp.transpose(p, (1, 0, 2)).reshape(SSD_GROUPS, 2 * SSD_REP, 1)


def kernel(x, c, ctx, c_ctx, w_ada, b_ada, norm1_w, w_in, ssd_conv_w, ssd_conv_b, dt_bias, a_log,
           d_skip, ssd_norm_w, w_ssd_out, pool_w, pool_scale, w_pool_out, w_o, norm2_w, w_up,
           ffn_conv_w, ffn_conv_b, w_down, final_norm_w):
    assert w_in.shape[0] == 1, "single-layer block only"
    d = D_MODEL
    x2d = x.reshape(TOKENS, d)
    ctx2d = ctx.reshape(BATCH * CTX_LEN, d)

    cs = jnp.zeros((8, d), F32).at[:BATCH].set(c).at[BATCH].set(c_ctx)
    mod = _ada(cs, w_ada[0], b_ada)
    sh1, sc1, g1, sh2, sc2, g2 = [mod[:BATCH, k * d:(k + 1) * d].reshape(BATCH, 1, d)
                                  for k in range(6)]
    csh1 = mod[BATCH:BATCH + 1, 0:d].reshape(1, 1, d)
    csc1 = mod[BATCH:BATCH + 1, d:2 * d].reshape(1, 1, d)

    w_in0 = w_in[0]
    n_dt = 2 * SSD_HEADS
    w_cat, w_dt = _drop_leading_cols(jnp.transpose(w_in0), n_dt)
    w_s = w_ssd_out[0].astype(BF16)
    w_p = w_pool_out[0].astype(BF16)
    w_ob = w_o[0].astype(BF16)
    w_upb = w_up[0].astype(BF16)
    w_db = w_down[0].astype(BF16)
    pool_wb = pool_w[0].astype(BF16)

    proj, dt_raw = _inproj(x2d, sh1, sc1, norm1_w, w_cat, w_dt, n_cols=N_PROJ, rows_per_mod=SEQ)
    cproj, cdt_raw = _inproj(ctx2d, csh1, csc1, norm1_w, w_cat, w_dt, n_cols=N_CTX_PROJ,
                             rows_per_mod=BATCH * CTX_LEN)

    dtr = _dt_rows(dt_raw, BATCH, N_CHUNKS)
    cdtr = _dt_rows(cdt_raw, BATCH, N_CTX_CHUNKS)
    dsk = jnp.repeat(jnp.transpose(d_skip[0].reshape(2, SSD_GROUPS, SSD_REP), (1, 0, 2)),
                     SSD_HEAD_DIM, axis=-1)
    idx = np.arange(CHUNK)
    tri = jnp.asarray(np.stack([idx[:, None] <= idx[None, :], idx[:, None] >= idx[None, :]])
                      .astype(np.float32), BF16)

    y_ssd = _ssd(proj.reshape(BATCH, SEQ, N_PROJ), dtr,
                 cproj.reshape(BATCH, CTX_LEN, N_CTX_PROJ), cdtr,
                 ssd_conv_w[0], ssd_conv_b, _group_rows(dt_bias[0]), _group_rows(a_log[0]),
                 dsk, ssd_norm_w, tri)
    y_pool = _pool(proj, pool_wb, pool_scale)
    merged = _merge(y_ssd.reshape(TOKENS, d), y_pool, w_s, w_p, proj)
    x1, h2 = _outproj(merged, w_ob, x2d, g1, sh2, sc2, norm2_w)
    u = _glu(h2, w_upb, ffn_conv_w[0], ffn_conv_b)
    out = _down(u, w_db, x1, g2, final_norm_w.reshape(1, d))
    return out.reshape(BATCH, SEQ, d)
```

```python
import functools

import numpy as np
import jax
import jax.numpy as jnp
from jax import lax
from jax.experimental import pallas as pl
from jax.experimental.pallas import tpu as pltpu

F32 = jnp.float32
BF16 = jnp.bfloat16

D_MODEL = 2048
BATCH = 4
SEQ = 4096
TOKENS = BATCH * SEQ
GRID_W = 64
CTX_LEN = 256

SSD_HEAD_DIM = 64
SSD_HEADS = 32
SSD_GROUPS = 8
SSD_REP = 4
SSD_STATE = 128
CHUNK = 128
GROUP_W = SSD_REP * SSD_HEAD_DIM
N_CHUNKS = SEQ // CHUNK
N_CTX_CHUNKS = CTX_LEN // CHUNK
SCAN_UNROLL = 16
GLU_ROWS = 16
DOT_ROWS = 1024
INPROJ_TN, GLU_TN = 1536, 512

POOL_WINDOWS = (2, 4, 8, 16)
POOL_GW = 512
D_FF = 5632
EPS = 1e-6

COL_X = 0
COL_B = 2048
COL_C = 3072
COL_Z = 4096
COL_V = 6144
COL_GS = 8192
COL_GP = 10240
N_PROJ = 12288
N_CTX_PROJ = 3072

LANES = 128
VMEM_LIMIT = 56 * 1024 * 1024


def _dot(a, b):
    return jnp.dot(a, b, preferred_element_type=F32)


def _silu(x):
    return x * jax.nn.sigmoid(x)


def _silu_of_twice(h):
    return h + h * jnp.tanh(h)


def _softplus(x):
    return jnp.maximum(x, 0.0) + jnp.log1p(jnp.exp(-jnp.abs(x)))


def _params(sem):
    return pltpu.CompilerParams(dimension_semantics=sem, vmem_limit_bytes=VMEM_LIMIT)


def _shift_cast_kernel(a_ref, b_ref, head_ref, o_ref, head_o_ref, *, shift):
    tn = o_ref.shape[1]

    @pl.when(pl.program_id(0) == 0)
    def _():
        head_o_ref[...] = head_ref[...].T

    x = jnp.concatenate([a_ref[shift:, :], b_ref[...]], axis=0)
    for r in range(0, tn, LANES):
        o_ref[:, r:r + LANES] = x[r:r + LANES, :].T.astype(o_ref.dtype)


def _drop_leading_cols(w_t, shift, tn=1024):
    n, k = w_t.shape
    n_out = n - shift
    return pl.pallas_call(
        functools.partial(_shift_cast_kernel, shift=shift),
        grid=(n_out // tn,),
        in_specs=[pl.BlockSpec((tn, k), lambda j: (j, 0)),
                  pl.BlockSpec((shift, k), lambda j: ((j + 1) * (tn // shift), 0)),
                  pl.BlockSpec((LANES, k), lambda j: (0, 0))],
        out_specs=[pl.BlockSpec((k, tn), lambda j: (0, j)),
                   pl.BlockSpec((k, LANES), lambda j: (0, 0))],
        out_shape=[jax.ShapeDtypeStruct((k, n_out), BF16),
                   jax.ShapeDtypeStruct((k, LANES), F32)],
        compiler_params=_params(("arbitrary",)),
        name="wprep",
    )(w_t, w_t, w_t)


def _ada_kernel(c_ref, w_ref, b_ref, o_ref):
    s_hi, s_lo = _split_bf16(_silu(c_ref[...]), 2)
    lhs = jnp.concatenate([s_hi, s_lo], axis=0).astype(BF16)
    w = w_ref[...]
    w_hi = w.astype(BF16)
    w_lo = (w - w_hi.astype(F32)).astype(BF16)
    r = _dot(lhs, w_hi) + _dot(lhs, w_lo)
    o_ref[...] = r[0:8] + r[8:16] + b_ref[...]


def _ada(cs, w, b):
    tn = 1024
    n = w.shape[1]
    return pl.pallas_call(
        _ada_kernel,
        grid=(n // tn,),
        in_specs=[pl.BlockSpec((8, D_MODEL), lambda j: (0, 0)),
                  pl.BlockSpec((D_MODEL, tn), lambda j: (0, j)),
                  pl.BlockSpec((1, tn), lambda j: (0, j))],
        out_specs=pl.BlockSpec((8, tn), lambda j: (0, j)),
        out_shape=jax.ShapeDtypeStruct((8, n), F32),
        compiler_params=_params(("arbitrary",)),
        name="ada",
    )(cs, w, b)


def _modulate_rows(x_ref, nw, sc, sh, out_ref, rows, rc=256):
    gain = nw * (1.0 + sc)

    def body(k, carry):
        r = pl.multiple_of(k * rc, rc)
        x = x_ref[pl.ds(r, rc), :]
        inv = lax.rsqrt(jnp.mean(x * x, axis=-1, keepdims=True) + EPS)
        out_ref[pl.ds(r, rc), :] = (x_ref[pl.ds(r, rc), :] * inv * gain + sh).astype(out_ref.dtype)
        return carry
    lax.fori_loop(0, rows // rc, body, 0)


def _inproj_kernel(x_ref, sh_ref, sc_ref, nw_ref, w_ref, wdt_ref, o_ref, dt_ref, h_scr):
    @pl.when(pl.program_id(1) == 0)
    def _():
        _modulate_rows(x_ref, nw_ref[...], sc_ref[0], sh_ref[0], h_scr, h_scr.shape[0])
        dt_ref[...] = _dot(h_scr[...], wdt_ref[...].astype(BF16))

    o_ref[...] = _dot(h_scr[...], w_ref[...]).astype(o_ref.dtype)


def _inproj(x2d, sh, sc, nw, w_cat, w_dt, *, n_cols, rows_per_mod, tm=1024, tn=INPROJ_TN):
    m = x2d.shape[0]
    tpb = rows_per_mod // tm
    return pl.pallas_call(
        _inproj_kernel,
        grid=(m // tm, n_cols // tn),
        in_specs=[pl.BlockSpec((tm, D_MODEL), lambda i, j: (i, 0)),
                  pl.BlockSpec((1, 1, D_MODEL), lambda i, j: (i // tpb, 0, 0)),
                  pl.BlockSpec((1, 1, D_MODEL), lambda i, j: (i // tpb, 0, 0)),
                  pl.BlockSpec((1, D_MODEL), lambda i, j: (0, 0)),
                  pl.BlockSpec((D_MODEL, tn), lambda i, j: (0, j)),
                  pl.BlockSpec((D_MODEL, LANES), lambda i, j: (0, 0))],
        out_specs=[pl.BlockSpec((tm, tn), lambda i, j: (i, j)),
                   pl.BlockSpec((tm, LANES), lambda i, j: (i, 0))],
        out_shape=[jax.ShapeDtypeStruct((m, n_cols), BF16),
                   jax.ShapeDtypeStruct((m, LANES), F32)],
        scratch_shapes=[pltpu.VMEM((tm, D_MODEL), BF16)],
        compiler_params=_params(("arbitrary", "arbitrary")),
        name="inproj",
    )(x2d, sh, sc, nw, w_cat, w_dt)


def _conv_silu_tile(ref, c, n_chunks, w, bias):
    length = n_chunks * CHUNK
    if isinstance(c, int):
        r0, p0, n0 = c * CHUNK, max(c * CHUNK - 16, 0), min(c * CHUNK + CHUNK, length - 16)
    else:
        r0 = pl.multiple_of(c * CHUNK, CHUNK)
        p0 = pl.multiple_of(jnp.maximum(c * CHUNK - 16, 0), 16)
        n0 = pl.multiple_of(jnp.minimum(c * CHUNK + CHUNK, length - 16), 16)
    x = ref[0, pl.ds(r0, CHUNK), :].astype(F32)
    prev_row = ref[0, pl.ds(p0, 16), :].astype(F32)[15:16, :]
    prev_row = jnp.where(c > 0, prev_row, 0.0)
    next_row = ref[0, pl.ds(n0, 16), :].astype(F32)[0:1, :]
    next_row = jnp.where(c < n_chunks - 1, next_row, 0.0)
    rows = lax.broadcasted_iota(jnp.int32, (8, x.shape[1]), 0)
    xm = pltpu.roll(x, 1, axis=0)
    xm = jnp.concatenate([jnp.where(rows == 0, prev_row, xm[0:8]), xm[8:]], axis=0)
    xp = pltpu.roll(x, CHUNK - 1, axis=0)
    xp = jnp.concatenate([xp[:CHUNK - 8], jnp.where(rows == 7, next_row, xp[CHUNK - 8:])], axis=0)
    hw = 0.5 * w
    half_y = xm * hw[0:1, :] + x * hw[1:2, :] + xp * hw[2:3, :] + 0.5 * bias
    return _silu_of_twice(half_y)


def _split_bf16(x, parts):
    out = []
    for _ in range(parts - 1):
        hi = x.astype(BF16).astype(F32)
        out.append(hi)
        x = x - hi
    out.append(x.astype(BF16).astype(F32))
    return out


TAB_ACS, TAB_ONE, TAB_EXP, TAB_W, TAB_DEC, TAB_USED = 0, 3, 6, 8, 10, 12
SEG_ROWS = 48


def _decay_tables(dt_raw, dt_bias, a_neg, tri_ref, a_scr, bsel_scr=None, dt_scr=None):
    n_chunks = dt_raw.shape[0]
    n = n_chunks * 8
    dt = _softplus(dt_raw + dt_bias)
    a = (dt * a_neg).reshape(n, CHUNK)
    dt = dt.reshape(n, CHUNK)
    hi, lo = _split_bf16(a, 2)
    lhs = jnp.concatenate([hi, lo], axis=0).astype(BF16)
    pre = _dot(lhs, tri_ref[0])
    suf = _dot(lhs, tri_ref[1])
    row = lax.broadcasted_iota(jnp.int32, (n, CHUNK), 0)
    fwd = jnp.bitwise_and(row, 7) < SSD_REP
    acs = jnp.where(fwd, pre[:n] + pre[n:], suf[:n] + suf[n:])
    tot = jnp.where(fwd[:, 0:1], acs[:, CHUNK - 1:CHUNK], acs[:, 0:1])
    ones = jnp.ones((n, CHUNK), F32)
    comps_a = (_split_bf16(acs, 3) + [ones, ones, ones] + _split_bf16(jnp.exp(acs), 2)
               + _split_bf16(jnp.exp(tot - acs) * dt, 2)
               + _split_bf16(jnp.broadcast_to(jnp.exp(tot), acs.shape), 2))
    comps_b = [ones, ones, ones] + [-p for p in _split_bf16(acs, 3)]
    row48 = jnp.bitwise_and(lax.broadcasted_iota(jnp.int32, (SEG_ROWS, CHUNK), 0), 7)
    pad = jnp.zeros((CHUNK - 8 * TAB_USED, CHUNK), F32)
    for c in range(n_chunks):
        sl = slice(c * 8, c * 8 + 8)
        a_scr[c] = jnp.concatenate([p[sl] for p in comps_a] + [pad], axis=0).T.astype(BF16)
        if bsel_scr is None:
            continue
        b_all = jnp.concatenate([p[sl] for p in comps_b], axis=0)
        for k in range(8):
            bsel_scr[c, :, k * CHUNK:(k + 1) * CHUNK] = jnp.where(row48 == k, b_all, 0.0).astype(BF16)
        dt_scr[c] = dt[sl]


def _split_heads(t):
    lane = lax.broadcasted_iota(jnp.int32, t.shape, 1)
    return (jnp.where(lane < SSD_HEAD_DIM, t, 0.0).astype(BF16),
            jnp.where(lane >= SSD_HEAD_DIM, t, 0.0).astype(BF16))


def _chunk_dir(xs, x_pairs, bt, cc, cb, tab, bsel, dt, sel, off, h_ref, causal):
    l_io = lax.broadcasted_iota(jnp.int32, (CHUNK, CHUNK), 0)
    s_io = lax.broadcasted_iota(jnp.int32, (CHUNK, CHUNK), 1)
    mask = (l_io >= s_io) if causal else (l_io <= s_io)
    seg = _dot(tab, jnp.concatenate([bsel, jnp.zeros((CHUNK - SEG_ROWS, bsel.shape[1]), BF16)],
                                    axis=0))
    spread = _dot(tab, sel)
    e_exp, w_exp, dec = spread[:, 0:GROUP_W], spread[:, GROUP_W:2 * GROUP_W], spread[0:1, 2 * GROUP_W:]
    h = h_ref[...]
    h_ref[...] = h * dec + _dot(bt, (xs * w_exp).astype(BF16))
    y = _dot(cc, h.astype(BF16)) * e_exp
    ys = []
    for pair in range(2):
        ms = []
        for r in (2 * pair, 2 * pair + 1):
            decay = jnp.exp(jnp.where(mask, seg[:, r * CHUNK:(r + 1) * CHUNK], -jnp.inf))
            ms.append((cb * decay * dt[off + r:off + r + 1, :]).astype(BF16))
        ys.append(_dot(jnp.concatenate(ms, axis=1), x_pairs[pair]))
    return y + jnp.concatenate(ys, axis=1)


def _ssd_kernel(xs_ref, b_ref, c_ref, z_ref, dtr_ref, cxs_ref, cbm_ref, cdtr_ref,
                cwx_ref, cwb_ref, cwc_ref, bx_ref, bb_ref, bc_ref,
                dtb_ref, alog_ref, dsk_ref, nw_ref, tri_ref, sel_ref,
                y_ref,
                xs_scr, xp_scr, bt_scr, cc_scr, cb_scr, y_scr, hf_scr, hb_scr,
                tab_scr, bsel_scr, dt_scr, ctab_scr):
    dt_bias = dtb_ref[0]
    a_neg = -jnp.exp(alog_ref[0])
    cwx, cwb, cwc = cwx_ref[...], cwb_ref[...], cwc_ref[...]
    bx, bb, bc = bx_ref[...], bb_ref[...], bc_ref[...]

    hf_scr[...] = jnp.zeros(hf_scr.shape, F32)
    hb_scr[...] = jnp.zeros(hb_scr.shape, F32)
    _decay_tables(cdtr_ref[0, 0], dt_bias, a_neg, tri_ref, ctab_scr)
    ctx_tiles = []
    for c in range(N_CTX_CHUNKS):
        xs = _conv_silu_tile(cxs_ref, c, N_CTX_CHUNKS, cwx, bx)
        bt = _conv_silu_tile(cbm_ref, c, N_CTX_CHUNKS, cwb, bb).T.astype(BF16)
        ctx_tiles.append((xs, bt))
    for d, h_ref, order in ((0, hf_scr, range(N_CTX_CHUNKS)), (1, hb_scr, reversed(range(N_CTX_CHUNKS)))):
        for c in order:
            xs, bt = ctx_tiles[c]
            spread = _dot(ctab_scr[c], sel_ref[d])
            w_exp, dec = spread[:, GROUP_W:2 * GROUP_W], spread[0:1, 2 * GROUP_W:]
            h_ref[...] = h_ref[...] * dec + _dot(bt, (xs * w_exp).astype(BF16))

    _decay_tables(dtr_ref[0, 0], dt_bias, a_neg, tri_ref, tab_scr, bsel_scr, dt_scr)

    def prepare(c):
        r0 = pl.multiple_of(c * CHUNK, CHUNK)
        xs = _conv_silu_tile(xs_ref, c, N_CHUNKS, cwx, bx)
        bt = _conv_silu_tile(b_ref, c, N_CHUNKS, cwb, bb).T.astype(BF16)
        cc = _conv_silu_tile(c_ref, c, N_CHUNKS, cwc, bc).astype(BF16)
        cb = _dot(cc, bt)
        x_pairs = [jnp.concatenate(_split_heads(xs[:, p * LANES:(p + 1) * LANES]), axis=0)
                   for p in range(2)]
        xs_scr[pl.ds(r0, CHUNK), :] = xs
        xp_scr[c, 0] = x_pairs[0]
        xp_scr[c, 1] = x_pairs[1]
        bt_scr[c] = bt
        cc_scr[c] = cc
        cb_scr[c] = cb
        return xs, x_pairs, bt, cc, cb

    def recall(c):
        r0 = pl.multiple_of(c * CHUNK, CHUNK)
        return (xs_scr[pl.ds(r0, CHUNK), :], [xp_scr[c, 0], xp_scr[c, 1]], bt_scr[c], cc_scr[c],
                cb_scr[c])

    def scan(c, tiles, causal):
        xs, x_pairs, bt, cc, cb = tiles
        d = 0 if causal else 1
        bsel = bsel_scr[c, :, d * SSD_REP * CHUNK:(d + 1) * SSD_REP * CHUNK]
        return _chunk_dir(xs, x_pairs, bt, cc, cb, tab_scr[c], bsel, dt_scr[c], sel_ref[d],
                          d * SSD_REP, hf_scr if causal else hb_scr, causal)

    d_sum = dsk_ref[0, 0:1, :] + dsk_ref[0, 1:2, :]
    nw = nw_ref[...]

    def finish(c, y_dir, xs):
        r0 = pl.multiple_of(c * CHUNK, CHUNK)
        y = y_scr[pl.ds(r0, CHUNK), :] + y_dir + xs * d_sum
        gated = y * _silu_of_twice(0.5 * z_ref[0, pl.ds(r0, CHUNK), :].astype(F32))
        ms = jnp.mean(gated * gated, axis=-1, keepdims=True)
        y_ref[0, pl.ds(r0, CHUNK), :] = (gated * lax.rsqrt(ms + EPS) * nw).astype(y_ref.dtype)

    def visits(i):
        fwd = [i * SCAN_UNROLL + u for u in range(SCAN_UNROLL)]
        return [(c, True) for c in fwd] + [(N_CHUNKS - 1 - c, False) for c in fwd]

    def first_half(i, carry):
        for c, causal in visits(i):
            r0 = pl.multiple_of(c * CHUNK, CHUNK)
            y_scr[pl.ds(r0, CHUNK), :] = scan(c, prepare(c), causal)
        return carry

    def second_half(i, carry):
        for c, causal in visits(i):
            tiles = recall(c)
            finish(c, scan(c, tiles, causal), tiles[0])
        return carry

    half = N_CHUNKS // 2 // SCAN_UNROLL
    lax.fori_loop(0, half, first_half, 0)
    lax.fori_loop(half, 2 * half, second_half, 0)


def _ssd(proj, dtr, cproj, cdtr, conv_w, conv_b, dtb, alog, dsk, norm_w, tri):
    gx = COL_X // GROUP_W
    gz = COL_Z // GROUP_W
    gb = COL_B // SSD_STATE
    gc = COL_C // SSD_STATE
    in_specs = [
        pl.BlockSpec((1, SEQ, GROUP_W), lambda b, g: (b, 0, gx + g)),
        pl.BlockSpec((1, SEQ, SSD_STATE), lambda b, g: (b, 0, gb + g)),
        pl.BlockSpec((1, SEQ, SSD_STATE), lambda b, g: (b, 0, gc + g)),
        pl.BlockSpec((1, SEQ, GROUP_W), lambda b, g: (b, 0, gz + g)),
        pl.BlockSpec((1, 1, N_CHUNKS, 8, CHUNK), lambda b, g: (g, b, 0, 0, 0)),
        pl.BlockSpec((1, CTX_LEN, GROUP_W), lambda b, g: (b, 0, gx + g)),
        pl.BlockSpec((1, CTX_LEN, SSD_STATE), lambda b, g: (b, 0, gb + g)),
        pl.BlockSpec((1, 1, N_CTX_CHUNKS, 8, CHUNK), lambda b, g: (g, b, 0, 0, 0)),
        pl.BlockSpec((3, GROUP_W), lambda b, g: (0, gx + g)),
        pl.BlockSpec((3, SSD_STATE), lambda b, g: (0, gb + g)),
        pl.BlockSpec((3, SSD_STATE), lambda b, g: (0, gc + g)),
        pl.BlockSpec((1, GROUP_W), lambda b, g: (0, gx + g)),
        pl.BlockSpec((1, SSD_STATE), lambda b, g: (0, gb + g)),
        pl.BlockSpec((1, SSD_STATE), lambda b, g: (0, gc + g)),
        pl.BlockSpec((1, 8, 1), lambda b, g: (g, 0, 0)),
        pl.BlockSpec((1, 8, 1), lambda b, g: (g, 0, 0)),
        pl.BlockSpec((1, 2, GROUP_W), lambda b, g: (g, 0, 0)),
        pl.BlockSpec((1, GROUP_W), lambda b, g: (0, g)),
        pl.BlockSpec((2, CHUNK, CHUNK), lambda b, g: (0, 0, 0)),
        pl.BlockSpec((2, CHUNK, 3 * GROUP_W), lambda b, g: (0, 0, 0)),
    ]
    return pl.pallas_call(
        _ssd_kernel,
        grid=(BATCH, SSD_GROUPS),
        in_specs=in_specs,
        out_specs=pl.BlockSpec((1, SEQ, GROUP_W), lambda b, g: (b, 0, g)),
        out_shape=jax.ShapeDtypeStruct((BATCH, SEQ, D_MODEL), BF16),
        scratch_shapes=[pltpu.VMEM((SEQ, GROUP_W), F32),
                        pltpu.VMEM((N_CHUNKS, 2, 2 * CHUNK, LANES), BF16),
                        pltpu.VMEM((N_CHUNKS, CHUNK, CHUNK), BF16),
                        pltpu.VMEM((N_CHUNKS, CHUNK, CHUNK), BF16),
                        pltpu.VMEM((N_CHUNKS, CHUNK, CHUNK), F32),
                        pltpu.VMEM((SEQ, GROUP_W), F32),
                        pltpu.VMEM((SSD_STATE, GROUP_W), F32),
                        pltpu.VMEM((SSD_STATE, GROUP_W), F32),
                        pltpu.VMEM((N_CHUNKS, CHUNK, CHUNK), BF16),
                        pltpu.VMEM((N_CHUNKS, SEG_ROWS, 8 * CHUNK), BF16),
                        pltpu.VMEM((N_CHUNKS, 8, CHUNK), F32),
                        pltpu.VMEM((N_CTX_CHUNKS, CHUNK, CHUNK), BF16)],
        compiler_params=_params(("arbitrary", "arbitrary")),
        name="ssd",
    )(proj, proj, proj, proj, dtr, cproj, cproj, cdtr,
      conv_w, conv_w, conv_w, conv_b, conv_b, conv_b, dtb, alog, dsk, norm_w, tri, _spread_matrix())


def _spread_matrix():
    sel = np.zeros((2, CHUNK, 3 * GROUP_W), np.float32)
    for d in range(2):
        for grp, comp in enumerate((TAB_EXP, TAB_W, TAB_DEC)):
            for part in range(2):
                for h in range(SSD_REP):
                    lanes = slice(grp * GROUP_W + h * SSD_HEAD_DIM,
                                  grp * GROUP_W + (h + 1) * SSD_HEAD_DIM)
                    sel[d, (comp + part) * 8 + d * SSD_REP + h, lanes] = 1.0
    return jnp.asarray(sel, BF16)


def _pool_constants(tm):
    t = np.arange(tm)
    col = t % GRID_W
    wins, invs = [], []
    for w in POOL_WINDOWS:
        start = np.clip(col - w // 2, 0, GRID_W)
        end = np.clip(col + w - w // 2, 0, GRID_W)
        same_row = (t[:, None] // GRID_W) == (t[None, :] // GRID_W)
        inside = (col[None, :] >= start[:, None]) & (col[None, :] < end[:, None])
        wins.append((same_row & inside).astype(np.float32))
        invs.append((1.0 / (end - start).astype(np.float32))[:, None])
    return jnp.asarray(np.stack(wins), BF16), jnp.asarray(np.stack(invs), F32)


def _merge_kernel(ys_ref, v_ref, win_ref, inv_ref, pw_ref, ps_ref, ws_ref, wp_ref, gs_ref, gp_ref,
                  o_ref, yp_scr):
    slab = win_ref.shape[1]
    for k in range(len(POOL_WINDOWS)):
        cols = slice(k * POOL_GW, (k + 1) * POOL_GW)
        parts = []
        for r in range(0, v_ref.shape[0], slab):
            v = v_ref[r:r + slab, cols]
            parts.append((_dot(win_ref[k], v) * inv_ref[k] - v.astype(F32)).astype(BF16))
        p = jnp.concatenate(parts, axis=0)
        yp_scr[:, cols] = (_dot(p, pw_ref[k]) * ps_ref[:, cols]).astype(BF16)
    a = jax.nn.sigmoid(gs_ref[...].astype(F32)) * _dot(ys_ref[...], ws_ref[...])
    b = jax.nn.sigmoid(gp_ref[...].astype(F32)) * _dot(yp_scr[...], wp_ref[...])
    o_ref[...] = (a + b).astype(o_ref.dtype)


def _merge(y_ssd, pool_w, pool_scale, w_s, w_p, proj2d, tm=512, slab=128):
    win, inv = _pool_constants(slab)
    n_win = len(POOL_WINDOWS)
    once = pl.Buffered(1)
    wide = lambda col: pl.BlockSpec((tm, D_MODEL), lambda i: (i, col))
    return pl.pallas_call(
        _merge_kernel,
        grid=(TOKENS // tm,),
        in_specs=[wide(0),
                  wide(COL_V // D_MODEL),
                  pl.BlockSpec((n_win, slab, slab), lambda i: (0, 0, 0), pipeline_mode=once),
                  pl.BlockSpec((n_win, slab, 1), lambda i: (0, 0, 0), pipeline_mode=once),
                  pl.BlockSpec((n_win, POOL_GW, POOL_GW), lambda i: (0, 0, 0), pipeline_mode=once),
                  pl.BlockSpec((1, D_MODEL), lambda i: (0, 0)),
                  pl.BlockSpec((D_MODEL, D_MODEL), lambda i: (0, 0), pipeline_mode=once),
                  pl.BlockSpec((D_MODEL, D_MODEL), lambda i: (0, 0), pipeline_mode=once),
                  wide(COL_GS // D_MODEL),
                  wide(COL_GP // D_MODEL)],
        out_specs=pl.BlockSpec((tm, D_MODEL), lambda i: (i, 0)),
        out_shape=jax.ShapeDtypeStruct((TOKENS, D_MODEL), BF16),
        scratch_shapes=[pltpu.VMEM((tm, D_MODEL), BF16)],
        compiler_params=_params(("arbitrary",)),
        name="merge",
    )(y_ssd, proj2d, win, inv, pool_w, pool_scale, w_s, w_p, proj2d, proj2d)


def _outproj_kernel(m_ref, wo_ref, x_ref, g1_ref, sh_ref, sc_ref, nw_ref, x1_ref, h2_ref):
    x1_ref[...] = x_ref[...] + g1_ref[0] * _dot(m_ref[...], wo_ref[...])
    _modulate_rows(x1_ref, nw_ref[...], sc_ref[0], sh_ref[0], h2_ref, x1_ref.shape[0], rc=128)


def _outproj(merged, w_o, x2d, g1, sh2, sc2, nw2, tm=512):
    tpb = SEQ // tm
    mod_spec = pl.BlockSpec((1, 1, D_MODEL), lambda i: (i // tpb, 0, 0))
    return pl.pallas_call(
        _outproj_kernel,
        grid=(TOKENS // tm,),
        in_specs=[pl.BlockSpec((tm, D_MODEL), lambda i: (i, 0)),
                  pl.BlockSpec((D_MODEL, D_MODEL), lambda i: (0, 0), pipeline_mode=pl.Buffered(1)),
                  pl.BlockSpec((tm, D_MODEL), lambda i: (i, 0)),
                  mod_spec, mod_spec, mod_spec,
                  pl.BlockSpec((1, D_MODEL), lambda i: (0, 0))],
        out_specs=[pl.BlockSpec((tm, D_MODEL), lambda i: (i, 0)),
                   pl.BlockSpec((tm, D_MODEL), lambda i: (i, 0))],
        out_shape=[jax.ShapeDtypeStruct((TOKENS, D_MODEL), F32),
                   jax.ShapeDtypeStruct((TOKENS, D_MODEL), BF16)],
        compiler_params=_params(("arbitrary",)),
        name="outproj",
    )(merged, w_o, x2d, g1, sh2, sc2, nw2)


def _glu_kernel(h_ref, hp_ref, hn_ref, wa_ref, wg_ref, fw_ref, fb_ref, u_ref,
                hext_scr, a_even, g_even, a_odd, g_odd, *, n_chunks, n_steps):
    s = pl.program_id(0)
    tm = h_ref.shape[0]
    cur = jnp.minimum(s, n_steps - 2)
    i = cur // n_chunks

    @pl.when(s == 0)
    def _():
        a_odd[...] = jnp.zeros(a_odd.shape, F32)
        g_odd[...] = jnp.zeros(g_odd.shape, F32)

    @pl.when(cur % n_chunks == 0)
    def _():
        has_prev = ((i * tm) % SEQ) != 0
        has_next = (((i + 1) * tm) % SEQ) != 0
        hext_scr[0:GRID_W, :] = jnp.where(has_prev, hp_ref[...], jnp.zeros_like(hp_ref[...]))
        hext_scr[GRID_W:GRID_W + tm, :] = h_ref[...]
        hext_scr[GRID_W + tm:, :] = jnp.where(has_next, hn_ref[...], jnp.zeros_like(hn_ref[...]))

    def step(wr_a, wr_g, rd_a, rd_g):
        fw = 0.5 * fw_ref[...]
        fb = 0.5 * fb_ref[...]
        for r in range(0, tm, GLU_ROWS):
            hv = (rd_g[r:r + GLU_ROWS, :] * fw[0:1, :] + fb
                  + rd_g[r + GRID_W:r + GRID_W + GLU_ROWS, :] * fw[1:2, :]
                  + rd_g[r + 2 * GRID_W:r + 2 * GRID_W + GLU_ROWS, :] * fw[2:3, :])
            gelu = hv * (1.0 + lax.erf(hv * np.float32(np.sqrt(2.0))))
            u_ref[r:r + GLU_ROWS, :] = (gelu * rd_a[r:r + GLU_ROWS, :]).astype(u_ref.dtype)
        for r in range(0, tm, DOT_ROWS):
            wr_a[r:r + DOT_ROWS, :] = _dot(hext_scr[GRID_W + r:GRID_W + r + DOT_ROWS, :], wa_ref[...])
        for r in range(0, tm + 2 * GRID_W, DOT_ROWS):
            n = min(DOT_ROWS, tm + 2 * GRID_W - r)
            wr_g[r:r + n, :] = _dot(hext_scr[r:r + n, :], wg_ref[...])

    @pl.when(s % 2 == 0)
    def _():
        step(a_even, g_even, a_odd, g_odd)

    @pl.when(s % 2 == 1)
    def _():
        step(a_odd, g_odd, a_even, g_even)


def _glu(h2, w_up, ffn_conv_w, ffn_conv_b, tm=2048, tn=GLU_TN):
    n_chunks = D_FF // tn
    n_tiles = TOKENS // tm
    n_steps = n_tiles * n_chunks + 1
    hb = tm // GRID_W
    n_halo = TOKENS // GRID_W

    def cur(s):
        return jnp.minimum(s, n_steps - 2)

    def prev(s):
        return jnp.maximum(s - 1, 0)

    return pl.pallas_call(
        functools.partial(_glu_kernel, n_chunks=n_chunks, n_steps=n_steps),
        grid=(n_steps,),
        in_specs=[pl.BlockSpec((tm, D_MODEL), lambda s: (cur(s) // n_chunks, 0),
                               pipeline_mode=pl.Buffered(1)),
                  pl.BlockSpec((GRID_W, D_MODEL),
                               lambda s: (jnp.maximum((cur(s) // n_chunks) * hb - 1, 0), 0)),
                  pl.BlockSpec((GRID_W, D_MODEL),
                               lambda s: (jnp.minimum((cur(s) // n_chunks + 1) * hb, n_halo - 1), 0)),
                  pl.BlockSpec((D_MODEL, tn), lambda s: (0, cur(s) % n_chunks)),
                  pl.BlockSpec((D_MODEL, tn), lambda s: (0, n_chunks + cur(s) % n_chunks)),
                  pl.BlockSpec((3, tn), lambda s: (0, prev(s) % n_chunks)),
                  pl.BlockSpec((1, tn), lambda s: (0, prev(s) % n_chunks))],
        out_specs=pl.BlockSpec((tm, tn), lambda s: (prev(s) // n_chunks, prev(s) % n_chunks)),
        out_shape=jax.ShapeDtypeStruct((TOKENS, D_FF), BF16),
        scratch_shapes=[pltpu.VMEM((tm + 2 * GRID_W, D_MODEL), BF16),
                        pltpu.VMEM((tm, tn), F32), pltpu.VMEM((tm + 2 * GRID_W, tn), F32),
                        pltpu.VMEM((tm, tn), F32), pltpu.VMEM((tm + 2 * GRID_W, tn), F32)],
        compiler_params=_params(("arbitrary",)),
        name="glu",
    )(h2, h2, h2, w_up, w_up, ffn_conv_w, ffn_conv_b)


def _down_kernel(u_ref, wd_ref, x1_ref, g2_ref, nw_ref, o_ref):
    x2 = x1_ref[...] + g2_ref[0] * _dot(u_ref[...], wd_ref[...])
    ms = jnp.mean(x2 * x2, axis=-1, keepdims=True)
    o_ref[...] = x2 * lax.rsqrt(ms + EPS) * nw_ref[...]


def _down(u, w_down, x1, g2, fnw, tm=512):
    tpb = SEQ // tm
    return pl.pallas_call(
        _down_kernel,
        grid=(TOKENS // tm,),
        in_specs=[pl.BlockSpec((tm, D_FF), lambda i: (i, 0)),
                  pl.BlockSpec((D_FF, D_MODEL), lambda i: (0, 0), pipeline_mode=pl.Buffered(1)),
                  pl.BlockSpec((tm, D_MODEL), lambda i: (i, 0)),
                  pl.BlockSpec((1, 1, D_MODEL), lambda i: (i // tpb, 0, 0)),
                  pl.BlockSpec((1, D_MODEL), lambda i: (0, 0))],
        out_specs=pl.BlockSpec((tm, D_MODEL), lambda i: (i, 0)),
        out_shape=jax.ShapeDtypeStruct((TOKENS, D_MODEL), F32),
        compiler_params=_params(("arbitrary",)),
        name="down",
    )(u, w_down, x1, g2, fnw)


def _dt_rows(dt_raw, batch, n_chunks):
    d = dt_raw[:, :2 * SSD_HEADS].reshape(batch, n_chunks, CHUNK, 2, SSD_GROUPS, SSD_REP)
    d = jnp.transpose(d, (4, 0, 1, 3, 5, 2))
    return d.reshape(SSD_GROUPS, batch, n_chunks, 2 * SSD_REP, CHUNK)


def _group_rows(p):
    p = p.reshape(2, SSD_GROUPS, SSD_REP)
    return jnp.transpose(p, (1, 0, 2)).reshape(SSD_GROUPS, 2 * SSD_REP, 1)


def kernel(x, c, ctx, c_ctx, w_ada, b_ada, norm1_w, w_in, ssd_conv_w, ssd_conv_b, dt_bias, a_log,
           d_skip, ssd_norm_w, w_ssd_out, pool_w, pool_scale, w_pool_out, w_o, norm2_w, w_up,
           ffn_conv_w, ffn_conv_b, w_down, final_norm_w):
    assert w_in.shape[0] == 1, "single-layer block only"
    d = D_MODEL
    x2d = x.reshape(TOKENS, d)
    ctx2d = ctx.reshape(BATCH * CTX_LEN, d)

    cs = jnp.zeros((8, d), F32).at[:BATCH].set(c).at[BATCH].set(c_ctx)
    mod = _ada(cs, w_ada[0], b_ada)
    sh1, sc1, g1, sh2, sc2, g2 = [mod[:BATCH, k * d:(k + 1) * d].reshape(BATCH, 1, d)
                                  for k in range(6)]
    csh1 = mod[BATCH:BATCH + 1, 0:d].reshape(1, 1, d)
    csc1 = mod[BATCH:BATCH + 1, d:2 * d].reshape(1, 1, d)

    w_in0 = w_in[0]
    n_dt = 2 * SSD_HEADS
    w_cat, w_dt = _drop_leading_cols(jnp.transpose(w_in0), n_dt)
    w_s = w_ssd_out[0].astype(BF16)
    w_p = w_pool_out[0].astype(BF16)
    w_ob = w_o[0].astype(BF16)
    w_upb = w_up[0].astype(BF16)
    w_db = w_down[0].astype(BF16)
    pool_wb = pool_w[0].astype(BF16)

    proj, dt_raw = _inproj(x2d, sh1, sc1, norm1_w, w_cat, w_dt, n_cols=N_PROJ, rows_per_mod=SEQ)
    cproj, cdt_raw = _inproj(ctx2d, csh1, csc1, norm1_w, w_cat, w_dt, n_cols=N_CTX_PROJ,
                             rows_per_mod=BATCH * CTX_LEN)

    dtr = _dt_rows(dt_raw, BATCH, N_CHUNKS)
    cdtr = _dt_rows(cdt_raw, BATCH, N_CTX_CHUNKS)
    dsk = jnp.repeat(jnp.transpose(d_skip[0].reshape(2, SSD_GROUPS, SSD_REP), (1, 0, 2)),
                     SSD_HEAD_DIM, axis=-1)
    idx = np.arange(CHUNK)
    tri = jnp.asarray(np.stack([idx[:, None] <= idx[None, :], idx[:, None] >= idx[None, :]])
                      .astype(np.float32), BF16)

    y_ssd = _ssd(proj.reshape(BATCH, SEQ, N_PROJ), dtr,
                 cproj.reshape(BATCH, CTX_LEN, N_CTX_PROJ), cdtr,
                 ssd_conv_w[0], ssd_conv_b, _group_rows(dt_bias[0]), _group_rows(a_log[0]),
                 dsk, ssd_norm_w, tri)
    merged = _merge(y_ssd.reshape(TOKENS, d), pool_wb, pool_scale, w_s, w_p, proj)
    x1, h2 = _outproj(merged, w_ob, x2d, g1, sh2, sc2, norm2_w)
    u = _glu(h2, w_upb, ffn_conv_w[0], ffn_conv_b)
    out = _down(u, w_db, x1, g2, final_norm_w.reshape(1, d))
    return out.reshape(BATCH, SEQ, d)
```

```python
import functools

import numpy as np
import jax
import jax.numpy as jnp
from jax import lax
from jax.experimental import pallas as pl
from jax.experimental.pallas import tpu as pltpu

F32 = jnp.float32
BF16 = jnp.bfloat16

D_MODEL = 2048
BATCH = 4
SEQ = 4096
TOKENS = BATCH * SEQ
GRID_W = 64
CTX_LEN = 256

SSD_HEAD_DIM = 64
SSD_HEADS = 32
SSD_GROUPS = 8
SSD_REP = 4
SSD_STATE = 128
CHUNK = 128
GROUP_W = SSD_REP * SSD_HEAD_DIM
N_CHUNKS = SEQ // CHUNK
N_CTX_CHUNKS = CTX_LEN // CHUNK
SCAN_UNROLL = 16
GLU_ROWS = 16
DOT_ROWS = 1024
INPROJ_TN, GLU_TN = 1536, 512

POOL_WINDOWS = (2, 4, 8, 16)
POOL_GW = 512
D_FF = 5632
EPS = 1e-6

COL_X = 0
COL_B = 2048
COL_C = 3072
COL_Z = 4096
COL_V = 6144
COL_GS = 8192
COL_GP = 10240
N_PROJ = 12288
N_CTX_PROJ = 3072

LANES = 128
VMEM_LIMIT = 56 * 1024 * 1024


def _dot(a, b):
    return jnp.dot(a, b, preferred_element_type=F32)


def _silu(x):
    return x * jax.nn.sigmoid(x)


def _silu_of_twice(h):
    return h + h * jnp.tanh(h)


def _softplus(x):
    return jnp.maximum(x, 0.0) + jnp.log1p(jnp.exp(-jnp.abs(x)))


def _params(sem):
    return pltpu.CompilerParams(dimension_semantics=sem, vmem_limit_bytes=VMEM_LIMIT)


def _shift_cast_kernel(a_ref, b_ref, head_ref, o_ref, head_o_ref, *, shift):
    tn = o_ref.shape[1]

    @pl.when(pl.program_id(0) == 0)
    def _():
        head_o_ref[...] = head_ref[...].T

    x = jnp.concatenate([a_ref[shift:, :], b_ref[...]], axis=0)
    j = pl.program_id(0)
    x = x * jnp.where(jnp.logical_and(j >= COL_Z // tn, j < COL_V // tn), 0.5, 1.0)
    for r in range(0, tn, LANES):
        o_ref[:, r:r + LANES] = x[r:r + LANES, :].T.astype(o_ref.dtype)


def _drop_leading_cols(w_t, shift, tn=1024):
    n, k = w_t.shape
    n_out = n - shift
    return pl.pallas_call(
        functools.partial(_shift_cast_kernel, shift=shift),
        grid=(n_out // tn,),
        in_specs=[pl.BlockSpec((tn, k), lambda j: (j, 0)),
                  pl.BlockSpec((shift, k), lambda j: ((j + 1) * (tn // shift), 0)),
                  pl.BlockSpec((LANES, k), lambda j: (0, 0))],
        out_specs=[pl.BlockSpec((k, tn), lambda j: (0, j)),
                   pl.BlockSpec((k, LANES), lambda j: (0, 0))],
        out_shape=[jax.ShapeDtypeStruct((k, n_out), BF16),
                   jax.ShapeDtypeStruct((k, LANES), F32)],
        compiler_params=_params(("arbitrary",)),
        name="wprep",
    )(w_t, w_t, w_t)


def _ada_kernel(c_ref, w_ref, b_ref, o_ref):
    s_hi, s_lo = _split_bf16(_silu(c_ref[...]), 2)
    lhs = jnp.concatenate([s_hi, s_lo], axis=0).astype(BF16)
    w = w_ref[...]
    w_hi = w.astype(BF16)
    w_lo = (w - w_hi.astype(F32)).astype(BF16)
    r = _dot(lhs, w_hi) + _dot(lhs, w_lo)
    o_ref[...] = r[0:8] + r[8:16] + b_ref[...]


def _ada(cs, w, b):
    tn = 1024
    n = w.shape[1]
    return pl.pallas_call(
        _ada_kernel,
        grid=(n // tn,),
        in_specs=[pl.BlockSpec((8, D_MODEL), lambda j: (0, 0)),
                  pl.BlockSpec((D_MODEL, tn), lambda j: (0, j)),
                  pl.BlockSpec((1, tn), lambda j: (0, j))],
        out_specs=pl.BlockSpec((8, tn), lambda j: (0, j)),
        out_shape=jax.ShapeDtypeStruct((8, n), F32),
        compiler_params=_params(("arbitrary",)),
        name="ada",
    )(cs, w, b)


def _modulate_rows(x_ref, nw, sc, sh, out_ref, rows, rc=256):
    gain = nw * (1.0 + sc)

    def body(k, carry):
        r = pl.multiple_of(k * rc, rc)
        x = x_ref[pl.ds(r, rc), :]
        inv = lax.rsqrt(jnp.mean(x * x, axis=-1, keepdims=True) + EPS)
        out_ref[pl.ds(r, rc), :] = (x_ref[pl.ds(r, rc), :] * inv * gain + sh).astype(out_ref.dtype)
        return carry
    lax.fori_loop(0, rows // rc, body, 0)


def _inproj_kernel(x_ref, sh_ref, sc_ref, nw_ref, w_ref, wdt_ref, o_ref, dt_ref, h_scr):
    @pl.when(pl.program_id(1) == 0)
    def _():
        _modulate_rows(x_ref, nw_ref[...], sc_ref[0], sh_ref[0], h_scr, h_scr.shape[0])
        dt_ref[...] = _dot(h_scr[...], wdt_ref[...].astype(BF16))

    o_ref[...] = _dot(h_scr[...], w_ref[...]).astype(o_ref.dtype)


def _inproj(x2d, sh, sc, nw, w_cat, w_dt, *, n_cols, rows_per_mod, tm=1024, tn=INPROJ_TN):
    m = x2d.shape[0]
    tpb = rows_per_mod // tm
    return pl.pallas_call(
        _inproj_kernel,
        grid=(m // tm, n_cols // tn),
        in_specs=[pl.BlockSpec((tm, D_MODEL), lambda i, j: (i, 0)),
                  pl.BlockSpec((1, 1, D_MODEL), lambda i, j: (i // tpb, 0, 0)),
                  pl.BlockSpec((1, 1, D_MODEL), lambda i, j: (i // tpb, 0, 0)),
                  pl.BlockSpec((1, D_MODEL), lambda i, j: (0, 0)),
                  pl.BlockSpec((D_MODEL, tn), lambda i, j: (0, j)),
                  pl.BlockSpec((D_MODEL, LANES), lambda i, j: (0, 0))],
        out_specs=[pl.BlockSpec((tm, tn), lambda i, j: (i, j)),
                   pl.BlockSpec((tm, LANES), lambda i, j: (i, 0))],
        out_shape=[jax.ShapeDtypeStruct((m, n_cols), BF16),
                   jax.ShapeDtypeStruct((m, LANES), F32)],
        scratch_shapes=[pltpu.VMEM((tm, D_MODEL), BF16)],
        compiler_params=_params(("arbitrary", "arbitrary")),
        name="inproj",
    )(x2d, sh, sc, nw, w_cat, w_dt)


def _conv_silu_tile(ref, c, n_chunks, w, bias):
    length = n_chunks * CHUNK
    if isinstance(c, int):
        r0, p0, n0 = c * CHUNK, max(c * CHUNK - 16, 0), min(c * CHUNK + CHUNK, length - 16)
    else:
        r0 = pl.multiple_of(c * CHUNK, CHUNK)
        p0 = pl.multiple_of(jnp.maximum(c * CHUNK - 16, 0), 16)
        n0 = pl.multiple_of(jnp.minimum(c * CHUNK + CHUNK, length - 16), 16)
    x = ref[0, pl.ds(r0, CHUNK), :].astype(F32)
    prev_row = ref[0, pl.ds(p0, 16), :].astype(F32)[15:16, :]
    prev_row = jnp.where(c > 0, prev_row, 0.0)
    next_row = ref[0, pl.ds(n0, 16), :].astype(F32)[0:1, :]
    next_row = jnp.where(c < n_chunks - 1, next_row, 0.0)
    rows = lax.broadcasted_iota(jnp.int32, (8, x.shape[1]), 0)
    xm = pltpu.roll(x, 1, axis=0)
    xm = jnp.concatenate([jnp.where(rows == 0, prev_row, xm[0:8]), xm[8:]], axis=0)
    xp = pltpu.roll(x, CHUNK - 1, axis=0)
    xp = jnp.concatenate([xp[:CHUNK - 8], jnp.where(rows == 7, next_row, xp[CHUNK - 8:])], axis=0)
    hw = 0.5 * w
    half_y = xm * hw[0:1, :] + x * hw[1:2, :] + xp * hw[2:3, :] + 0.5 * bias
    return _silu_of_twice(half_y)


def _split_bf16(x, parts):
    out = []
    for _ in range(parts - 1):
        hi = x.astype(BF16).astype(F32)
        out.append(hi)
        x = x - hi
    out.append(x.astype(BF16).astype(F32))
    return out


TAB_ACS, TAB_ONE, TAB_EXP, TAB_W, TAB_DEC, TAB_USED = 0, 3, 6, 8, 10, 12
SEG_ROWS = 48


def _decay_tables(dt_raw, dt_bias, a_neg, tri_ref, a_scr, bsel_scr=None, dt_scr=None):
    n_chunks = dt_raw.shape[0]
    n = n_chunks * 8
    dt = _softplus(dt_raw + dt_bias)
    a = (dt * a_neg).reshape(n, CHUNK)
    dt = dt.reshape(n, CHUNK)
    hi, lo = _split_bf16(a, 2)
    lhs = jnp.concatenate([hi, lo], axis=0).astype(BF16)
    pre = _dot(lhs, tri_ref[0])
    suf = _dot(lhs, tri_ref[1])
    row = lax.broadcasted_iota(jnp.int32, (n, CHUNK), 0)
    fwd = jnp.bitwise_and(row, 7) < SSD_REP
    acs = jnp.where(fwd, pre[:n] + pre[n:], suf[:n] + suf[n:])
    tot = jnp.where(fwd[:, 0:1], acs[:, CHUNK - 1:CHUNK], acs[:, 0:1])
    ones = jnp.ones((n, CHUNK), F32)
    comps_a = (_split_bf16(acs, 3) + [ones, ones, ones] + _split_bf16(jnp.exp(acs), 2)
               + _split_bf16(jnp.exp(tot - acs) * dt, 2)
               + _split_bf16(jnp.broadcast_to(jnp.exp(tot), acs.shape), 2))
    comps_b = [ones, ones, ones] + [-p for p in _split_bf16(acs, 3)]
    row48 = jnp.bitwise_and(lax.broadcasted_iota(jnp.int32, (SEG_ROWS, CHUNK), 0), 7)
    pad = jnp.zeros((CHUNK - 8 * TAB_USED, CHUNK), F32)
    for c in range(n_chunks):
        sl = slice(c * 8, c * 8 + 8)
        a_scr[c] = jnp.concatenate([p[sl] for p in comps_a] + [pad], axis=0).T.astype(BF16)
        if bsel_scr is None:
            continue
        b_all = jnp.concatenate([p[sl] for p in comps_b], axis=0)
        for k in range(8):
            bsel_scr[c, :, k * CHUNK:(k + 1) * CHUNK] = jnp.where(row48 == k, b_all, 0.0).astype(BF16)
        dt_scr[c] = dt[sl]


def _split_heads(t):
    lane = lax.broadcasted_iota(jnp.int32, t.shape, 1)
    return (jnp.where(lane < SSD_HEAD_DIM, t, 0.0).astype(BF16),
            jnp.where(lane >= SSD_HEAD_DIM, t, 0.0).astype(BF16))


def _chunk_dir(xs, x_pairs, bt, cc, cb, tab, bsel, dt, sel, off, h_ref, causal):
    l_io = lax.broadcasted_iota(jnp.int32, (CHUNK, CHUNK), 0)
    s_io = lax.broadcasted_iota(jnp.int32, (CHUNK, CHUNK), 1)
    mask = (l_io >= s_io) if causal else (l_io <= s_io)
    seg = _dot(tab, jnp.concatenate([bsel, jnp.zeros((CHUNK - SEG_ROWS, bsel.shape[1]), BF16)],
                                    axis=0))
    spread = _dot(tab, sel)
    e_exp, w_exp, dec = spread[:, 0:GROUP_W], spread[:, GROUP_W:2 * GROUP_W], spread[0:1, 2 * GROUP_W:]
    h = h_ref[...]
    h_ref[...] = h * dec + _dot(bt, (xs * w_exp).astype(BF16))
    y = _dot(cc, h.astype(BF16)) * e_exp
    ys = []
    for pair in range(2):
        ms = []
        for r in (2 * pair, 2 * pair + 1):
            decay = jnp.exp(jnp.where(mask, seg[:, r * CHUNK:(r + 1) * CHUNK], -jnp.inf))
            ms.append((cb * decay * dt[off + r:off + r + 1, :]).astype(BF16))
        ys.append(_dot(jnp.concatenate(ms, axis=1), x_pairs[pair]))
    return y + jnp.concatenate(ys, axis=1)


def _ssd_kernel(xs_ref, b_ref, c_ref, z_ref, dtr_ref, cxs_ref, cbm_ref, cdtr_ref,
                cwx_ref, cwb_ref, cwc_ref, bx_ref, bb_ref, bc_ref,
                dtb_ref, alog_ref, dsk_ref, nw_ref, tri_ref, sel_ref,
                y_ref,
                xs_scr, xp_scr, bt_scr, cc_scr, cb_scr, y_scr, hf_scr, hb_scr,
                tab_scr, bsel_scr, dt_scr, ctab_scr):
    dt_bias = dtb_ref[0]
    a_neg = -jnp.exp(alog_ref[0])
    cwx, cwb, cwc = cwx_ref[...], cwb_ref[...], cwc_ref[...]
    bx, bb, bc = bx_ref[...], bb_ref[...], bc_ref[...]

    hf_scr[...] = jnp.zeros(hf_scr.shape, F32)
    hb_scr[...] = jnp.zeros(hb_scr.shape, F32)
    _decay_tables(cdtr_ref[0, 0], dt_bias, a_neg, tri_ref, ctab_scr)
    ctx_tiles = []
    for c in range(N_CTX_CHUNKS):
        xs = _conv_silu_tile(cxs_ref, c, N_CTX_CHUNKS, cwx, bx)
        bt = _conv_silu_tile(cbm_ref, c, N_CTX_CHUNKS, cwb, bb).T.astype(BF16)
        ctx_tiles.append((xs, bt))
    for d, h_ref, order in ((0, hf_scr, range(N_CTX_CHUNKS)), (1, hb_scr, reversed(range(N_CTX_CHUNKS)))):
        for c in order:
            xs, bt = ctx_tiles[c]
            spread = _dot(ctab_scr[c], sel_ref[d])
            w_exp, dec = spread[:, GROUP_W:2 * GROUP_W], spread[0:1, 2 * GROUP_W:]
            h_ref[...] = h_ref[...] * dec + _dot(bt, (xs * w_exp).astype(BF16))

    _decay_tables(dtr_ref[0, 0], dt_bias, a_neg, tri_ref, tab_scr, bsel_scr, dt_scr)

    def prepare(c):
        r0 = pl.multiple_of(c * CHUNK, CHUNK)
        xs = _conv_silu_tile(xs_ref, c, N_CHUNKS, cwx, bx)
        bt = _conv_silu_tile(b_ref, c, N_CHUNKS, cwb, bb).T.astype(BF16)
        cc = _conv_silu_tile(c_ref, c, N_CHUNKS, cwc, bc).astype(BF16)
        cb = _dot(cc, bt)
        x_pairs = [jnp.concatenate(_split_heads(xs[:, p * LANES:(p + 1) * LANES]), axis=0)
                   for p in range(2)]
        xs_scr[pl.ds(r0, CHUNK), :] = xs
        xp_scr[c, 0] = x_pairs[0]
        xp_scr[c, 1] = x_pairs[1]
        bt_scr[c] = bt
        cc_scr[c] = cc
        cb_scr[c] = cb
        return xs, x_pairs, bt, cc, cb

    def recall(c):
        r0 = pl.multiple_of(c * CHUNK, CHUNK)
        return (xs_scr[pl.ds(r0, CHUNK), :], [xp_scr[c, 0], xp_scr[c, 1]], bt_scr[c], cc_scr[c],
                cb_scr[c])

    def scan(c, tiles, causal):
        xs, x_pairs, bt, cc, cb = tiles
        d = 0 if causal else 1
        bsel = bsel_scr[c, :, d * SSD_REP * CHUNK:(d + 1) * SSD_REP * CHUNK]
        return _chunk_dir(xs, x_pairs, bt, cc, cb, tab_scr[c], bsel, dt_scr[c], sel_ref[d],
                          d * SSD_REP, hf_scr if causal else hb_scr, causal)

    d_sum = dsk_ref[0, 0:1, :] + dsk_ref[0, 1:2, :]
    nw = nw_ref[...]

    def finish(c, y_dir, xs):
        r0 = pl.multiple_of(c * CHUNK, CHUNK)
        y = y_scr[pl.ds(r0, CHUNK), :] + y_dir + xs * d_sum
        gated = y * _silu_of_twice(z_ref[0, pl.ds(r0, CHUNK), :].astype(F32))
        ms = jnp.mean(gated * gated, axis=-1, keepdims=True)
        y_ref[0, pl.ds(r0, CHUNK), :] = (gated * lax.rsqrt(ms + EPS) * nw).astype(y_ref.dtype)

    def visits(i):
        fwd = [i * SCAN_UNROLL + u for u in range(SCAN_UNROLL)]
        return [(c, True) for c in fwd] + [(N_CHUNKS - 1 - c, False) for c in fwd]

    def first_half(i, carry):
        for c, causal in visits(i):
            r0 = pl.multiple_of(c * CHUNK, CHUNK)
            y_scr[pl.ds(r0, CHUNK), :] = scan(c, prepare(c), causal)
        return carry

    def second_half(i, carry):
        for c, causal in visits(i):
            tiles = recall(c)
            finish(c, scan(c, tiles, causal), tiles[0])
        return carry

    half = N_CHUNKS // 2 // SCAN_UNROLL
    lax.fori_loop(0, half, first_half, 0)
    lax.fori_loop(half, 2 * half, second_half, 0)


def _ssd(proj, dtr, cproj, cdtr, conv_w, conv_b, dtb, alog, dsk, norm_w, tri):
    gx = COL_X // GROUP_W
    gz = COL_Z // GROUP_W
    gb = COL_B // SSD_STATE
    gc = COL_C // SSD_STATE
    in_specs = [
        pl.BlockSpec((1, SEQ, GROUP_W), lambda b, g: (b, 0, gx + g)),
        pl.BlockSpec((1, SEQ, SSD_STATE), lambda b, g: (b, 0, gb + g)),
        pl.BlockSpec((1, SEQ, SSD_STATE), lambda b, g: (b, 0, gc + g)),
        pl.BlockSpec((1, SEQ, GROUP_W), lambda b, g: (b, 0, gz + g)),
        pl.BlockSpec((1, 1, N_CHUNKS, 8, CHUNK), lambda b, g: (g, b, 0, 0, 0)),
        pl.BlockSpec((1, CTX_LEN, GROUP_W), lambda b, g: (b, 0, gx + g)),
        pl.BlockSpec((1, CTX_LEN, SSD_STATE), lambda b, g: (b, 0, gb + g)),
        pl.BlockSpec((1, 1, N_CTX_CHUNKS, 8, CHUNK), lambda b, g: (g, b, 0, 0, 0)),
        pl.BlockSpec((3, GROUP_W), lambda b, g: (0, gx + g)),
        pl.BlockSpec((3, SSD_STATE), lambda b, g: (0, gb + g)),
        pl.BlockSpec((3, SSD_STATE), lambda b, g: (0, gc + g)),
        pl.BlockSpec((1, GROUP_W), lambda b, g: (0, gx + g)),
        pl.BlockSpec((1, SSD_STATE), lambda b, g: (0, gb + g)),
        pl.BlockSpec((1, SSD_STATE), lambda b, g: (0, gc + g)),
        pl.BlockSpec((1, 8, 1), lambda b, g: (g, 0, 0)),
        pl.BlockSpec((1, 8, 1), lambda b, g: (g, 0, 0)),
        pl.BlockSpec((1, 2, GROUP_W), lambda b, g: (g, 0, 0)),
        pl.BlockSpec((1, GROUP_W), lambda b, g: (0, g)),
        pl.BlockSpec((2, CHUNK, CHUNK), lambda b, g: (0, 0, 0)),
        pl.BlockSpec((2, CHUNK, 3 * GROUP_W), lambda b, g: (0, 0, 0)),
    ]
    return pl.pallas_call(
        _ssd_kernel,
        grid=(BATCH, SSD_GROUPS),
        in_specs=in_specs,
        out_specs=pl.BlockSpec((1, SEQ, GROUP_W), lambda b, g: (b, 0, g)),
        out_shape=jax.ShapeDtypeStruct((BATCH, SEQ, D_MODEL), BF16),
        scratch_shapes=[pltpu.VMEM((SEQ, GROUP_W), F32),
                        pltpu.VMEM((N_CHUNKS, 2, 2 * CHUNK, LANES), BF16),
                        pltpu.VMEM((N_CHUNKS, CHUNK, CHUNK), BF16),
                        pltpu.VMEM((N_CHUNKS, CHUNK, CHUNK), BF16),
                        pltpu.VMEM((N_CHUNKS, CHUNK, CHUNK), F32),
                        pltpu.VMEM((SEQ, GROUP_W), F32),
                        pltpu.VMEM((SSD_STATE, GROUP_W), F32),
                        pltpu.VMEM((SSD_STATE, GROUP_W), F32),
                        pltpu.VMEM((N_CHUNKS, CHUNK, CHUNK), BF16),
                        pltpu.VMEM((N_CHUNKS, SEG_ROWS, 8 * CHUNK), BF16),
                        pltpu.VMEM((N_CHUNKS, 8, CHUNK), F32),
                        pltpu.VMEM((N_CTX_CHUNKS, CHUNK, CHUNK), BF16)],
        compiler_params=_params(("arbitrary", "arbitrary")),
        name="ssd",
    )(proj, proj, proj, proj, dtr, cproj, cproj, cdtr,
      conv_w, conv_w, conv_w, conv_b, conv_b, conv_b, dtb, alog, dsk, norm_w, tri, _spread_matrix())


def _spread_matrix():
    sel = np.zeros((2, CHUNK, 3 * GROUP_W), np.float32)
    for d in range(2):
        for grp, comp in enumerate((TAB_EXP, TAB_W, TAB_DEC)):
            for part in range(2):
                for h in range(SSD_REP):
                    lanes = slice(grp * GROUP_W + h * SSD_HEAD_DIM,
                                  grp * GROUP_W + (h + 1) * SSD_HEAD_DIM)
                    sel[d, (comp + part) * 8 + d * SSD_REP + h, lanes] = 1.0
    return jnp.asarray(sel, BF16)


def _pool_constants(tm):
    t = np.arange(tm)
    col = t % GRID_W
    wins, invs = [], []
    for w in POOL_WINDOWS:
        start = np.clip(col - w // 2, 0, GRID_W)
        end = np.clip(col + w - w // 2, 0, GRID_W)
        same_row = (t[:, None] // GRID_W) == (t[None, :] // GRID_W)
        inside = (col[None, :] >= start[:, None]) & (col[None, :] < end[:, None])
        wins.append((same_row & inside).astype(np.float32))
        invs.append((1.0 / (end - start).astype(np.float32))[:, None])
    return jnp.asarray(np.stack(wins), BF16), jnp.asarray(np.stack(invs), F32)


def _merge_kernel(ys_ref, v_ref, win_ref, inv_ref, pw_ref, ps_ref, ws_ref, wp_ref, gs_ref, gp_ref,
                  o_ref, yp_scr):
    slab = win_ref.shape[1]
    for k in range(len(POOL_WINDOWS)):
        cols = slice(k * POOL_GW, (k + 1) * POOL_GW)
        parts = []
        for r in range(0, v_ref.shape[0], slab):
            v = v_ref[r:r + slab, cols]
            parts.append((_dot(win_ref[k], v) * inv_ref[k] - v.astype(F32)).astype(BF16))
        p = jnp.concatenate(parts, axis=0)
        yp_scr[:, cols] = (_dot(p, pw_ref[k]) * ps_ref[:, cols]).astype(BF16)
    a = jax.nn.sigmoid(gs_ref[...].astype(F32)) * _dot(ys_ref[...], ws_ref[...])
    b = jax.nn.sigmoid(gp_ref[...].astype(F32)) * _dot(yp_scr[...], wp_ref[...])
    o_ref[...] = (a + b).astype(o_ref.dtype)


def _merge(y_ssd, pool_w, pool_scale, w_s, w_p, proj2d, tm=512, slab=128):
    win, inv = _pool_constants(slab)
    n_win = len(POOL_WINDOWS)
    once = pl.Buffered(1)
    wide = lambda col: pl.BlockSpec((tm, D_MODEL), lambda i: (i, col))
    return pl.pallas_call(
        _merge_kernel,
        grid=(TOKENS // tm,),
        in_specs=[wide(0),
                  wide(COL_V // D_MODEL),
                  pl.BlockSpec((n_win, slab, slab), lambda i: (0, 0, 0), pipeline_mode=once),
                  pl.BlockSpec((n_win, slab, 1), lambda i: (0, 0, 0), pipeline_mode=once),
                  pl.BlockSpec((n_win, POOL_GW, POOL_GW), lambda i: (0, 0, 0), pipeline_mode=once),
                  pl.BlockSpec((1, D_MODEL), lambda i: (0, 0)),
                  pl.BlockSpec((D_MODEL, D_MODEL), lambda i: (0, 0), pipeline_mode=once),
                  pl.BlockSpec((D_MODEL, D_MODEL), lambda i: (0, 0), pipeline_mode=once),
                  wide(COL_GS // D_MODEL),
                  wide(COL_GP // D_MODEL)],
        out_specs=pl.BlockSpec((tm, D_MODEL), lambda i: (i, 0)),
        out_shape=jax.ShapeDtypeStruct((TOKENS, D_MODEL), BF16),
        scratch_shapes=[pltpu.VMEM((tm, D_MODEL), BF16)],
        compiler_params=_params(("arbitrary",)),
        name="merge",
    )(y_ssd, proj2d, win, inv, pool_w, pool_scale, w_s, w_p, proj2d, proj2d)


def _outproj_kernel(m_ref, wo_ref, x_ref, g1_ref, sh_ref, sc_ref, nw_ref, x1_ref, h2_ref):
    x1_ref[...] = x_ref[...] + g1_ref[0] * _dot(m_ref[...], wo_ref[...])
    _modulate_rows(x1_ref, nw_ref[...], sc_ref[0], sh_ref[0], h2_ref, x1_ref.shape[0], rc=128)


def _outproj(merged, w_o, x2d, g1, sh2, sc2, nw2, tm=512):
    tpb = SEQ // tm
    mod_spec = pl.BlockSpec((1, 1, D_MODEL), lambda i: (i // tpb, 0, 0))
    return pl.pallas_call(
        _outproj_kernel,
        grid=(TOKENS // tm,),
        in_specs=[pl.BlockSpec((tm, D_MODEL), lambda i: (i, 0)),
                  pl.BlockSpec((D_MODEL, D_MODEL), lambda i: (0, 0), pipeline_mode=pl.Buffered(1)),
                  pl.BlockSpec((tm, D_MODEL), lambda i: (i, 0)),
                  mod_spec, mod_spec, mod_spec,
                  pl.BlockSpec((1, D_MODEL), lambda i: (0, 0))],
        out_specs=[pl.BlockSpec((tm, D_MODEL), lambda i: (i, 0)),
                   pl.BlockSpec((tm, D_MODEL), lambda i: (i, 0))],
        out_shape=[jax.ShapeDtypeStruct((TOKENS, D_MODEL), F32),
                   jax.ShapeDtypeStruct((TOKENS, D_MODEL), BF16)],
        compiler_params=_params(("arbitrary",)),
        name="outproj",
    )(merged, w_o, x2d, g1, sh2, sc2, nw2)


def _glu_kernel(h_ref, hp_ref, hn_ref, wa_ref, wg_ref, fw_ref, fb_ref, u_ref,
                hext_scr, a_even, g_even, a_odd, g_odd, *, n_chunks, n_steps):
    s = pl.program_id(0)
    tm = h_ref.shape[0]
    cur = jnp.minimum(s, n_steps - 2)
    i = cur // n_chunks

    @pl.when(s == 0)
    def _():
        a_odd[...] = jnp.zeros(a_odd.shape, F32)
        g_odd[...] = jnp.zeros(g_odd.shape, F32)

    @pl.when(cur % n_chunks == 0)
    def _():
        has_prev = ((i * tm) % SEQ) != 0
        has_next = (((i + 1) * tm) % SEQ) != 0
        hext_scr[0:GRID_W, :] = jnp.where(has_prev, hp_ref[...], jnp.zeros_like(hp_ref[...]))
        hext_scr[GRID_W:GRID_W + tm, :] = h_ref[...]
        hext_scr[GRID_W + tm:, :] = jnp.where(has_next, hn_ref[...], jnp.zeros_like(hn_ref[...]))

    def step(wr_a, wr_g, rd_a, rd_g):
        fw = 0.5 * fw_ref[...]
        fb = 0.5 * fb_ref[...]
        for r in range(0, tm, GLU_ROWS):
            hv = (rd_g[r:r + GLU_ROWS, :] * fw[0:1, :] + fb
                  + rd_g[r + GRID_W:r + GRID_W + GLU_ROWS, :] * fw[1:2, :]
                  + rd_g[r + 2 * GRID_W:r + 2 * GRID_W + GLU_ROWS, :] * fw[2:3, :])
            gelu = hv * (1.0 + lax.erf(hv * np.float32(np.sqrt(2.0))))
            u_ref[r:r + GLU_ROWS, :] = (gelu * rd_a[r:r + GLU_ROWS, :]).astype(u_ref.dtype)
        for r in range(0, tm, DOT_ROWS):
            wr_a[r:r + DOT_ROWS, :] = _dot(hext_scr[GRID_W + r:GRID_W + r + DOT_ROWS, :], wa_ref[...])
        for r in range(0, tm + 2 * GRID_W, DOT_ROWS):
            n = min(DOT_ROWS, tm + 2 * GRID_W - r)
            wr_g[r:r + n, :] = _dot(hext_scr[r:r + n, :], wg_ref[...])

    @pl.when(s % 2 == 0)
    def _():
        step(a_even, g_even, a_odd, g_odd)

    @pl.when(s % 2 == 1)
    def _():
        step(a_odd, g_odd, a_even, g_even)


def _glu(h2, w_up, ffn_conv_w, ffn_conv_b, tm=2048, tn=GLU_TN):
    n_chunks = D_FF // tn
    n_tiles = TOKENS // tm
    n_steps = n_tiles * n_chunks + 1
    hb = tm // GRID_W
    n_halo = TOKENS // GRID_W

    def cur(s):
        return jnp.minimum(s, n_steps - 2)

    def prev(s):
        return jnp.maximum(s - 1, 0)

    return pl.pallas_call(
        functools.partial(_glu_kernel, n_chunks=n_chunks, n_steps=n_steps),
        grid=(n_steps,),
        in_specs=[pl.BlockSpec((tm, D_MODEL), lambda s: (cur(s) // n_chunks, 0),
                               pipeline_mode=pl.Buffered(1)),
                  pl.BlockSpec((GRID_W, D_MODEL),
                               lambda s: (jnp.maximum((cur(s) // n_chunks) * hb - 1, 0), 0)),
                  pl.BlockSpec((GRID_W, D_MODEL),
                               lambda s: (jnp.minimum((cur(s) // n_chunks + 1) * hb, n_halo - 1), 0)),
                  pl.BlockSpec((D_MODEL, tn), lambda s: (0, cur(s) % n_chunks)),
                  pl.BlockSpec((D_MODEL, tn), lambda s: (0, n_chunks + cur(s) % n_chunks)),
                  pl.BlockSpec((3, tn), lambda s: (0, prev(s) % n_chunks)),
                  pl.BlockSpec((1, tn), lambda s: (0, prev(s) % n_chunks))],
        out_specs=pl.BlockSpec((tm, tn), lambda s: (prev(s) // n_chunks, prev(s) % n_chunks)),
        out_shape=jax.ShapeDtypeStruct((TOKENS, D_FF), BF16),
        scratch_shapes=[pltpu.VMEM((tm + 2 * GRID_W, D_MODEL), BF16),
                        pltpu.VMEM((tm, tn), F32), pltpu.VMEM((tm + 2 * GRID_W, tn), F32),
                        pltpu.VMEM((tm, tn), F32), pltpu.VMEM((tm + 2 * GRID_W, tn), F32)],
        compiler_params=_params(("arbitrary",)),
        name="glu",
    )(h2, h2, h2, w_up, w_up, ffn_conv_w, ffn_conv_b)


def _down_kernel(u_ref, wd_ref, x1_ref, g2_ref, nw_ref, o_ref):
    x2 = x1_ref[...] + g2_ref[0] * _dot(u_ref[...], wd_ref[...])
    ms = jnp.mean(x2 * x2, axis=-1, keepdims=True)
    o_ref[...] = x2 * lax.rsqrt(ms + EPS) * nw_ref[...]


def _down(u, w_down, x1, g2, fnw, tm=512):
    tpb = SEQ // tm
    return pl.pallas_call(
        _down_kernel,
        grid=(TOKENS // tm,),
        in_specs=[pl.BlockSpec((tm, D_FF), lambda i: (i, 0)),
                  pl.BlockSpec((D_FF, D_MODEL), lambda i: (0, 0), pipeline_mode=pl.Buffered(1)),
                  pl.BlockSpec((tm, D_MODEL), lambda i: (i, 0)),
                  pl.BlockSpec((1, 1, D_MODEL), lambda i: (i // tpb, 0, 0)),
                  pl.BlockSpec((1, D_MODEL), lambda i: (0, 0))],
        out_specs=pl.BlockSpec((tm, D_MODEL), lambda i: (i, 0)),
        out_shape=jax.ShapeDtypeStruct((TOKENS, D_MODEL), F32),
        compiler_params=_params(("arbitrary",)),
        name="down",
    )(u, w_down, x1, g2, fnw)


def _dt_rows(dt_raw, batch, n_chunks):
    d = dt_raw[:, :2 * SSD_HEADS].reshape(batch, n_chunks, CHUNK, 2, SSD_GROUPS, SSD_REP)
    d = jnp.transpose(d, (4, 0, 1, 3, 5, 2))
    return d.reshape(SSD_GROUPS, batch, n_chunks, 2 * SSD_REP, CHUNK)


def _group_rows(p):
    p = p.reshape(2, SSD_GROUPS, SSD_REP)
    return jnp.transpose(p, (1, 0, 2)).reshape(SSD_GROUPS, 2 * SSD_REP, 1)


def kernel(x, c, ctx, c_ctx, w_ada, b_ada, norm1_w, w_in, ssd_conv_w, ssd_conv_b, dt_bias, a_log,
           d_skip, ssd_norm_w, w_ssd_out, pool_w, pool_scale, w_pool_out, w_o, norm2_w, w_up,
           ffn_conv_w, ffn_conv_b, w_down, final_norm_w):
    assert w_in.shape[0] == 1, "single-layer block only"
    d = D_MODEL
    x2d = x.reshape(TOKENS, d)
    ctx2d = ctx.reshape(BATCH * CTX_LEN, d)

    cs = jnp.zeros((8, d), F32).at[:BATCH].set(c).at[BATCH].set(c_ctx)
    mod = _ada(cs, w_ada[0], b_ada)
    sh1, sc1, g1, sh2, sc2, g2 = [mod[:BATCH, k * d:(k + 1) * d].reshape(BATCH, 1, d)
                                  for k in range(6)]
    csh1 = mod[BATCH:BATCH + 1, 0:d].reshape(1, 1, d)
    csc1 = mod[BATCH:BATCH + 1, d:2 * d].reshape(1, 1, d)

    w_in0 = w_in[0]
    n_dt = 2 * SSD_HEADS
    w_cat, w_dt = _drop_leading_cols(jnp.transpose(w_in0), n_dt)
    w_s = w_ssd_out[0].astype(BF16)
    w_p = w_pool_out[0].astype(BF16)
    w_ob = w_o[0].astype(BF16)
    w_upb = w_up[0].astype(BF16)
    w_db = w_down[0].astype(BF16)
    pool_wb = pool_w[0].astype(BF16)

    proj, dt_raw = _inproj(x2d, sh1, sc1, norm1_w, w_cat, w_dt, n_cols=N_PROJ, rows_per_mod=SEQ)
    cproj, cdt_raw = _inproj(ctx2d, csh1, csc1, norm1_w, w_cat, w_dt, n_cols=N_CTX_PROJ,
                             rows_per_mod=BATCH * CTX_LEN)

    dtr = _dt_rows(dt_raw, BATCH, N_CHUNKS)
    cdtr = _dt_rows(cdt_raw, BATCH, N_CTX_CHUNKS)
    dsk = jnp.repeat(jnp.transpose(d_skip[0].reshape(2, SSD_GROUPS, SSD_REP), (1, 0, 2)),
                     SSD_HEAD_DIM, axis=-1)
    idx = np.arange(CHUNK)
    tri = jnp.asarray(np.stack([idx[:, None] <= idx[None, :], idx[:, None] >= idx[None, :]])
                      .astype(np.float32), BF16)

    y_ssd = _ssd(proj.reshape(BATCH, SEQ, N_PROJ), dtr,
                 cproj.reshape(BATCH, CTX_LEN, N_CTX_PROJ), cdtr,
                 ssd_conv_w[0], ssd_conv_b, _group_rows(dt_bias[0]), _group_rows(a_log[0]),
                 dsk, ssd_norm_w, tri)
    merged = _merge(y_ssd.reshape(TOKENS, d), pool_wb, pool_scale, w_s, w_p, proj)
    x1, h2 = _outproj(merged, w_ob, x2d, g1, sh2, sc2, norm2_w)
    u = _glu(h2, w_upb, ffn_conv_w[0], ffn_conv_b)
    out = _down(u, w_db, x1, g2, final_norm_w.reshape(1, d))
    return out.reshape(BATCH, SEQ, d)
```
